```python
import jax, jax.numpy as jnp
from jax import lax
import numpy as np


D_MODEL = 2048
BATCH = 1
SEQ = 8192
DEPTH = 1

N_META = 16
EPS = 1e-6

N_HEADS = 16
QK_NOPE_DIM = 128
QK_ROPE_DIM = 64
QK_HEAD_DIM = QK_NOPE_DIM + QK_ROPE_DIM
V_HEAD_DIM = 128
Q_LORA_RANK = 512
KV_LORA_RANK = 512
ROPE_THETA = 10000.0
Q_BLOCK = 128
MLA_WIDTH = N_HEADS * V_HEAD_DIM

POOL_WINDOWS = (2, 4, 8, 16)
POOL_GROUPS = len(POOL_WINDOWS)
POOL_WIDTH = D_MODEL // 2
POOL_GROUP_DIM = POOL_WIDTH // POOL_GROUPS

IN_COLS = Q_LORA_RANK + KV_LORA_RANK + QK_ROPE_DIM + POOL_WIDTH + 2 * D_MODEL
SPLIT_POINTS = (Q_LORA_RANK,
                Q_LORA_RANK + KV_LORA_RANK,
                Q_LORA_RANK + KV_LORA_RANK + QK_ROPE_DIM,
                Q_LORA_RANK + KV_LORA_RANK + QK_ROPE_DIM + POOL_WIDTH,
                Q_LORA_RANK + KV_LORA_RANK + QK_ROPE_DIM + POOL_WIDTH + D_MODEL)

N_EXPERT_GROUPS = 8
EXPERTS_PER_GROUP = 8
N_EXPERTS = N_EXPERT_GROUPS * EXPERTS_PER_GROUP
TOP_K_IN_GROUP = 2
EXPERT_FF = D_MODEL // 4
EXPERT_BLOCK = 128

kernel_name = 'hybrid_mla_pool_hmoe'


def rms_norm(x, w):
    xf = x.astype(jnp.float32)
    y = xf * lax.rsqrt(jnp.mean(xf * xf, axis=-1, keepdims=True) + EPS)
    return (y * w.astype(jnp.float32)).astype(x.dtype)


def rope_tables(length):
    inv = 1.0 / (ROPE_THETA ** (jnp.arange(0, QK_ROPE_DIM, 2, dtype=jnp.float32) / QK_ROPE_DIM))
    ang = jnp.arange(length, dtype=jnp.float32)[:, None] * inv[None, :]
    return jnp.cos(ang), jnp.sin(ang)


def apply_rope(x, cos, sin):
    half = QK_ROPE_DIM // 2
    xf = x.astype(jnp.float32)
    x1, x2 = xf[..., :half], xf[..., half:]
    return jnp.concatenate([x1 * cos - x2 * sin, x1 * sin + x2 * cos], axis=-1).astype(x.dtype)


def mla_attention(c_q, c_kv, k_rope, q_norm_w, w_uq, kv_norm_w, w_ukv, cos, sin):
    B, L, _ = c_q.shape
    q = (rms_norm(c_q, q_norm_w) @ w_uq).reshape(B, L, N_HEADS, QK_HEAD_DIM)
    q_nope, q_pe = q[..., :QK_NOPE_DIM], q[..., QK_NOPE_DIM:]
    q_pe = apply_rope(q_pe, cos[None, :, None, :], sin[None, :, None, :])
    kv = (rms_norm(c_kv, kv_norm_w) @ w_ukv).reshape(B, L, N_HEADS, QK_NOPE_DIM + V_HEAD_DIM)
    k_nope, v = kv[..., :QK_NOPE_DIM], kv[..., QK_NOPE_DIM:]
    k_pe = apply_rope(k_rope, cos[None], sin[None])
    q = jnp.concatenate([q_nope, q_pe], axis=-1)
    k = jnp.concatenate([k_nope, jnp.broadcast_to(k_pe[:, :, None, :], (B, L, N_HEADS, QK_ROPE_DIM))], axis=-1)
    Lp = -(-L // Q_BLOCK) * Q_BLOCK
    pad = ((0, 0), (0, Lp - L), (0, 0), (0, 0))
    q, k, v = jnp.pad(q, pad), jnp.pad(k, pad), jnp.pad(v, pad)
    n_blocks = Lp // Q_BLOCK
    q_blocks = q.reshape(B, n_blocks, Q_BLOCK, N_HEADS, QK_HEAD_DIM).transpose(1, 0, 2, 3, 4)
    kpos = jnp.arange(Lp)
    scale = QK_HEAD_DIM ** -0.5

    def attend_block(args):
        q_blk, blk = args
        s = jnp.einsum('bqhd,bkhd->bhqk', q_blk, k).astype(jnp.float32) * scale
        qpos = blk * Q_BLOCK + jnp.arange(Q_BLOCK)
        causal = kpos[None, :] <= qpos[:, None]
        s = jnp.where(causal[None, None], s, jnp.finfo(jnp.float32).min)
        p = jax.nn.softmax(s, axis=-1).astype(v.dtype)
        return jnp.einsum('bhqk,bkhd->bqhd', p, v)

    o = lax.map(attend_block, (q_blocks, jnp.arange(n_blocks)))
    return o.transpose(1, 0, 2, 3, 4).reshape(B, Lp, MLA_WIDTH)[:, :L]


def multiscale_pool(u, pool_w, pool_scale):
    B, L, C = u.shape
    uf = u.astype(jnp.float32)
    csum = jnp.concatenate([jnp.zeros((B, 1, C), jnp.float32), lax.cumsum(uf, axis=1)], axis=1)
    t = jnp.arange(L)
    outs = []
    for g, w in enumerate(POOL_WINDOWS):
        lo, hi = g * POOL_GROUP_DIM, (g + 1) * POOL_GROUP_DIM
        start = jnp.maximum(t + 1 - w, 0)
        total = csum[:, 1:, lo:hi] - csum[:, start, lo:hi]
        count = (t + 1 - start).astype(jnp.float32)
        outs.append(total / count[None, :, None] - uf[:, :, lo:hi])
    pooled = jnp.stack(outs, axis=2).astype(u.dtype)
    mixed = jnp.einsum('blgc,gcd->blgd', pooled, pool_w).reshape(B, L, POOL_WIDTH)
    return mixed * pool_scale


def hybrid_mixer(a, w_in, q_norm_w, w_uq, kv_norm_w, w_ukv, w_o_mla, pool_w, pool_scale,
                 w_pool_out, w_out, cos, sin):
    proj = a @ w_in
    c_q, c_kv, k_rope, u_pool, gate_mla, gate_pool = jnp.split(proj, SPLIT_POINTS, axis=-1)
    y_mla = mla_attention(c_q, c_kv, k_rope, q_norm_w, w_uq, kv_norm_w, w_ukv, cos, sin) @ w_o_mla
    y_pool = multiscale_pool(u_pool, pool_w, pool_scale) @ w_pool_out
    merged = jax.nn.sigmoid(gate_mla) * y_mla + jax.nn.sigmoid(gate_pool) * y_pool
    return merged @ w_out


def routed_experts(xt, expert_ids, weights, w_gate, w_up, w_down):
    T, D = xt.shape
    A = T * TOP_K_IN_GROUP
    n_blocks = -(-(A + N_EXPERTS * (EXPERT_BLOCK - 1)) // EXPERT_BLOCK)
    cap = n_blocks * EXPERT_BLOCK
    flat_e = expert_ids.reshape(A)
    order = jnp.argsort(flat_e)
    sorted_e = flat_e[order]
    counts = jnp.bincount(flat_e, length=N_EXPERTS)
    padded = (counts + EXPERT_BLOCK - 1) // EXPERT_BLOCK * EXPERT_BLOCK
    seg_start = jnp.cumsum(counts) - counts
    pad_end = jnp.cumsum(padded)
    pad_start = pad_end - padded
    dest = pad_start[sorted_e] + (jnp.arange(A) - seg_start[sorted_e])
    slot_token = jnp.full((cap,), T, jnp.int32).at[dest].set((order // TOP_K_IN_GROUP).astype(jnp.int32))
    slot_weight = jnp.zeros((cap,), xt.dtype).at[dest].set(weights.reshape(A)[order].astype(xt.dtype))
    block_expert = jnp.minimum(jnp.searchsorted(pad_end, jnp.arange(n_blocks) * EXPERT_BLOCK, side='right'),
                               N_EXPERTS - 1)
    x_pad = jnp.concatenate([xt, jnp.zeros((1, D), xt.dtype)], axis=0)
    xb = x_pad[slot_token].reshape(n_blocks, EXPERT_BLOCK, D)

    def expert_block(args):
        x_blk, e = args
        hdn = jax.nn.silu(x_blk @ w_gate[e]) * (x_blk @ w_up[e])
        return hdn @ w_down[e]

    yb = lax.map(expert_block, (xb, block_expert)).reshape(cap, D)
    out = jnp.zeros((T + 1, D), xt.dtype).at[slot_token].add(yb * slot_weight[:, None])
    return out[:T]


def hierarchical_moe(b, w_router_group, b_router_group, w_router_expert, b_router_expert,
                     w_exp_gate, w_exp_up, w_exp_down):
    B, L, D = b.shape
    T = B * L
    xt = b.reshape(T, D)
    g_logits = (xt @ w_router_group).astype(jnp.float32) + b_router_group.astype(jnp.float32)
    p_group = jax.nn.softmax(g_logits, axis=-1)
    _, g_idx = lax.top_k(g_logits, 1)
    p_g = jnp.take_along_axis(p_group, g_idx, axis=-1)
    e_logits = ((xt @ w_router_expert).astype(jnp.float32) + b_router_expert.astype(jnp.float32)).reshape(
        T, N_EXPERT_GROUPS, EXPERTS_PER_GROUP)
    e_in_group = jnp.take_along_axis(e_logits, g_idx[:, :, None], axis=1)[:, 0]
    e_val, e_idx = lax.top_k(e_in_group, TOP_K_IN_GROUP)
    weights = p_g * jax.nn.softmax(e_val, axis=-1)
    expert_ids = g_idx * EXPERTS_PER_GROUP + e_idx
    y = routed_experts(xt, expert_ids, weights, w_exp_gate, w_exp_up, w_exp_down)
    return y.reshape(B, L, D)


def setup_inputs(seed: int = 0) -> dict:
    key = jax.random.key(seed)
    ks = jax.random.split(key, 24)
    f32 = jnp.float32

    def dense(k, shape, fan_in):
        return jax.random.normal(k, shape, f32) * fan_in ** -0.5

    def gain(k, shape, noise=0.05):
        return 1.0 + noise * jax.random.normal(k, shape, f32)

    return {
        'x': jax.random.normal(ks[0], (BATCH, SEQ, D_MODEL), f32),
        'meta_tokens': jax.random.normal(ks[1], (N_META, D_MODEL), f32),
        'norm_mix_w': gain(ks[2], (DEPTH, D_MODEL)),
        'w_in': dense(ks[3], (DEPTH, D_MODEL, IN_COLS), D_MODEL),
        'q_norm_w': gain(ks[4], (DEPTH, Q_LORA_RANK)),
        'w_uq': dense(ks[5], (DEPTH, Q_LORA_RANK, N_HEADS * QK_HEAD_DIM), Q_LORA_RANK),
        'kv_norm_w': gain(ks[6], (DEPTH, KV_LORA_RANK)),
        'w_ukv': dense(ks[7], (DEPTH, KV_LORA_RANK, N_HEADS * (QK_NOPE_DIM + V_HEAD_DIM)), KV_LORA_RANK),
        'w_o_mla': dense(ks[8], (DEPTH, MLA_WIDTH, D_MODEL), MLA_WIDTH),
        'pool_w': dense(ks[9], (DEPTH, POOL_GROUPS, POOL_GROUP_DIM, POOL_GROUP_DIM), POOL_GROUP_DIM),
        'pool_scale': gain(ks[10], (DEPTH, POOL_WIDTH), 0.1),
        'w_pool_out': dense(ks[11], (DEPTH, POOL_WIDTH, D_MODEL), POOL_WIDTH),
        'w_out': dense(ks[12], (DEPTH, D_MODEL, D_MODEL), D_MODEL),
        'norm_ffn_w': gain(ks[13], (DEPTH, D_MODEL)),
        'w_router_group': dense(ks[14], (DEPTH, D_MODEL, N_EXPERT_GROUPS), D_MODEL),
        'b_router_group': 0.01 * jax.random.normal(ks[15], (DEPTH, N_EXPERT_GROUPS), f32),
        'w_router_expert': dense(ks[16], (DEPTH, D_MODEL, N_EXPERTS), D_MODEL),
        'b_router_expert': 0.01 * jax.random.normal(ks[17], (DEPTH, N_EXPERTS), f32),
        'w_exp_gate': dense(ks[18], (DEPTH, N_EXPERTS, D_MODEL, EXPERT_FF), D_MODEL),
        'w_exp_up': dense(ks[19], (DEPTH, N_EXPERTS, D_MODEL, EXPERT_FF), D_MODEL),
        'w_exp_down': dense(ks[20], (DEPTH, N_EXPERTS, EXPERT_FF, D_MODEL), EXPERT_FF),
        'final_norm_w': gain(ks[21], (D_MODEL,)),
    }


def reference(x, meta_tokens, norm_mix_w, w_in, q_norm_w, w_uq, kv_norm_w, w_ukv, w_o_mla,
              pool_w, pool_scale, w_pool_out, w_out, norm_ffn_w, w_router_group, b_router_group,
              w_router_expert, b_router_expert, w_exp_gate, w_exp_up, w_exp_down, final_norm_w):
    B = x.shape[0]
    meta = jnp.broadcast_to(meta_tokens[None].astype(x.dtype), (B, N_META, D_MODEL))
    h = jnp.concatenate([meta, x], axis=1)
    cos, sin = rope_tables(h.shape[1])
    for layer in range(DEPTH):
        a = rms_norm(h, norm_mix_w[layer])
        h = h + hybrid_mixer(a, w_in[layer], q_norm_w[layer], w_uq[layer], kv_norm_w[layer],
                             w_ukv[layer], w_o_mla[layer], pool_w[layer], pool_scale[layer],
                             w_pool_out[layer], w_out[layer], cos, sin)
        bn = rms_norm(h, norm_ffn_w[layer])
        h = h + hierarchical_moe(bn, w_router_group[layer], b_router_group[layer],
                                 w_router_expert[layer], b_router_expert[layer],
                                 w_exp_gate[layer], w_exp_up[layer], w_exp_down[layer])
    h = rms_norm(h, final_norm_w)
    return h[:, N_META:]
```

```python
import functools

import jax
import jax.numpy as jnp
from jax import lax
from jax.experimental import pallas as pl
from jax.experimental.pallas import tpu as pltpu

F32 = jnp.float32
BF16 = jnp.bfloat16

D_MODEL = 2048
N_META = 16
EPS = 1e-6
N_HEADS = 16
QK_NOPE = 128
QK_ROPE = 64
QK_HEAD = QK_NOPE + QK_ROPE
V_HEAD = 128
Q_LORA = 512
KV_LORA = 512
ROPE_THETA = 10000.0
POOL_WINDOWS = (2, 4, 8, 16)
POOL_WIDTH = 1024
POOL_GROUP_DIM = 256
N_GROUPS = 8
EXPERTS_PER_GROUP = 8
N_EXPERTS = 64
TOP_K = 2
EXPERT_FF = 512

LANES = 128
QK_PAD = 256
MAIN_COLS = Q_LORA + KV_LORA + POOL_WIDTH + 2 * D_MODEL
COL_CQ, COL_CKV, COL_POOL, COL_GM, COL_GP = 0, 512, 1024, 2048, 4096
ROUTE_BLOCK = 128
NEG_BIG = -1e30

VMEM_LIMIT = 56 * 1024 * 1024


def _params(sem):
    return pltpu.CompilerParams(dimension_semantics=sem, vmem_limit_bytes=VMEM_LIMIT)


def _rms(x, w):
    return x * lax.rsqrt(jnp.mean(x * x, axis=-1, keepdims=True) + EPS) * w


def _rope128(v, c, s1, s2):
    return v * c + pltpu.roll(v, 96, 1) * s1 + pltpu.roll(v, 32, 1) * s2


def _inproj_kernel(x_ref, nw_ref, w_ref, wr_ref, o_ref, kr_ref, a_sc):
    @pl.when(pl.program_id(1) == 0)
    def _():
        a = _rms(x_ref[...], nw_ref[...]).astype(BF16)
        a_sc[...] = a
        kr_ref[...] = jnp.dot(a, wr_ref[...], preferred_element_type=F32)

    o_ref[...] = jnp.dot(a_sc[...], w_ref[...], preferred_element_type=F32)


def _inproj(x, norm_w, w_main, w_rope, tm, tn):
    m, k = x.shape
    n = w_main.shape[1]
    return pl.pallas_call(
        _inproj_kernel,
        grid=(m // tm, n // tn),
        in_specs=[
            pl.BlockSpec((tm, k), lambda i, j: (i, 0)),
            pl.BlockSpec((1, k), lambda i, j: (0, 0)),
            pl.BlockSpec((k, tn), lambda i, j: (0, j)),
            pl.BlockSpec((k, LANES), lambda i, j: (0, 0)),
        ],
        out_specs=[
            pl.BlockSpec((tm, tn), lambda i, j: (i, j)),
            pl.BlockSpec((tm, LANES), lambda i, j: (i, 0)),
        ],
        out_shape=[jax.ShapeDtypeStruct((m, n), F32), jax.ShapeDtypeStruct((m, LANES), F32)],
        scratch_shapes=[pltpu.VMEM((tm, k), BF16)],
        compiler_params=_params(("parallel", "arbitrary")),
        name="inproj",
    )(x, norm_w, w_main, w_rope)


def _qproj_kernel(cq_ref, nw_ref, w_ref, c_ref, s1_ref, s2_ref, q_ref, *, scale):
    a = _rms(cq_ref[...], nw_ref[...]).astype(BF16)
    c, s1, s2 = c_ref[...], s1_ref[...], s2_ref[...]
    for h in range(N_HEADS):
        qh = jnp.dot(a, w_ref[:, h * QK_PAD:(h + 1) * QK_PAD], preferred_element_type=F32) * scale
        q_ref[:, h * QK_PAD:h * QK_PAD + LANES] = qh[:, :LANES].astype(BF16)
        q_ref[:, h * QK_PAD + LANES:(h + 1) * QK_PAD] = _rope128(qh[:, LANES:], c, s1, s2).astype(BF16)


def _qproj(proj, norm_w, w_q, c, s1, s2, tm):
    m = proj.shape[0]
    n = N_HEADS * QK_PAD
    tab = pl.BlockSpec((tm, LANES), lambda i: (i, 0))
    return pl.pallas_call(
        functools.partial(_qproj_kernel, scale=QK_HEAD ** -0.5),
        grid=(m // tm,),
        in_specs=[
            pl.BlockSpec((tm, Q_LORA), lambda i: (i, COL_CQ // Q_LORA)),
            pl.BlockSpec((1, Q_LORA), lambda i: (0, 0)),
            pl.BlockSpec((Q_LORA, n), lambda i: (0, 0)),
            tab, tab, tab,
        ],
        out_specs=pl.BlockSpec((tm, n), lambda i: (i, 0)),
        out_shape=jax.ShapeDtypeStruct((m, n), BF16),
        compiler_params=_params(("parallel",)),
        name="qproj",
    )(proj, norm_w, w_q, c, s1, s2)


def _kvproj_kernel(ckv_ref, kr_ref, nw_ref, wk_ref, wv_ref, c_ref, s1_ref, s2_ref, k_ref, v_ref):
    a = _rms(ckv_ref[...], nw_ref[...]).astype(BF16)
    kpe = _rope128(kr_ref[...], c_ref[...], s1_ref[...], s2_ref[...]).astype(BF16)
    v_ref[...] = jnp.dot(a, wv_ref[...], preferred_element_type=F32).astype(BF16)
    for h2 in range(N_HEADS // 2):
        kn = jnp.dot(a, wk_ref[:, h2 * 256:(h2 + 1) * 256], preferred_element_type=F32).astype(BF16)
        for d in range(2):
            h = 2 * h2 + d
            k_ref[:, h * QK_PAD:h * QK_PAD + LANES] = kn[:, d * LANES:(d + 1) * LANES]
            k_ref[:, h * QK_PAD + LANES:(h + 1) * QK_PAD] = kpe


def _kvproj(proj, krope, norm_w, w_k, w_v, c, s1, s2, tm):
    m = proj.shape[0]
    tab = pl.BlockSpec((tm, LANES), lambda i: (i, 0))
    return pl.pallas_call(
        _kvproj_kernel,
        grid=(m // tm,),
        in_specs=[
            pl.BlockSpec((tm, KV_LORA), lambda i: (i, COL_CKV // KV_LORA)),
            tab,
            pl.BlockSpec((1, KV_LORA), lambda i: (0, 0)),
            pl.BlockSpec((KV_LORA, N_HEADS * QK_NOPE), lambda i: (0, 0)),
            pl.BlockSpec((KV_LORA, N_HEADS * V_HEAD), lambda i: (0, 0)),
            tab, tab, tab,
        ],
        out_specs=[
            pl.BlockSpec((tm, N_HEADS * QK_PAD), lambda i: (i, 0)),
            pl.BlockSpec((tm, N_HEADS * V_HEAD), lambda i: (i, 0)),
        ],
        out_shape=[jax.ShapeDtypeStruct((m, N_HEADS * QK_PAD), BF16),
                   jax.ShapeDtypeStruct((m, N_HEADS * V_HEAD), BF16)],
        compiler_params=_params(("parallel",)),
        name="kvproj",
    )(proj, krope, norm_w, w_k, w_v, c, s1, s2)


def _attn_kernel(q_ref, k_ref, v_ref, km_ref, vm_ref, o_ref, *, tq, tk):
    i = pl.program_id(1)
    q = q_ref[...]
    nt = (((1,), (1,)), ((), ()))

    s = lax.dot_general(q, km_ref[...], nt, preferred_element_type=F32)
    m = jnp.max(s, axis=-1, keepdims=True)
    p = jnp.exp(s - m)
    l = jnp.sum(p, axis=-1, keepdims=True)
    acc = jnp.dot(p.astype(BF16), vm_ref[...], preferred_element_type=F32)

    def step(start, carry, mask):
        m, l, acc = carry
        k = k_ref[pl.ds(start, tk), :]
        v = v_ref[pl.ds(start, tk), :]
        s = lax.dot_general(q, k, nt, preferred_element_type=F32)
        if mask is not None:
            s = jnp.where(mask, s, NEG_BIG)
        m_new = jnp.maximum(m, jnp.max(s, axis=-1, keepdims=True))
        alpha = jnp.exp(m - m_new)
        p = jnp.exp(s - m_new)
        l = alpha * l + jnp.sum(p, axis=-1, keepdims=True)
        acc = alpha * acc + jnp.dot(p.astype(BF16), v, preferred_element_type=F32)
        return m_new, l, acc

    n_sub = tq // tk
    carry = lax.fori_loop(
        0, i * n_sub, lambda j, c: step(pl.multiple_of(j * tk, tk), c, None), (m, l, acc))
    row = lax.broadcasted_iota(jnp.int32, (tq, tk), 0)
    col = lax.broadcasted_iota(jnp.int32, (tq, tk), 1)
    for d in range(n_sub):
        carry = step(pl.multiple_of(i * tq + d * tk, tk), carry, col + d * tk <= row)
    m, l, acc = carry
    o_ref[...] = (acc / l).astype(o_ref.dtype)


def _attention(q, k, v, k_meta, v_meta, tq, tk):
    n = q.shape[0]
    return pl.pallas_call(
        functools.partial(_attn_kernel, tq=tq, tk=tk),
        grid=(N_HEADS, n // tq),
        in_specs=[
            pl.BlockSpec((tq, QK_PAD), lambda h, i: (i, h)),
            pl.BlockSpec((n, QK_PAD), lambda h, i: (0, h)),
            pl.BlockSpec((n, V_HEAD), lambda h, i: (0, h)),
            pl.BlockSpec((N_META, QK_PAD), lambda h, i: (0, h)),
            pl.BlockSpec((N_META, V_HEAD), lambda h, i: (0, h)),
        ],
        out_specs=pl.BlockSpec((tq, V_HEAD), lambda h, i: (i, h)),
        out_shape=jax.ShapeDtypeStruct((n, N_HEADS * V_HEAD), BF16),
        compiler_params=_params(("parallel", "arbitrary")),
        name="attention",
    )(q, k, v, k_meta, v_meta)


def _merge_kernel(o_ref, u_ref, halo_ref, um_ref, gm_ref, gp_ref, wo_ref, pw_ref, ps_ref, wpo_ref,
                  out_ref, ext_sc, *, tm):
    i = pl.program_id(0)
    ext_sc[0:N_META, :] = jnp.where(i == 0, um_ref[...], halo_ref[...])
    ext_sc[N_META:, :] = u_ref[...]
    y_pool = jnp.zeros((tm, D_MODEL), F32)
    for g, w in enumerate(POOL_WINDOWS):
        lo, hi = g * POOL_GROUP_DIM, (g + 1) * POOL_GROUP_DIM
        u = ext_sc[N_META:, lo:hi]
        tot = u
        for d in range(1, w):
            tot = tot + ext_sc[N_META - d:N_META - d + tm, lo:hi]
        pooled = (tot * (1.0 / w) - u).astype(BF16)
        mixed = jnp.dot(pooled, pw_ref[g], preferred_element_type=F32) * ps_ref[:, lo:hi]
        y_pool = y_pool + jnp.dot(mixed.astype(BF16), wpo_ref[lo:hi, :], preferred_element_type=F32)
    y_mla = jnp.dot(o_ref[...], wo_ref[...], preferred_element_type=F32)
    merged = jax.nn.sigmoid(gm_ref[...]) * y_mla + jax.nn.sigmoid(gp_ref[...]) * y_pool
    out_ref[...] = merged.astype(out_ref.dtype)


def _merge(o, proj, u_meta, w_o, pool_w, pool_scale, w_pool_out, tm):
    n = o.shape[0]
    hb = tm // N_META
    const = lambda i: (0, 0)
    return pl.pallas_call(
        functools.partial(_merge_kernel, tm=tm),
        grid=(n // tm,),
        in_specs=[
            pl.BlockSpec((tm, D_MODEL), lambda i: (i, 0)),
            pl.BlockSpec((tm, POOL_WIDTH), lambda i: (i, COL_POOL // POOL_WIDTH)),
            pl.BlockSpec((N_META, POOL_WIDTH),
                         lambda i: (jnp.maximum(i * hb - 1, 0), COL_POOL // POOL_WIDTH)),
            pl.BlockSpec((N_META, POOL_WIDTH), lambda i: (0, COL_POOL // POOL_WIDTH)),
            pl.BlockSpec((tm, D_MODEL), lambda i: (i, COL_GM // D_MODEL)),
            pl.BlockSpec((tm, D_MODEL), lambda i: (i, COL_GP // D_MODEL)),
            pl.BlockSpec((D_MODEL, D_MODEL), const),
            pl.BlockSpec((len(POOL_WINDOWS), POOL_GROUP_DIM, POOL_GROUP_DIM), lambda i: (0, 0, 0)),
            pl.BlockSpec((1, POOL_WIDTH), const),
            pl.BlockSpec((POOL_WIDTH, D_MODEL), const),
        ],
        out_specs=pl.BlockSpec((tm, D_MODEL), lambda i: (i, 0)),
        out_shape=jax.ShapeDtypeStruct((n, D_MODEL), BF16),
        scratch_shapes=[pltpu.VMEM((tm + N_META, POOL_WIDTH), F32)],
        compiler_params=_params(("arbitrary",)),
        name="merge",
    )(o, proj, proj, u_meta, proj, proj, w_o, pool_w, pool_scale, w_pool_out)


def _route_kernel(mg_ref, x_ref, wout_ref, nw_ref, wr_ref, br_ref, h_ref, bn_ref, ids_ref, wts_ref):
    h = x_ref[...] + jnp.dot(mg_ref[...], wout_ref[...], preferred_element_type=F32)
    h_ref[...] = h
    bn = _rms(h, nw_ref[...])
    bn_ref[...] = bn
    logits = jnp.dot(bn.astype(BF16), wr_ref[...], preferred_element_type=F32) + br_ref[...]
    lane = lax.broadcasted_iota(jnp.int32, logits.shape, 1)
    lane_f = lane.astype(F32)

    def first_max(vals):
        vmax = jnp.max(vals, axis=-1, keepdims=True)
        idx = jnp.min(jnp.where(vals == vmax, lane_f, float(LANES)), axis=-1, keepdims=True)
        return vmax, idx.astype(jnp.int32)

    glog = jnp.where(lane < N_GROUPS, logits, -jnp.inf)
    gmax, gidx = first_max(glog)
    p_g = 1.0 / jnp.sum(jnp.exp(glog - gmax), axis=-1, keepdims=True)
    lo = N_GROUPS + gidx * EXPERTS_PER_GROUP
    elog = jnp.where((lane >= lo) & (lane < lo + EXPERTS_PER_GROUP), logits, -jnp.inf)
    v1, i1 = first_max(elog)
    v2, i2 = first_max(jnp.where(lane == i1, -jnp.inf, elog))
    e2 = jnp.exp(v2 - v1)
    den = 1.0 + e2
    w1 = p_g * (1.0 / den)
    w2 = p_g * (e2 / den)
    ids_ref[...] = jnp.where(lane == 0, i1 - N_GROUPS, jnp.where(lane == 1, i2 - N_GROUPS, 0))
    wts_ref[...] = jnp.where(lane == 0, w1, jnp.where(lane == 1, w2, 0.0))


def _route(merged, x, w_out, norm_w, w_router, b_router, tm):
    n = x.shape[0]
    const = lambda i: (0, 0)
    row = lambda width: pl.BlockSpec((tm, width), lambda i: (i, 0))
    return pl.pallas_call(
        _route_kernel,
        grid=(n // tm,),
        in_specs=[
            row(D_MODEL), row(D_MODEL),
            pl.BlockSpec((D_MODEL, D_MODEL), const),
            pl.BlockSpec((1, D_MODEL), const),
            pl.BlockSpec((D_MODEL, LANES), const),
            pl.BlockSpec((1, LANES), const),
        ],
        out_specs=[row(D_MODEL), row(D_MODEL), row(LANES), row(LANES)],
        out_shape=[jax.ShapeDtypeStruct((n, D_MODEL), F32), jax.ShapeDtypeStruct((n, D_MODEL), F32),
                   jax.ShapeDtypeStruct((n, LANES), jnp.int32), jax.ShapeDtypeStruct((n, LANES), F32)],
        compiler_params=_params(("parallel",)),
        name="route",
    )(merged, x, w_out, norm_w, w_router, b_router)


def _experts_kernel(order_ref, seg_ref, bn_hbm, wg_ref, wu_ref, wd_ref, y_hbm,
                    xbuf, ybuf, wg_sc, wu_sc, wd_sc, gsem, ssem):
    e = pl.program_id(0)
    start = seg_ref[e]
    end = seg_ref[e + 1]
    n_blocks = (end - start + ROUTE_BLOCK - 1) // ROUTE_BLOCK

    @pl.when(end > start)
    def _():
        wg_sc[...] = wg_ref[0].astype(BF16)
        wu_sc[...] = wu_ref[0].astype(BF16)
        wd_sc[...] = wd_ref[0].astype(BF16)

    def gather_copy(r, base):
        pos = jnp.minimum(base + r, end - 1)
        tok = order_ref[pos] // TOP_K
        return pltpu.make_async_copy(bn_hbm.at[pl.ds(tok, 1)], xbuf.at[pl.ds(r, 1)], gsem)

    def scatter_copy(r, base):
        dst = order_ref[jnp.minimum(base + r, end - 1)]
        return pltpu.make_async_copy(ybuf.at[pl.ds(r, 1)], y_hbm.at[pl.ds(dst, 1)], ssem)

    def block(b, _):
        base = start + b * ROUTE_BLOCK
        n_valid = jnp.minimum(end - base, ROUTE_BLOCK)

        def g_start(r, _):
            gather_copy(r, base).start()
            return 0

        def g_wait(r, _):
            gather_copy(r, base).wait()
            return 0

        lax.fori_loop(0, ROUTE_BLOCK, g_start, 0)
        lax.fori_loop(0, ROUTE_BLOCK, g_wait, 0)
        x = xbuf[...].astype(BF16)
        gate = jnp.dot(x, wg_sc[...], preferred_element_type=F32)
        up = jnp.dot(x, wu_sc[...], preferred_element_type=F32)
        hdn = (gate * jax.nn.sigmoid(gate) * up).astype(BF16)
        ybuf[...] = jnp.dot(hdn, wd_sc[...], preferred_element_type=F32)

        def s_start(r, _):
            scatter_copy(r, base).start()
            return 0

        def s_wait(r, _):
            scatter_copy(r, base).wait()
            return 0

        lax.fori_loop(0, n_valid, s_start, 0)
        lax.fori_loop(0, n_valid, s_wait, 0)
        return 0

    lax.fori_loop(0, n_blocks, block, 0)


def _experts(order, seg, bn, w_gate, w_up, w_down):
    n = bn.shape[0]
    a = n * TOP_K
    grid_spec = pltpu.PrefetchScalarGridSpec(
        num_scalar_prefetch=2,
        grid=(N_EXPERTS,),
        in_specs=[
            pl.BlockSpec(memory_space=pl.ANY),
            pl.BlockSpec((1, D_MODEL, EXPERT_FF), lambda e, o, s: (e, 0, 0)),
            pl.BlockSpec((1, D_MODEL, EXPERT_FF), lambda e, o, s: (e, 0, 0)),
            pl.BlockSpec((1, EXPERT_FF, D_MODEL), lambda e, o, s: (e, 0, 0)),
        ],
        out_specs=pl.BlockSpec(memory_space=pl.ANY),
        scratch_shapes=[
            pltpu.VMEM((ROUTE_BLOCK, D_MODEL), F32),
            pltpu.VMEM((ROUTE_BLOCK, D_MODEL), F32),
            pltpu.VMEM((D_MODEL, EXPERT_FF), BF16),
            pltpu.VMEM((D_MODEL, EXPERT_FF), BF16),
            pltpu.VMEM((EXPERT_FF, D_MODEL), BF16),
            pltpu.SemaphoreType.DMA(()),
            pltpu.SemaphoreType.DMA(()),
        ],
    )
    return pl.pallas_call(
        _experts_kernel,
        grid_spec=grid_spec,
        out_shape=jax.ShapeDtypeStruct((a, D_MODEL), F32),
        compiler_params=_params(("arbitrary",)),
        name="experts",
    )(order, seg, bn, w_gate, w_up, w_down)


def _final_kernel(h_ref, y_ref, wts_ref, nw_ref, o_ref):
    w = wts_ref[...]
    moe = y_ref[:, :D_MODEL] * w[:, 0:1] + y_ref[:, D_MODEL:] * w[:, 1:2]
    o_ref[...] = _rms(h_ref[...] + moe, nw_ref[...])


def _final(h, y2, wts, norm_w, tm):
    n = h.shape[0]
    return pl.pallas_call(
        _final_kernel,
        grid=(n // tm,),
        in_specs=[
            pl.BlockSpec((tm, D_MODEL), lambda i: (i, 0)),
            pl.BlockSpec((tm, TOP_K * D_MODEL), lambda i: (i, 0)),
            pl.BlockSpec((tm, LANES), lambda i: (i, 0)),
            pl.BlockSpec((1, D_MODEL), lambda i: (0, 0)),
        ],
        out_specs=pl.BlockSpec((tm, D_MODEL), lambda i: (i, 0)),
        out_shape=jax.ShapeDtypeStruct((n, D_MODEL), F32),
        compiler_params=_params(("parallel",)),
        name="final",
    )(h, y2, wts, norm_w)


def _rope_tables(length):
    inv = 1.0 / (ROPE_THETA ** (jnp.arange(0, QK_ROPE, 2, dtype=F32) / QK_ROPE))
    ang = jnp.arange(length, dtype=F32)[:, None] * inv[None, :]
    cos, sin = jnp.cos(ang), jnp.sin(ang)
    z32 = jnp.zeros_like(cos)
    z64 = jnp.zeros((length, LANES - QK_ROPE), F32)
    c = jnp.concatenate([cos, cos, z64], axis=1)
    s1 = jnp.concatenate([-sin, z32, z64], axis=1)
    s2 = jnp.concatenate([z32, sin, z64], axis=1)
    return c, s1, s2


def kernel(x, meta_tokens, norm_mix_w, w_in, q_norm_w, w_uq, kv_norm_w, w_ukv, w_o_mla, pool_w,
           pool_scale, w_pool_out, w_out, norm_ffn_w, w_router_group, b_router_group,
           w_router_expert, b_router_expert, w_exp_gate, w_exp_up, w_exp_down, final_norm_w):
    assert x.shape == (1, 8192, D_MODEL) and norm_mix_w.shape[0] == 1
    n = x.shape[1]
    xr = x[0]

    wi = w_in[0]
    w_main = jnp.concatenate([wi[:, :1024], wi[:, 1088:]], axis=1).astype(BF16)
    w_rope = jnp.pad(wi[:, 1024:1088], ((0, 0), (0, LANES - QK_ROPE))).astype(BF16)
    w_q = jnp.pad(w_uq[0].reshape(Q_LORA, N_HEADS, QK_HEAD),
                  ((0, 0), (0, 0), (0, QK_PAD - QK_HEAD))).reshape(Q_LORA, N_HEADS * QK_PAD).astype(BF16)
    w_kv = w_ukv[0].reshape(KV_LORA, N_HEADS, QK_NOPE + V_HEAD)
    w_k = w_kv[:, :, :QK_NOPE].reshape(KV_LORA, N_HEADS * QK_NOPE).astype(BF16)
    w_v = w_kv[:, :, QK_NOPE:].reshape(KV_LORA, N_HEADS * V_HEAD).astype(BF16)
    w_router = jnp.pad(jnp.concatenate([w_router_group[0], w_router_expert[0]], axis=1),
                       ((0, 0), (0, LANES - N_GROUPS - N_EXPERTS))).astype(BF16)
    b_router = jnp.pad(jnp.concatenate([b_router_group[0], b_router_expert[0]]),
                       (0, LANES - N_GROUPS - N_EXPERTS))[None]
    c, s1, s2 = _rope_tables(N_META + n)

    proj, krope = _inproj(xr, norm_mix_w, w_main, w_rope, tm=512, tn=2048)
    proj_m, krope_m = _inproj(meta_tokens, norm_mix_w, w_main, w_rope, tm=N_META, tn=2048)
    q = _qproj(proj, q_norm_w, w_q, c[N_META:], s1[N_META:], s2[N_META:], tm=512)
    k, v = _kvproj(proj, krope, kv_norm_w, w_k, w_v, c[N_META:], s1[N_META:], s2[N_META:], tm=512)
    k_m, v_m = _kvproj(proj_m, krope_m, kv_norm_w, w_k, w_v, c[:N_META], s1[:N_META], s2[:N_META],
                       tm=N_META)
    o = _attention(q, k, v, k_m, v_m, tq=512, tk=512)

    merged = _merge(o, proj, proj_m, w_o_mla[0].astype(BF16), pool_w[0].astype(BF16), pool_scale,
                    w_pool_out[0].astype(BF16), tm=256)
    h, bn, ids, wts = _route(merged, xr, w_out[0].astype(BF16), norm_ffn_w, w_router, b_router, tm=256)

    flat_e = ids[:, :TOP_K].reshape(n * TOP_K)
    order = jnp.argsort(flat_e).astype(jnp.int32)
    counts = jnp.zeros((N_EXPERTS,), jnp.int32).at[flat_e].add(1)
    seg = jnp.concatenate([jnp.zeros((1,), jnp.int32), jnp.cumsum(counts).astype(jnp.int32)])
    y = _experts(order, seg, bn, w_exp_gate[0], w_exp_up[0], w_exp_down[0])

    out = _final(h, y.reshape(n, TOP_K * D_MODEL), wts, final_norm_w[None], tm=512)
    return out[None]
```

```python
import functools
import math

import jax
import jax.numpy as jnp
from jax import lax
from jax.experimental import pallas as pl
from jax.experimental.pallas import tpu as pltpu

F32 = jnp.float32
BF16 = jnp.bfloat16

D_MODEL = 2048
N_META = 16
EPS = 1e-6
N_HEADS = 16
QK_NOPE = 128
QK_ROPE = 64
QK_HEAD = QK_NOPE + QK_ROPE
V_HEAD = 128
Q_LORA = 512
KV_LORA = 512
ROPE_THETA = 10000.0
POOL_WINDOWS = (2, 4, 8, 16)
POOL_WIDTH = 1024
POOL_GROUP_DIM = 256
N_GROUPS = 8
EXPERTS_PER_GROUP = 8
N_EXPERTS = 64
TOP_K = 2
EXPERT_FF = 512

LANES = 128
QK_PAD = 256
V_PAD = 256
MAIN_COLS = Q_LORA + KV_LORA + POOL_WIDTH + 2 * D_MODEL
COL_CQ, COL_CKV, COL_POOL, COL_GM, COL_GP = 0, 512, 1024, 2048, 4096
ATTN_STRIP = 64
ROUTE_BLOCK = 256
DMA_UNROLL = 8
NEG_BIG = -1e30

VMEM_LIMIT = 56 * 1024 * 1024


def _params(sem):
    return pltpu.CompilerParams(dimension_semantics=sem, vmem_limit_bytes=VMEM_LIMIT)


def _rms(x, w):
    return x * lax.rsqrt(jnp.mean(x * x, axis=-1, keepdims=True) + EPS) * w


def _rope128(v, c, s1, s2):
    return v * c + pltpu.roll(v, 96, 1) * s1 + pltpu.roll(v, 32, 1) * s2


def _inproj_kernel(x_ref, nw_ref, w_ref, wr_ref, o_ref, kr_ref, a_sc):
    @pl.when(pl.program_id(1) == 0)
    def _():
        a = _rms(x_ref[...], nw_ref[...]).astype(BF16)
        a_sc[...] = a
        kr_ref[...] = jnp.dot(a, wr_ref[...], preferred_element_type=F32)

    o_ref[...] = jnp.dot(a_sc[...], w_ref[...], preferred_element_type=F32)


def _inproj(x, norm_w, w_main, w_rope, tm, tn):
    m, k = x.shape
    n = w_main.shape[1]
    return pl.pallas_call(
        _inproj_kernel,
        grid=(m // tm, n // tn),
        in_specs=[
            pl.BlockSpec((tm, k), lambda i, j: (i, 0)),
            pl.BlockSpec((1, k), lambda i, j: (0, 0)),
            pl.BlockSpec((k, tn), lambda i, j: (0, j)),
            pl.BlockSpec((k, LANES), lambda i, j: (0, 0)),
        ],
        out_specs=[
            pl.BlockSpec((tm, tn), lambda i, j: (i, j)),
            pl.BlockSpec((tm, LANES), lambda i, j: (i, 0)),
        ],
        out_shape=[jax.ShapeDtypeStruct((m, n), F32), jax.ShapeDtypeStruct((m, LANES), F32)],
        scratch_shapes=[pltpu.VMEM((tm, k), BF16)],
        compiler_params=_params(("parallel", "arbitrary")),
        name="inproj",
    )(x, norm_w, w_main, w_rope)


def _qproj_kernel(cq_ref, nw_ref, w_ref, c_ref, s1_ref, s2_ref, q_ref, *, scale):
    a = _rms(cq_ref[...], nw_ref[...]).astype(BF16)
    c, s1, s2 = c_ref[...], s1_ref[...], s2_ref[...]
    for h in range(N_HEADS):
        qh = jnp.dot(a, w_ref[:, h * QK_PAD:(h + 1) * QK_PAD], preferred_element_type=F32) * scale
        q_ref[:, h * QK_PAD:h * QK_PAD + LANES] = qh[:, :LANES].astype(BF16)
        q_ref[:, h * QK_PAD + LANES:(h + 1) * QK_PAD] = _rope128(qh[:, LANES:], c, s1, s2).astype(BF16)


def _qproj(proj, norm_w, w_q, c, s1, s2, tm):
    m = proj.shape[0]
    n = N_HEADS * QK_PAD
    tab = pl.BlockSpec((tm, LANES), lambda i: (i, 0))
    return pl.pallas_call(
        functools.partial(_qproj_kernel, scale=QK_HEAD ** -0.5 * math.log2(math.e)),
        grid=(m // tm,),
        in_specs=[
            pl.BlockSpec((tm, Q_LORA), lambda i: (i, COL_CQ // Q_LORA)),
            pl.BlockSpec((1, Q_LORA), lambda i: (0, 0)),
            pl.BlockSpec((Q_LORA, n), lambda i: (0, 0)),
            tab, tab, tab,
        ],
        out_specs=pl.BlockSpec((tm, n), lambda i: (i, 0)),
        out_shape=jax.ShapeDtypeStruct((m, n), BF16),
        compiler_params=_params(("parallel",)),
        name="qproj",
    )(proj, norm_w, w_q, c, s1, s2)


def _kvproj_kernel(ckv_ref, kr_ref, nw_ref, wk_ref, wv_ref, c_ref, s1_ref, s2_ref, k_ref, v_ref):
    a = _rms(ckv_ref[...], nw_ref[...]).astype(BF16)
    kpe = _rope128(kr_ref[...], c_ref[...], s1_ref[...], s2_ref[...]).astype(BF16)
    ones = jnp.ones((a.shape[0], V_PAD - V_HEAD), BF16)
    for h2 in range(N_HEADS // 2):
        kn = jnp.dot(a, wk_ref[:, h2 * 256:(h2 + 1) * 256], preferred_element_type=F32).astype(BF16)
        vv = jnp.dot(a, wv_ref[:, h2 * 256:(h2 + 1) * 256], preferred_element_type=F32).astype(BF16)
        for d in range(2):
            h = 2 * h2 + d
            k_ref[:, h * QK_PAD:h * QK_PAD + LANES] = kn[:, d * LANES:(d + 1) * LANES]
            k_ref[:, h * QK_PAD + LANES:(h + 1) * QK_PAD] = kpe
            v_ref[:, h * V_PAD:h * V_PAD + V_HEAD] = vv[:, d * LANES:(d + 1) * LANES]
            v_ref[:, h * V_PAD + V_HEAD:(h + 1) * V_PAD] = ones


def _kvproj(proj, krope, norm_w, w_k, w_v, c, s1, s2, tm):
    m = proj.shape[0]
    tab = pl.BlockSpec((tm, LANES), lambda i: (i, 0))
    return pl.pallas_call(
        _kvproj_kernel,
        grid=(m // tm,),
        in_specs=[
            pl.BlockSpec((tm, KV_LORA), lambda i: (i, COL_CKV // KV_LORA)),
            tab,
            pl.BlockSpec((1, KV_LORA), lambda i: (0, 0)),
            pl.BlockSpec((KV_LORA, N_HEADS * QK_NOPE), lambda i: (0, 0)),
            pl.BlockSpec((KV_LORA, N_HEADS * V_HEAD), lambda i: (0, 0)),
            tab, tab, tab,
        ],
        out_specs=[
            pl.BlockSpec((tm, N_HEADS * QK_PAD), lambda i: (i, 0)),
            pl.BlockSpec((tm, N_HEADS * V_PAD), lambda i: (i, 0)),
        ],
        out_shape=[jax.ShapeDtypeStruct((m, N_HEADS * QK_PAD), BF16),
                   jax.ShapeDtypeStruct((m, N_HEADS * V_PAD), BF16)],
        compiler_params=_params(("parallel",)),
        name="kvproj",
    )(proj, krope, norm_w, w_k, w_v, c, s1, s2)


def _attn_kernel(q_ref, k_ref, v_ref, km_ref, vm_ref, o_ref,
                 acc_sc, m_sc, s_a, s_b, p_a, p_b, al_a, al_b, *, t):
    i = pl.program_id(1)
    q = q_ref[...]
    nt = (((1,), (1,)), ((), ()))

    def chunk(ref, c):
        return ref[pl.ds(pl.multiple_of(c * t, t), t), :]

    def scores(c, s_out):
        s_out[...] = lax.dot_general(q, chunk(k_ref, c), nt, preferred_element_type=F32)

    def softmax(s_in, p_out, al_out, masked):
        for r in range(t // ATTN_STRIP):
            rows = slice(r * ATTN_STRIP, (r + 1) * ATTN_STRIP)
            s = s_in[rows, :]
            if masked:
                row = r * ATTN_STRIP + lax.broadcasted_iota(jnp.int32, s.shape, 0)
                s = jnp.where(lax.broadcasted_iota(jnp.int32, s.shape, 1) <= row, s, NEG_BIG)
            m_old = m_sc[rows, :]
            m_new = jnp.maximum(m_old, jnp.broadcast_to(jnp.max(s, axis=-1, keepdims=True), m_old.shape))
            p_out[rows, :] = jnp.concatenate(
                [jnp.exp2(s[:, c * LANES:(c + 1) * LANES] - m_new) for c in range(t // LANES)],
                axis=1).astype(BF16)
            al_out[rows, :] = jnp.exp2(m_old - m_new)
            m_sc[rows, :] = m_new

    def accumulate(p_in, al_in, c):
        pv = jnp.dot(p_in[...], chunk(v_ref, c), preferred_element_type=F32)
        alpha = al_in[...]
        acc_sc[...] = jnp.concatenate([alpha, alpha], axis=1) * acc_sc[...] + pv

    def stage(k, ins, outs, with_next=True):
        if with_next:
            scores(jnp.minimum(k + 1, i - 1), outs[0])
        accumulate(ins[1], ins[2], jnp.where(k == 0, i, k - 1))
        softmax(ins[0], outs[1], outs[2], masked=False)

    buf_a, buf_b = (s_a, p_a, al_a), (s_b, p_b, al_b)

    s = lax.dot_general(q, km_ref[...], nt, preferred_element_type=F32)
    m0 = jnp.max(s, axis=-1, keepdims=True)
    acc_sc[...] = jnp.dot(jnp.exp2(s - m0).astype(BF16), vm_ref[...], preferred_element_type=F32)
    m_sc[...] = jnp.broadcast_to(m0, m_sc.shape)
    scores(i, s_b)
    scores(0, s_a)
    softmax(s_b, p_a, al_a, masked=True)

    def pair(g, _):
        stage(2 * g, buf_a, buf_b)
        stage(2 * g + 1, buf_b, buf_a)
        return 0

    lax.fori_loop(0, i // 2, pair, 0)
    odd = lax.rem(i, 2) == 1

    @pl.when(odd)
    def _():
        stage(i - 1, buf_a, buf_b, with_next=False)
        accumulate(p_b, al_b, i - 1)

    @pl.when(jnp.logical_not(odd))
    def _():
        accumulate(p_a, al_a, jnp.maximum(i - 1, 0))

    acc = acc_sc[...]
    o_ref[...] = (acc[:, :V_HEAD] / acc[:, V_HEAD:]).astype(o_ref.dtype)


def _attention(q, k, v, k_meta, v_meta, t):
    n = q.shape[0]
    return pl.pallas_call(
        functools.partial(_attn_kernel, t=t),
        grid=(N_HEADS, n // t),
        in_specs=[
            pl.BlockSpec((t, QK_PAD), lambda h, i: (i, h)),
            pl.BlockSpec((n, QK_PAD), lambda h, i: (0, h)),
            pl.BlockSpec((n, V_PAD), lambda h, i: (0, h)),
            pl.BlockSpec((N_META, QK_PAD), lambda h, i: (0, h)),
            pl.BlockSpec((N_META, V_PAD), lambda h, i: (0, h)),
        ],
        out_specs=pl.BlockSpec((t, V_HEAD), lambda h, i: (i, h)),
        out_shape=jax.ShapeDtypeStruct((n, N_HEADS * V_HEAD), BF16),
        scratch_shapes=[
            pltpu.VMEM((t, V_PAD), F32), pltpu.VMEM((t, LANES), F32),
            pltpu.VMEM((t, t), F32), pltpu.VMEM((t, t), F32),
            pltpu.VMEM((t, t), BF16), pltpu.VMEM((t, t), BF16),
            pltpu.VMEM((t, LANES), F32), pltpu.VMEM((t, LANES), F32),
        ],
        compiler_params=_params(("parallel", "arbitrary")),
        name="attention",
    )(q, k, v, k_meta, v_meta)


def _merge_kernel(o_ref, u_ref, halo_ref, um_ref, gm_ref, gp_ref, wo_ref, pw_ref, ps_ref, wpo_ref,
                  out_ref, ext_sc, *, tm):
    i = pl.program_id(0)
    ext_sc[0:N_META, :] = jnp.where(i == 0, um_ref[...], halo_ref[...])
    ext_sc[N_META:, :] = u_ref[...]
    y_pool = jnp.zeros((tm, D_MODEL), F32)
    for g, w in enumerate(POOL_WINDOWS):
        lo, hi = g * POOL_GROUP_DIM, (g + 1) * POOL_GROUP_DIM
        u = ext_sc[N_META:, lo:hi]
        tot = u
        for d in range(1, w):
            tot = tot + ext_sc[N_META - d:N_META - d + tm, lo:hi]
        pooled = (tot * (1.0 / w) - u).astype(BF16)
        mixed = jnp.dot(pooled, pw_ref[g], preferred_element_type=F32) * ps_ref[:, lo:hi]
        y_pool = y_pool + jnp.dot(mixed.astype(BF16), wpo_ref[lo:hi, :], preferred_element_type=F32)
    y_mla = jnp.dot(o_ref[...], wo_ref[...], preferred_element_type=F32)
    merged = jax.nn.sigmoid(gm_ref[...]) * y_mla + jax.nn.sigmoid(gp_ref[...]) * y_pool
    out_ref[...] = merged.astype(out_ref.dtype)


def _merge(o, proj, u_meta, w_o, pool_w, pool_scale, w_pool_out, tm):
    n = o.shape[0]
    hb = tm // N_META
    const = lambda i: (0, 0)
    return pl.pallas_call(
        functools.partial(_merge_kernel, tm=tm),
        grid=(n // tm,),
        in_specs=[
            pl.BlockSpec((tm, D_MODEL), lambda i: (i, 0)),
            pl.BlockSpec((tm, POOL_WIDTH), lambda i: (i, COL_POOL // POOL_WIDTH)),
            pl.BlockSpec((N_META, POOL_WIDTH),
                         lambda i: (jnp.maximum(i * hb - 1, 0), COL_POOL // POOL_WIDTH)),
            pl.BlockSpec((N_META, POOL_WIDTH), lambda i: (0, COL_POOL // POOL_WIDTH)),
            pl.BlockSpec((tm, D_MODEL), lambda i: (i, COL_GM // D_MODEL)),
            pl.BlockSpec((tm, D_MODEL), lambda i: (i, COL_GP // D_MODEL)),
            pl.BlockSpec((D_MODEL, D_MODEL), const),
            pl.BlockSpec((len(POOL_WINDOWS), POOL_GROUP_DIM, POOL_GROUP_DIM), lambda i: (0, 0, 0)),
            pl.BlockSpec((1, POOL_WIDTH), const),
            pl.BlockSpec((POOL_WIDTH, D_MODEL), const),
        ],
        out_specs=pl.BlockSpec((tm, D_MODEL), lambda i: (i, 0)),
        out_shape=jax.ShapeDtypeStruct((n, D_MODEL), BF16),
        scratch_shapes=[pltpu.VMEM((tm + N_META, POOL_WIDTH), F32)],
        compiler_params=_params(("arbitrary",)),
        name="merge",
    )(o, proj, proj, u_meta, proj, proj, w_o, pool_w, pool_scale, w_pool_out)


def _route_kernel(mg_ref, x_ref, wout_ref, nw_ref, wr_ref, br_ref, h_ref, bn_ref, ids_ref, wts_ref):
    h = x_ref[...] + jnp.dot(mg_ref[...], wout_ref[...], preferred_element_type=F32)
    h_ref[...] = h
    bn = _rms(h, nw_ref[...])
    bn_ref[...] = bn
    logits = jnp.dot(bn.astype(BF16), wr_ref[...], preferred_element_type=F32) + br_ref[...]
    lane = lax.broadcasted_iota(jnp.int32, logits.shape, 1)
    lane_f = lane.astype(F32)

    def first_max(vals):
        vmax = jnp.max(vals, axis=-1, keepdims=True)
        idx = jnp.min(jnp.where(vals == vmax, lane_f, float(LANES)), axis=-1, keepdims=True)
        return vmax, idx.astype(jnp.int32)

    glog = jnp.where(lane < N_GROUPS, logits, -jnp.inf)
    gmax, gidx = first_max(glog)
    p_g = 1.0 / jnp.sum(jnp.exp(glog - gmax), axis=-1, keepdims=True)
    lo = N_GROUPS + gidx * EXPERTS_PER_GROUP
    elog = jnp.where((lane >= lo) & (lane < lo + EXPERTS_PER_GROUP), logits, -jnp.inf)
    v1, i1 = first_max(elog)
    v2, i2 = first_max(jnp.where(lane == i1, -jnp.inf, elog))
    e2 = jnp.exp(v2 - v1)
    den = 1.0 + e2
    w1 = p_g * (1.0 / den)
    w2 = p_g * (e2 / den)
    ids_ref[...] = jnp.where(lane == 0, i1 - N_GROUPS, jnp.where(lane == 1, i2 - N_GROUPS, 0))
    wts_ref[...] = jnp.where(lane == 0, w1, jnp.where(lane == 1, w2, 0.0))


def _route(merged, x, w_out, norm_w, w_router, b_router, tm):
    n = x.shape[0]
    const = lambda i: (0, 0)
    row = lambda width: pl.BlockSpec((tm, width), lambda i: (i, 0))
    return pl.pallas_call(
        _route_kernel,
        grid=(n // tm,),
        in_specs=[
            row(D_MODEL), row(D_MODEL),
            pl.BlockSpec((D_MODEL, D_MODEL), const),
            pl.BlockSpec((1, D_MODEL), const),
            pl.BlockSpec((D_MODEL, LANES), const),
            pl.BlockSpec((1, LANES), const),
        ],
        out_specs=[row(D_MODEL), row(D_MODEL), row(LANES), row(LANES)],
        out_shape=[jax.ShapeDtypeStruct((n, D_MODEL), F32), jax.ShapeDtypeStruct((n, D_MODEL), F32),
                   jax.ShapeDtypeStruct((n, LANES), jnp.int32), jax.ShapeDtypeStruct((n, LANES), F32)],
        compiler_params=_params(("parallel",)),
        name="route",
    )(merged, x, w_out, norm_w, w_router, b_router)


def _for_rows(n, fn):
    n_groups = lax.shift_right_logical(n, int(math.log2(DMA_UNROLL)))

    def group(g, _):
        for u in range(DMA_UNROLL):
            fn(g * DMA_UNROLL + u)
        return 0

    def single(r, _):
        fn(r)
        return 0

    lax.fori_loop(0, n_groups, group, 0)
    lax.fori_loop(n_groups * DMA_UNROLL, n, single, 0)


def _experts_kernel(be_ref, nv_ref, sb_ref, order_ref, bn_hbm, wg_ref, wu_ref, wd_ref, y_hbm,
                    xbuf, ybuf, wg_sc, wu_sc, wd_sc, gsem, ssem, *, n_tokens):
    b = pl.program_id(0)
    nb = pl.num_programs(0)
    slot = lax.rem(b, 2)

    def gather(blk, slt, r):
        tok = lax.shift_right_logical(order_ref[sb_ref[blk] + r], 1)
        return pltpu.make_async_copy(bn_hbm.at[pl.ds(tok, 1)], xbuf.at[slt, pl.ds(r, 1)], gsem.at[slt])

    def scatter(blk, slt, r):
        a = order_ref[sb_ref[blk] + r]
        dst = (a & 1) * n_tokens + lax.shift_right_logical(a, 1)
        return pltpu.make_async_copy(ybuf.at[slt, pl.ds(r, 1)], y_hbm.at[pl.ds(dst, 1)], ssem.at[slt])

    def gather_start(blk, slt):
        _for_rows(nv_ref[blk], lambda r: gather(blk, slt, r).start())

    def scatter_wait(blk, slt):
        _for_rows(nv_ref[blk], lambda r: scatter(blk, slt, r).wait())

    @pl.when(b == 0)
    def _():
        xbuf[...] = jnp.zeros(xbuf.shape, xbuf.dtype)
        gather_start(0, 0)

    @pl.when(b + 1 < nb)
    def _():
        gather_start(b + 1, 1 - slot)

    @pl.when(nv_ref[b] > 0)
    def _():
        @pl.when((b == 0) | (be_ref[b] != be_ref[jnp.maximum(b - 1, 0)]))
        def _():
            wg_sc[...] = wg_ref[0].astype(BF16)
            wu_sc[...] = wu_ref[0].astype(BF16)
            wd_sc[...] = wd_ref[0].astype(BF16)

        _for_rows(nv_ref[b], lambda r: gather(b, slot, r).wait())
        x = xbuf[slot].astype(BF16)
        gate = jnp.dot(x, wg_sc[...], preferred_element_type=F32)
        up = jnp.dot(x, wu_sc[...], preferred_element_type=F32)
        hdn = (gate * jax.nn.sigmoid(gate) * up).astype(BF16)
        ybuf[slot] = jnp.dot(hdn, wd_sc[...], preferred_element_type=F32)
        _for_rows(nv_ref[b], lambda r: scatter(b, slot, r).start())

    @pl.when(b >= 1)
    def _():
        scatter_wait(b - 1, 1 - slot)

    @pl.when(b == nb - 1)
    def _():
        scatter_wait(b, slot)


def _experts(block_expert, block_rows, block_base, order, bn, w_gate, w_up, w_down):
    n = bn.shape[0]
    wmap = lambda b, be, nv, sb, o: (be[b], 0, 0)
    grid_spec = pltpu.PrefetchScalarGridSpec(
        num_scalar_prefetch=4,
        grid=(block_expert.shape[0],),
        in_specs=[
            pl.BlockSpec(memory_space=pl.ANY),
            pl.BlockSpec((1, D_MODEL, EXPERT_FF), wmap),
            pl.BlockSpec((1, D_MODEL, EXPERT_FF), wmap),
            pl.BlockSpec((1, EXPERT_FF, D_MODEL), wmap),
        ],
        out_specs=pl.BlockSpec(memory_space=pl.ANY),
        scratch_shapes=[
            pltpu.VMEM((2, ROUTE_BLOCK, D_MODEL), F32),
            pltpu.VMEM((2, ROUTE_BLOCK, D_MODEL), F32),
            pltpu.VMEM((D_MODEL, EXPERT_FF), BF16),
            pltpu.VMEM((D_MODEL, EXPERT_FF), BF16),
            pltpu.VMEM((EXPERT_FF, D_MODEL), BF16),
            pltpu.SemaphoreType.DMA((2,)),
            pltpu.SemaphoreType.DMA((2,)),
        ],
    )
    return pl.pallas_call(
        functools.partial(_experts_kernel, n_tokens=n),
        grid_spec=grid_spec,
        out_shape=jax.ShapeDtypeStruct((TOP_K * n, D_MODEL), F32),
        compiler_params=_params(("arbitrary",)),
        name="experts",
    )(block_expert, block_rows, block_base, order, bn, w_gate, w_up, w_down)


def _route_blocks(flat_e):
    a = flat_e.shape[0]
    i32 = jnp.int32
    order = jnp.argsort(flat_e).astype(i32)
    counts = jnp.zeros((N_EXPERTS,), i32).at[flat_e].add(1)
    padded = (counts + ROUTE_BLOCK - 1) // ROUTE_BLOCK * ROUTE_BLOCK
    pad_end = jnp.cumsum(padded).astype(i32)
    pad_start = pad_end - padded
    seg_start = jnp.cumsum(counts).astype(i32) - counts
    n_blocks = -(-(a + N_EXPERTS * (ROUTE_BLOCK - 1)) // ROUTE_BLOCK)
    blk = jnp.arange(n_blocks, dtype=i32)
    be = jnp.minimum(jnp.searchsorted(pad_end, blk * ROUTE_BLOCK, side='right'), N_EXPERTS - 1).astype(i32)
    off = blk * ROUTE_BLOCK - pad_start[be]
    rows = jnp.clip(counts[be] - off, 0, ROUTE_BLOCK).astype(i32)
    base = jnp.where(rows > 0, seg_start[be] + off, 0).astype(i32)
    n_used = pad_end[-1] // ROUTE_BLOCK
    be = jnp.where(blk < n_used, be, be[jnp.maximum(n_used - 1, 0)])
    return be, rows, base, order


def _final_kernel(h_ref, y0_ref, y1_ref, wts_ref, nw_ref, o_ref):
    w = wts_ref[...]
    moe = y0_ref[...] * w[:, 0:1] + y1_ref[...] * w[:, 1:2]
    o_ref[...] = _rms(h_ref[...] + moe, nw_ref[...])


def _final(h, y, wts, norm_w, tm):
    n = h.shape[0]
    return pl.pallas_call(
        _final_kernel,
        grid=(n // tm,),
        in_specs=[
            pl.BlockSpec((tm, D_MODEL), lambda i: (i, 0)),
            pl.BlockSpec((tm, D_MODEL), lambda i: (i, 0)),
            pl.BlockSpec((tm, D_MODEL), lambda i: (i + n // tm, 0)),
            pl.BlockSpec((tm, LANES), lambda i: (i, 0)),
            pl.BlockSpec((1, D_MODEL), lambda i: (0, 0)),
        ],
        out_specs=pl.BlockSpec((tm, D_MODEL), lambda i: (i, 0)),
        out_shape=jax.ShapeDtypeStruct((n, D_MODEL), F32),
        compiler_params=_params(("parallel",)),
        name="final",
    )(h, y, y, wts, norm_w)


def _rope_tables(length):
    inv = 1.0 / (ROPE_THETA ** (jnp.arange(0, QK_ROPE, 2, dtype=F32) / QK_ROPE))
    ang = jnp.arange(length, dtype=F32)[:, None] * inv[None, :]
    cos, sin = jnp.cos(ang), jnp.sin(ang)
    z32 = jnp.zeros_like(cos)
    z64 = jnp.zeros((length, LANES - QK_ROPE), F32)
    c = jnp.concatenate([cos, cos, z64], axis=1)
    s1 = jnp.concatenate([-sin, z32, z64], axis=1)
    s2 = jnp.concatenate([z32, sin, z64], axis=1)
    return c, s1, s2


def kernel(x, meta_tokens, norm_mix_w, w_in, q_norm_w, w_uq, kv_norm_w, w_ukv, w_o_mla, pool_w,
           pool_scale, w_pool_out, w_out, norm_ffn_w, w_router_group, b_router_group,
           w_router_expert, b_router_expert, w_exp_gate, w_exp_up, w_exp_down, final_norm_w):
    assert x.shape == (1, 8192, D_MODEL) and norm_mix_w.shape[0] == 1
    n = x.shape[1]
    xr = x[0]

    wi = w_in[0]
    w_main = jnp.concatenate([wi[:, :1024], wi[:, 1088:]], axis=1).astype(BF16)
    w_rope = jnp.pad(wi[:, 1024:1088], ((0, 0), (0, LANES - QK_ROPE))).astype(BF16)
    w_q = jnp.pad(w_uq[0].reshape(Q_LORA, N_HEADS, QK_HEAD),
                  ((0, 0), (0, 0), (0, QK_PAD - QK_HEAD))).reshape(Q_LORA, N_HEADS * QK_PAD).astype(BF16)
    w_kv = w_ukv[0].reshape(KV_LORA, N_HEADS, QK_NOPE + V_HEAD)
    w_k = w_kv[:, :, :QK_NOPE].reshape(KV_LORA, N_HEADS * QK_NOPE).astype(BF16)
    w_v = w_kv[:, :, QK_NOPE:].reshape(KV_LORA, N_HEADS * V_HEAD).astype(BF16)
    w_router = jnp.pad(jnp.concatenate([w_router_group[0], w_router_expert[0]], axis=1),
                       ((0, 0), (0, LANES - N_GROUPS - N_EXPERTS))).astype(BF16)
    b_router = jnp.pad(jnp.concatenate([b_router_group[0], b_router_expert[0]]),
                       (0, LANES - N_GROUPS - N_EXPERTS))[None]
    c, s1, s2 = _rope_tables(N_META + n)

    proj, krope = _inproj(xr, norm_mix_w, w_main, w_rope, tm=512, tn=2048)
    proj_m, krope_m = _inproj(meta_tokens, norm_mix_w, w_main, w_rope, tm=N_META, tn=2048)
    q = _qproj(proj, q_norm_w, w_q, c[N_META:], s1[N_META:], s2[N_META:], tm=512)
    k, v = _kvproj(proj, krope, kv_norm_w, w_k, w_v, c[N_META:], s1[N_META:], s2[N_META:], tm=512)
    k_m, v_m = _kvproj(proj_m, krope_m, kv_norm_w, w_k, w_v, c[:N_META], s1[:N_META], s2[:N_META],
                       tm=N_META)
    o = _attention(q, k, v, k_m, v_m, t=512)

    merged = _merge(o, proj, proj_m, w_o_mla[0].astype(BF16), pool_w[0].astype(BF16), pool_scale,
                    w_pool_out[0].astype(BF16), tm=256)
    h, bn, ids, wts = _route(merged, xr, w_out[0].astype(BF16), norm_ffn_w, w_router, b_router, tm=256)

    be, rows, base, order = _route_blocks(ids[:, :TOP_K].reshape(n * TOP_K))
    y = _experts(be, rows, base, order, bn, w_exp_gate[0], w_exp_up[0], w_exp_down[0])

    out = _final(h, y, wts, final_norm_w[None], tm=512)
    return out[None]
```

```python
import functools
import math

import jax
import jax.numpy as jnp
import numpy as np
from jax import lax
from jax.experimental import pallas as pl
from jax.experimental.pallas import tpu as pltpu

F32 = jnp.float32
BF16 = jnp.bfloat16

D_MODEL = 2048
N_META = 16
EPS = 1e-6
N_HEADS = 16
QK_NOPE = 128
QK_ROPE = 64
QK_HEAD = QK_NOPE + QK_ROPE
V_HEAD = 128
Q_LORA = 512
KV_LORA = 512
ROPE_THETA = 10000.0
POOL_WINDOWS = (2, 4, 8, 16)
POOL_WIDTH = 1024
POOL_GROUP_DIM = 256
N_GROUPS = 8
EXPERTS_PER_GROUP = 8
N_EXPERTS = 64
TOP_K = 2
EXPERT_FF = 512

LANES = 128
SUBLANES = 8
QK_PAD = 256
V_PAD = 256
MAIN_COLS = Q_LORA + KV_LORA + POOL_WIDTH + 2 * D_MODEL
COL_CQ, COL_CKV, COL_POOL, COL_GM, COL_GP = 0, 512, 1024, 2048, 4096
ATTN_HEADS = 2
ATTN_STRIP = 64
ROUTE_BLOCK = 256
NEG_BIG = -1e30

VMEM_LIMIT = 56 * 1024 * 1024


def _params(sem):
    return pltpu.CompilerParams(dimension_semantics=sem, vmem_limit_bytes=VMEM_LIMIT)


def _rms(x, w):
    return x * lax.rsqrt(jnp.mean(x * x, axis=-1, keepdims=True) + EPS) * w


def _rope128(v, c, s1, s2):
    return v * c + pltpu.roll(v, 96, 1) * s1 + pltpu.roll(v, 32, 1) * s2


def _inproj_kernel(x_ref, nw_ref, wa_ref, wb_ref, wr_ref, o_ref, kr_ref, a_sc):
    j = pl.program_id(1)

    @pl.when(j == 0)
    def _():
        a = _rms(x_ref[...], nw_ref[...]).astype(BF16)
        a_sc[...] = a
        kr_ref[...] = jnp.dot(a, wr_ref[...], preferred_element_type=F32)
        o_ref[...] = jnp.dot(a, wa_ref[...], preferred_element_type=F32)

    @pl.when(j > 0)
    def _():
        o_ref[...] = jnp.dot(a_sc[...], wb_ref[...], preferred_element_type=F32)


def _inproj(x, norm_w, w_a, w_b, w_rope, tm):
    m, k = x.shape
    tn = w_a.shape[1]
    n = tn + w_b.shape[1]
    return pl.pallas_call(
        _inproj_kernel,
        grid=(m // tm, n // tn),
        in_specs=[
            pl.BlockSpec((tm, k), lambda i, j: (i, 0)),
            pl.BlockSpec((1, k), lambda i, j: (0, 0)),
            pl.BlockSpec((k, tn), lambda i, j: (0, 0)),
            pl.BlockSpec((k, tn), lambda i, j: (0, jnp.maximum(j - 1, 0))),
            pl.BlockSpec((k, LANES), lambda i, j: (0, 0)),
        ],
        out_specs=[
            pl.BlockSpec((tm, tn), lambda i, j: (i, j)),
            pl.BlockSpec((tm, LANES), lambda i, j: (i, 0)),
        ],
        out_shape=[jax.ShapeDtypeStruct((m, n), F32), jax.ShapeDtypeStruct((m, LANES), F32)],
        scratch_shapes=[pltpu.VMEM((tm, k), BF16)],
        compiler_params=_params(("parallel", "arbitrary")),
        name="inproj",
    )(x, norm_w, w_a, w_b, w_rope)


def _qproj_kernel(cq_ref, nw_ref, w_ref, c_ref, s1_ref, s2_ref, q_ref, *, scale):
    a = _rms(cq_ref[...], nw_ref[...]).astype(BF16)
    c, s1, s2 = c_ref[...], s1_ref[...], s2_ref[...]
    for h in range(N_HEADS):
        qh = jnp.dot(a, w_ref[:, h * QK_PAD:(h + 1) * QK_PAD], preferred_element_type=F32) * scale
        q_ref[:, h * QK_PAD:h * QK_PAD + LANES] = qh[:, :LANES].astype(BF16)
        q_ref[:, h * QK_PAD + LANES:(h + 1) * QK_PAD] = _rope128(qh[:, LANES:], c, s1, s2).astype(BF16)


def _qproj(proj, norm_w, w_q, c, s1, s2, tm):
    m = proj.shape[0]
    n = N_HEADS * QK_PAD
    tab = pl.BlockSpec((tm, LANES), lambda i: (i, 0))
    return pl.pallas_call(
        functools.partial(_qproj_kernel, scale=QK_HEAD ** -0.5 * math.log2(math.e)),
        grid=(m // tm,),
        in_specs=[
            pl.BlockSpec((tm, Q_LORA), lambda i: (i, COL_CQ // Q_LORA)),
            pl.BlockSpec((1, Q_LORA), lambda i: (0, 0)),
            pl.BlockSpec((Q_LORA, n), lambda i: (0, 0)),
            tab, tab, tab,
        ],
        out_specs=pl.BlockSpec((tm, n), lambda i: (i, 0)),
        out_shape=jax.ShapeDtypeStruct((m, n), BF16),
        compiler_params=_params(("parallel",)),
        name="qproj",
    )(proj, norm_w, w_q, c, s1, s2)


def _kvproj_kernel(ckv_ref, kr_ref, nw_ref, wk_ref, wv_ref, c_ref, s1_ref, s2_ref, k_ref, v_ref):
    a = _rms(ckv_ref[...], nw_ref[...]).astype(BF16)
    kpe = _rope128(kr_ref[...], c_ref[...], s1_ref[...], s2_ref[...]).astype(BF16)
    ones = jnp.ones((a.shape[0], V_PAD - V_HEAD), BF16)
    for h2 in range(N_HEADS // 2):
        kn = jnp.dot(a, wk_ref[:, h2 * 256:(h2 + 1) * 256], preferred_element_type=F32).astype(BF16)
        vv = jnp.dot(a, wv_ref[:, h2 * 256:(h2 + 1) * 256], preferred_element_type=F32).astype(BF16)
        for d in range(2):
            h = 2 * h2 + d
            k_ref[:, h * QK_PAD:h * QK_PAD + LANES] = kn[:, d * LANES:(d + 1) * LANES]
            k_ref[:, h * QK_PAD + LANES:(h + 1) * QK_PAD] = kpe
            v_ref[:, h * V_PAD:h * V_PAD + V_HEAD] = vv[:, d * LANES:(d + 1) * LANES]
            v_ref[:, h * V_PAD + V_HEAD:(h + 1) * V_PAD] = ones


def _kvproj(proj, krope, norm_w, w_k, w_v, c, s1, s2, tm):
    m = proj.shape[0]
    tab = pl.BlockSpec((tm, LANES), lambda i: (i, 0))
    return pl.pallas_call(
        _kvproj_kernel,
        grid=(m // tm,),
        in_specs=[
            pl.BlockSpec((tm, KV_LORA), lambda i: (i, COL_CKV // KV_LORA)),
            tab,
            pl.BlockSpec((1, KV_LORA), lambda i: (0, 0)),
            pl.BlockSpec((KV_LORA, N_HEADS * QK_NOPE), lambda i: (0, 0)),
            pl.BlockSpec((KV_LORA, N_HEADS * V_HEAD), lambda i: (0, 0)),
            tab, tab, tab,
        ],
        out_specs=[
            pl.BlockSpec((tm, N_HEADS * QK_PAD), lambda i: (i, 0)),
            pl.BlockSpec((tm, N_HEADS * V_PAD), lambda i: (i, 0)),
        ],
        out_shape=[jax.ShapeDtypeStruct((m, N_HEADS * QK_PAD), BF16),
                   jax.ShapeDtypeStruct((m, N_HEADS * V_PAD), BF16)],
        compiler_params=_params(("parallel",)),
        name="kvproj",
    )(proj, krope, norm_w, w_k, w_v, c, s1, s2)


def _attn_kernel(q_ref, k_ref, v_ref, km_ref, vm_ref, o_ref, acc_sc, m_sc, s_sc, p_sc, al_sc, *, t):
    i = pl.program_id(1)
    heads = range(ATTN_HEADS)
    nt = (((1,), (1,)), ((), ()))
    qs = [q_ref[:, g * QK_PAD:(g + 1) * QK_PAD] for g in heads]

    def chunk(ref, g, width, c):
        return ref[pl.ds(pl.multiple_of(c * t, t), t), g * width:(g + 1) * width]

    def scores(g, c, slot):
        s_sc[g, slot] = lax.dot_general(qs[g], chunk(k_ref, g, QK_PAD, c), nt, preferred_element_type=F32)

    def softmax(g, src, dst, masked):
        for r in range(t // ATTN_STRIP):
            rows = slice(r * ATTN_STRIP, (r + 1) * ATTN_STRIP)
            s = s_sc[g, src, rows, :]
            if masked:
                row = r * ATTN_STRIP + lax.broadcasted_iota(jnp.int32, s.shape, 0)
                s = jnp.where(lax.broadcasted_iota(jnp.int32, s.shape, 1) <= row, s, NEG_BIG)
            m_old = m_sc[g, rows, :]
            m_new = jnp.maximum(m_old, jnp.broadcast_to(jnp.max(s, axis=-1, keepdims=True), m_old.shape))
            p_sc[g, dst, rows, :] = jnp.concatenate(
                [jnp.exp2(s[:, c * LANES:(c + 1) * LANES] - m_new) for c in range(t // LANES)],
                axis=1).astype(BF16)
            al_sc[g, dst, rows, :] = jnp.exp2(m_old - m_new)
            m_sc[g, rows, :] = m_new

    def accumulate(g, slot, c):
        pv = jnp.dot(p_sc[g, slot], chunk(v_ref, g, V_PAD, c), preferred_element_type=F32)
        alpha = al_sc[g, slot]
        acc_sc[g] = jnp.concatenate([alpha, alpha], axis=1) * acc_sc[g] + pv

    def stage(k, src, with_next=True):
        for g in heads:
            if with_next:
                scores(g, jnp.minimum(k + 1, i - 1), 1 - src)
            accumulate(g, src, jnp.where(k == 0, i, k - 1))
            softmax(g, src, 1 - src, masked=False)

    for g in heads:
        s = lax.dot_general(qs[g], km_ref[:, g * QK_PAD:(g + 1) * QK_PAD], nt, preferred_element_type=F32)
        m0 = jnp.max(s, axis=-1, keepdims=True)
        acc_sc[g] = jnp.dot(jnp.exp2(s - m0).astype(BF16), vm_ref[:, g * V_PAD:(g + 1) * V_PAD],
                            preferred_element_type=F32)
        m_sc[g] = jnp.broadcast_to(m0, (t, LANES))
        scores(g, i, 1)
        scores(g, 0, 0)
        softmax(g, 1, 0, masked=True)

    def pair(j, _):
        stage(2 * j, 0)
        stage(2 * j + 1, 1)
        return 0

    lax.fori_loop(0, i // 2, pair, 0)
    odd = lax.rem(i, 2) == 1

    @pl.when(odd)
    def _():
        stage(i - 1, 0, with_next=False)
        for g in heads:
            accumulate(g, 1, i - 1)

    @pl.when(jnp.logical_not(odd))
    def _():
        for g in heads:
            accumulate(g, 0, jnp.maximum(i - 1, 0))

    for g in heads:
        acc = acc_sc[g]
        o_ref[:, g * V_HEAD:(g + 1) * V_HEAD] = (acc[:, :V_HEAD] / acc[:, V_HEAD:]).astype(o_ref.dtype)


def _attention(q, k, v, k_meta, v_meta, t):
    n = q.shape[0]
    g = ATTN_HEADS
    return pl.pallas_call(
        functools.partial(_attn_kernel, t=t),
        grid=(N_HEADS // g, n // t),
        in_specs=[
            pl.BlockSpec((t, g * QK_PAD), lambda h, i: (i, h)),
            pl.BlockSpec((n, g * QK_PAD), lambda h, i: (0, h)),
            pl.BlockSpec((n, g * V_PAD), lambda h, i: (0, h)),
            pl.BlockSpec((N_META, g * QK_PAD), lambda h, i: (0, h)),
            pl.BlockSpec((N_META, g * V_PAD), lambda h, i: (0, h)),
        ],
        out_specs=pl.BlockSpec((t, g * V_HEAD), lambda h, i: (i, h)),
        out_shape=jax.ShapeDtypeStruct((n, N_HEADS * V_HEAD), BF16),
        scratch_shapes=[
            pltpu.VMEM((g, t, V_PAD), F32), pltpu.VMEM((g, t, LANES), F32),
            pltpu.VMEM((g, 2, t, t), F32), pltpu.VMEM((g, 2, t, t), BF16),
            pltpu.VMEM((g, 2, t, LANES), F32),
        ],
        compiler_params=_params(("parallel", "arbitrary")),
        name="attention",
    )(q, k, v, k_meta, v_meta)


def _merge_kernel(o_ref, u_ref, halo_ref, um_ref, gm_ref, gp_ref, wo_ref, pw_ref, ps_ref, wpo_ref,
                  out_ref, ext_sc, *, tm):
    i = pl.program_id(0)
    ext_sc[0:N_META, :] = jnp.where(i == 0, um_ref[...], halo_ref[...])
    ext_sc[N_META:, :] = u_ref[...]
    y_pool = jnp.zeros((tm, D_MODEL), F32)
    for g, w in enumerate(POOL_WINDOWS):
        lo, hi = g * POOL_GROUP_DIM, (g + 1) * POOL_GROUP_DIM
        u = ext_sc[N_META:, lo:hi]
        tot = u
        for d in range(1, w):
            tot = tot + ext_sc[N_META - d:N_META - d + tm, lo:hi]
        pooled = (tot * (1.0 / w) - u).astype(BF16)
        mixed = jnp.dot(pooled, pw_ref[g], preferred_element_type=F32) * ps_ref[:, lo:hi]
        y_pool = y_pool + jnp.dot(mixed.astype(BF16), wpo_ref[lo:hi, :], preferred_element_type=F32)
    y_mla = jnp.dot(o_ref[...], wo_ref[...], preferred_element_type=F32)
    merged = jax.nn.sigmoid(gm_ref[...]) * y_mla + jax.nn.sigmoid(gp_ref[...]) * y_pool
    out_ref[...] = merged.astype(out_ref.dtype)


def _merge(o, proj, u_meta, w_o, pool_w, pool_scale, w_pool_out, tm):
    n = o.shape[0]
    hb = tm // N_META
    const = lambda i: (0, 0)
    return pl.pallas_call(
        functools.partial(_merge_kernel, tm=tm),
        grid=(n // tm,),
        in_specs=[
            pl.BlockSpec((tm, D_MODEL), lambda i: (i, 0)),
            pl.BlockSpec((tm, POOL_WIDTH), lambda i: (i, COL_POOL // POOL_WIDTH)),
            pl.BlockSpec((N_META, POOL_WIDTH),
                         lambda i: (jnp.maximum(i * hb - 1, 0), COL_POOL // POOL_WIDTH)),
            pl.BlockSpec((N_META, POOL_WIDTH), lambda i: (0, COL_POOL // POOL_WIDTH)),
            pl.BlockSpec((tm, D_MODEL), lambda i: (i, COL_GM // D_MODEL)),
            pl.BlockSpec((tm, D_MODEL), lambda i: (i, COL_GP // D_MODEL)),
            pl.BlockSpec((D_MODEL, D_MODEL), const),
            pl.BlockSpec((len(POOL_WINDOWS), POOL_GROUP_DIM, POOL_GROUP_DIM), lambda i: (0, 0, 0)),
            pl.BlockSpec((1, POOL_WIDTH), const),
            pl.BlockSpec((POOL_WIDTH, D_MODEL), const),
        ],
        out_specs=pl.BlockSpec((tm, D_MODEL), lambda i: (i, 0)),
        out_shape=jax.ShapeDtypeStruct((n, D_MODEL), BF16),
        scratch_shapes=[pltpu.VMEM((tm + N_META, POOL_WIDTH), F32)],
        compiler_params=_params(("arbitrary",)),
        name="merge",
    )(o, proj, proj, u_meta, proj, proj, w_o, pool_w, pool_scale, w_pool_out)


def _route_kernel(mg_ref, x_ref, wout_ref, nw_ref, wr_ref, br_ref, h_ref, bn_ref, ids_ref, wts_ref):
    h = x_ref[...] + jnp.dot(mg_ref[...], wout_ref[...], preferred_element_type=F32)
    h_ref[...] = h
    bn = _rms(h, nw_ref[...])
    bn_ref[...] = bn
    logits = jnp.dot(bn.astype(BF16), wr_ref[...], preferred_element_type=F32) + br_ref[...]
    lane = lax.broadcasted_iota(jnp.int32, logits.shape, 1)
    lane_f = lane.astype(F32)

    def first_max(vals):
        vmax = jnp.max(vals, axis=-1, keepdims=True)
        idx = jnp.min(jnp.where(vals == vmax, lane_f, float(LANES)), axis=-1, keepdims=True)
        return vmax, idx.astype(jnp.int32)

    glog = jnp.where(lane < N_GROUPS, logits, -jnp.inf)
    gmax, gidx = first_max(glog)
    p_g = 1.0 / jnp.sum(jnp.exp(glog - gmax), axis=-1, keepdims=True)
    lo = N_GROUPS + gidx * EXPERTS_PER_GROUP
    elog = jnp.where((lane >= lo) & (lane < lo + EXPERTS_PER_GROUP), logits, -jnp.inf)
    v1, i1 = first_max(elog)
    v2, i2 = first_max(jnp.where(lane == i1, -jnp.inf, elog))
    e2 = jnp.exp(v2 - v1)
    den = 1.0 + e2
    w1 = p_g * (1.0 / den)
    w2 = p_g * (e2 / den)
    ids_ref[...] = jnp.where(lane == 0, i1 - N_GROUPS, jnp.where(lane == 1, i2 - N_GROUPS, 0))
    wts_ref[...] = jnp.where(lane == 0, w1, jnp.where(lane == 1, w2, 0.0))


def _route(merged, x, w_out, norm_w, w_router, b_router, tm):
    n = x.shape[0]
    const = lambda i: (0, 0)
    row = lambda width: pl.BlockSpec((tm, width), lambda i: (i, 0))
    return pl.pallas_call(
        _route_kernel,
        grid=(n // tm,),
        in_specs=[
            row(D_MODEL), row(D_MODEL),
            pl.BlockSpec((D_MODEL, D_MODEL), const),
            pl.BlockSpec((1, D_MODEL), const),
            pl.BlockSpec((D_MODEL, LANES), const),
            pl.BlockSpec((1, LANES), const),
        ],
        out_specs=[row(D_MODEL), row(D_MODEL), row(LANES), row(LANES)],
        out_shape=[jax.ShapeDtypeStruct((n, D_MODEL), F32), jax.ShapeDtypeStruct((n, D_MODEL), F32),
                   jax.ShapeDtypeStruct((n, LANES), jnp.int32), jax.ShapeDtypeStruct((n, LANES), F32)],
        compiler_params=_params(("parallel",)),
        name="route",
    )(merged, x, w_out, norm_w, w_router, b_router)


def _for_rows(n, fn):
    shift = int(math.log2(SUBLANES))
    n_groups = lax.shift_right_logical(n, shift)

    def group(g, _):
        for u in range(SUBLANES):
            fn(g, u)
        return 0

    def single(r, _):
        fn(n_groups, r)
        return 0

    lax.fori_loop(0, n_groups, group, 0)
    lax.fori_loop(0, n - n_groups * SUBLANES, single, 0)


def _experts_kernel(be_ref, nv_ref, sb_ref, tok_ref, dst_ref, bn_hbm, wg_ref, wu_ref, wd_ref, y_hbm,
                    xbuf, ybuf, wg_sc, wu_sc, wd_sc, gsem, ssem):
    b = pl.program_id(0)
    nb = pl.num_programs(0)
    slot = lax.rem(b, 2)

    def gather(blk, slt):
        base = sb_ref[blk]

        def copy(g, u):
            tok = tok_ref[base + g * SUBLANES + u]
            return pltpu.make_async_copy(bn_hbm.at[pl.ds(tok, 1)], xbuf.at[slt, g, pl.ds(u, 1)],
                                         gsem.at[slt])
        return copy

    def scatter(blk, slt):
        base = sb_ref[blk]

        def copy(g, u):
            dst = dst_ref[base + g * SUBLANES + u]
            return pltpu.make_async_copy(ybuf.at[slt, g, pl.ds(u, 1)], y_hbm.at[pl.ds(dst, 1)],
                                         ssem.at[slt])
        return copy

    def gather_start(blk, slt):
        copy = gather(blk, slt)
        _for_rows(nv_ref[blk], lambda g, u: copy(g, u).start())

    def scatter_wait(blk, slt):
        copy = scatter(blk, slt)
        _for_rows(nv_ref[blk], lambda g, u: copy(g, u).wait())

    @pl.when(b == 0)
    def _():
        xbuf[...] = jnp.zeros(xbuf.shape, xbuf.dtype)
        gather_start(0, 0)

    @pl.when(b + 1 < nb)
    def _():
        gather_start(b + 1, 1 - slot)

    @pl.when(nv_ref[b] > 0)
    def _():
        @pl.when((b == 0) | (be_ref[b] != be_ref[jnp.maximum(b - 1, 0)]))
        def _():
            wg_sc[...] = wg_ref[0].astype(BF16)
            wu_sc[...] = wu_ref[0].astype(BF16)
            wd_sc[...] = wd_ref[0].astype(BF16)

        wait_copy = gather(b, slot)
        _for_rows(nv_ref[b], lambda g, u: wait_copy(g, u).wait())
        x = xbuf[slot].reshape(ROUTE_BLOCK, D_MODEL).astype(BF16)
        gate = jnp.dot(x, wg_sc[...], preferred_element_type=F32)
        up = jnp.dot(x, wu_sc[...], preferred_element_type=F32)
        hdn = (gate * jax.nn.sigmoid(gate) * up).astype(BF16)
        y = jnp.dot(hdn, wd_sc[...], preferred_element_type=F32)
        ybuf[slot] = y.reshape(ROUTE_BLOCK // SUBLANES, SUBLANES, D_MODEL)
        start_copy = scatter(b, slot)
        _for_rows(nv_ref[b], lambda g, u: start_copy(g, u).start())

    @pl.when(b >= 1)
    def _():
        scatter_wait(b - 1, 1 - slot)

    @pl.when(b == nb - 1)
    def _():
        scatter_wait(b, slot)


def _experts(block_expert, block_rows, block_base, tok, dst, bn, w_gate, w_up, w_down):
    n = bn.shape[0]
    wmap = lambda b, be, nv, sb, tk, ds: (be[b], 0, 0)
    buf = pltpu.VMEM((2, ROUTE_BLOCK // SUBLANES, SUBLANES, D_MODEL), F32)
    grid_spec = pltpu.PrefetchScalarGridSpec(
        num_scalar_prefetch=5,
        grid=(block_expert.shape[0],),
        in_specs=[
            pl.BlockSpec(memory_space=pl.ANY),
            pl.BlockSpec((1, D_MODEL, EXPERT_FF), wmap),
            pl.BlockSpec((1, D_MODEL, EXPERT_FF), wmap),
            pl.BlockSpec((1, EXPERT_FF, D_MODEL), wmap),
        ],
        out_specs=pl.BlockSpec(memory_space=pl.ANY),
        scratch_shapes=[
            buf,
            buf,
            pltpu.VMEM((D_MODEL, EXPERT_FF), BF16),
            pltpu.VMEM((D_MODEL, EXPERT_FF), BF16),
            pltpu.VMEM((EXPERT_FF, D_MODEL), BF16),
            pltpu.SemaphoreType.DMA((2,)),
            pltpu.SemaphoreType.DMA((2,)),
        ],
    )
    return pl.pallas_call(
        _experts_kernel,
        grid_spec=grid_spec,
        out_shape=jax.ShapeDtypeStruct((TOP_K * n, D_MODEL), F32),
        compiler_params=_params(("arbitrary",)),
        name="experts",
    )(block_expert, block_rows, block_base, tok, dst, bn, w_gate, w_up, w_down)


def _route_blocks(flat_e):
    a = flat_e.shape[0]
    i32 = jnp.int32
    order = jnp.argsort(flat_e).astype(i32)
    counts = jnp.sum(flat_e[:, None] == jnp.arange(N_EXPERTS, dtype=i32)[None, :], axis=0, dtype=i32)
    padded = (counts + ROUTE_BLOCK - 1) // ROUTE_BLOCK * ROUTE_BLOCK
    pad_end = jnp.cumsum(padded).astype(i32)
    pad_start = pad_end - padded
    seg_start = jnp.cumsum(counts).astype(i32) - counts
    n_blocks = -(-(a + N_EXPERTS * (ROUTE_BLOCK - 1)) // ROUTE_BLOCK)
    blk = jnp.arange(n_blocks, dtype=i32)
    be = jnp.sum(pad_end[None, :] <= (blk * ROUTE_BLOCK)[:, None], axis=1, dtype=i32)
    be = jnp.minimum(be, N_EXPERTS - 1)
    off = blk * ROUTE_BLOCK - pad_start[be]
    rows = jnp.clip(counts[be] - off, 0, ROUTE_BLOCK).astype(i32)
    base = jnp.where(rows > 0, seg_start[be] + off, 0).astype(i32)
    n_used = pad_end[-1] // ROUTE_BLOCK
    be = jnp.where(blk < n_used, be, be[jnp.maximum(n_used - 1, 0)])
    tok = order // TOP_K
    dst = (order % TOP_K) * (a // TOP_K) + tok
    return be, rows, base, tok, dst


def _final_kernel(h_ref, y0_ref, y1_ref, wts_ref, nw_ref, o_ref):
    w = wts_ref[...]
    moe = y0_ref[...] * w[:, 0:1] + y1_ref[...] * w[:, 1:2]
    o_ref[...] = _rms(h_ref[...] + moe, nw_ref[...])


def _final(h, y, wts, norm_w, tm):
    n = h.shape[0]
    return pl.pallas_call(
        _final_kernel,
        grid=(n // tm,),
        in_specs=[
            pl.BlockSpec((tm, D_MODEL), lambda i: (i, 0)),
            pl.BlockSpec((tm, D_MODEL), lambda i: (i, 0)),
            pl.BlockSpec((tm, D_MODEL), lambda i: (i + n // tm, 0)),
            pl.BlockSpec((tm, LANES), lambda i: (i, 0)),
            pl.BlockSpec((1, D_MODEL), lambda i: (0, 0)),
        ],
        out_specs=pl.BlockSpec((tm, D_MODEL), lambda i: (i, 0)),
        out_shape=jax.ShapeDtypeStruct((n, D_MODEL), F32),
        compiler_params=_params(("parallel",)),
        name="final",
    )(h, y, y, wts, norm_w)


def _rope_tables(length):
    f32 = np.float32
    inv = (f32(1.0) / np.power(f32(ROPE_THETA), np.arange(0, QK_ROPE, 2, dtype=f32) / f32(QK_ROPE))).astype(f32)
    ang = (np.arange(length, dtype=f32)[:, None] * inv[None, :]).astype(f32)
    cos, sin = np.cos(ang).astype(f32), np.sin(ang).astype(f32)
    z32 = np.zeros_like(cos)
    z64 = np.zeros((length, LANES - QK_ROPE), f32)
    c = np.concatenate([cos, cos, z64], axis=1)
    s1 = np.concatenate([-sin, z32, z64], axis=1)
    s2 = np.concatenate([z32, sin, z64], axis=1)
    return c, s1, s2


def kernel(x, meta_tokens, norm_mix_w, w_in, q_norm_w, w_uq, kv_norm_w, w_ukv, w_o_mla, pool_w,
           pool_scale, w_pool_out, w_out, norm_ffn_w, w_router_group, b_router_group,
           w_router_expert, b_router_expert, w_exp_gate, w_exp_up, w_exp_down, final_norm_w):
    assert x.shape == (1, 8192, D_MODEL) and norm_mix_w.shape[0] == 1
    n = x.shape[1]
    xr = x[0]

    wi = w_in[0]
    w_a = wi[:, :Q_LORA + KV_LORA].astype(BF16)
    w_b = wi[:, Q_LORA + KV_LORA + QK_ROPE:].astype(BF16)
    w_rope = jnp.pad(wi[:, Q_LORA + KV_LORA:Q_LORA + KV_LORA + QK_ROPE],
                     ((0, 0), (0, LANES - QK_ROPE))).astype(BF16)
    w_q = jnp.pad(w_uq[0].reshape(Q_LORA, N_HEADS, QK_HEAD),
                  ((0, 0), (0, 0), (0, QK_PAD - QK_HEAD))).reshape(Q_LORA, N_HEADS * QK_PAD).astype(BF16)
    w_kv = w_ukv[0].reshape(KV_LORA, N_HEADS, QK_NOPE + V_HEAD)
    w_k = w_kv[:, :, :QK_NOPE].reshape(KV_LORA, N_HEADS * QK_NOPE).astype(BF16)
    w_v = w_kv[:, :, QK_NOPE:].reshape(KV_LORA, N_HEADS * V_HEAD).astype(BF16)
    w_router = jnp.pad(jnp.concatenate([w_router_group[0], w_router_expert[0]], axis=1),
                       ((0, 0), (0, LANES - N_GROUPS - N_EXPERTS))).astype(BF16)
    b_router = jnp.pad(jnp.concatenate([b_router_group[0], b_router_expert[0]]),
                       (0, LANES - N_GROUPS - N_EXPERTS))[None]
    c, s1, s2 = _rope_tables(N_META + n)

    proj, krope = _inproj(xr, norm_mix_w, w_a, w_b, w_rope, tm=1024)
    proj_m, krope_m = _inproj(meta_tokens, norm_mix_w, w_a, w_b, w_rope, tm=N_META)
    q = _qproj(proj, q_norm_w, w_q, c[N_META:], s1[N_META:], s2[N_META:], tm=512)
    k, v = _kvproj(proj, krope, kv_norm_w, w_k, w_v, c[N_META:], s1[N_META:], s2[N_META:], tm=512)
    k_m, v_m = _kvproj(proj_m, krope_m, kv_norm_w, w_k, w_v, c[:N_META], s1[:N_META], s2[:N_META],
                       tm=N_META)
    o = _attention(q, k, v, k_m, v_m, t=512)

    merged = _merge(o, proj, proj_m, w_o_mla[0].astype(BF16), pool_w[0].astype(BF16), pool_scale,
                    w_pool_out[0].astype(BF16), tm=256)
    h, bn, ids, wts = _route(merged, xr, w_out[0].astype(BF16), norm_ffn_w, w_router, b_router, tm=256)

    be, rows, base, tok, dst = _route_blocks(ids[:, :TOP_K].reshape(n * TOP_K))
    y = _experts(be, rows, base, tok, dst, bn, w_exp_gate[0], w_exp_up[0], w_exp_down[0])

    out = _final(h, y, wts, final_norm_w[None], tm=512)
    return out[None]
```

```python
import functools
import math

import jax
import jax.numpy as jnp
import numpy as np
from jax import lax
from jax.experimental import pallas as pl
from jax.experimental.pallas import tpu as pltpu

F32 = jnp.float32
BF16 = jnp.bfloat16

D_MODEL = 2048
N_META = 16
EPS = 1e-6
N_HEADS = 16
QK_NOPE = 128
QK_ROPE = 64
QK_HEAD = QK_NOPE + QK_ROPE
V_HEAD = 128
Q_LORA = 512
KV_LORA = 512
ROPE_THETA = 10000.0
POOL_WINDOWS = (2, 4, 8, 16)
POOL_WIDTH = 1024
POOL_GROUP_DIM = 256
N_GROUPS = 8
EXPERTS_PER_GROUP = 8
N_EXPERTS = 64
TOP_K = 2
EXPERT_FF = 512

LANES = 128
SUBLANES = 8
QK_PAD = 256
V_PAD = 256
MAIN_COLS = Q_LORA + KV_LORA + POOL_WIDTH + 2 * D_MODEL
COL_CQ, COL_CKV, COL_POOL, COL_GM, COL_GP = 0, 512, 1024, 2048, 4096
ATTN_HEADS = 2
ATTN_STRIP = 64
ROUTE_BLOCK = 256
NEG_BIG = -1e30

VMEM_LIMIT = 56 * 1024 * 1024


def _params(sem):
    return pltpu.CompilerParams(dimension_semantics=sem, vmem_limit_bytes=VMEM_LIMIT)


def _rms(x, w):
    return x * lax.rsqrt(jnp.mean(x * x, axis=-1, keepdims=True) + EPS) * w


def _rope128(v, c, s1, s2):
    return v * c + pltpu.roll(v, 96, 1) * s1 + pltpu.roll(v, 32, 1) * s2


def _inproj_kernel(x_ref, nw_ref, wa_ref, wb_ref, wr_ref, o_ref, kr_ref, a_sc):
    j = pl.program_id(1)

    @pl.when(j == 0)
    def _():
        a = _rms(x_ref[...], nw_ref[...]).astype(BF16)
        a_sc[...] = a
        kr_ref[...] = jnp.dot(a, wr_ref[...], preferred_element_type=F32)
        o_ref[...] = jnp.dot(a, wa_ref[...], preferred_element_type=F32)

    @pl.when(j > 0)
    def _():
        o_ref[...] = jnp.dot(a_sc[...], wb_ref[...], preferred_element_type=F32)


def _inproj(x, norm_w, w_a, w_b, w_rope, tm):
    m, k = x.shape
    tn = w_a.shape[1]
    n = tn + w_b.shape[1]
    return pl.pallas_call(
        _inproj_kernel,
        grid=(m // tm, n // tn),
        in_specs=[
            pl.BlockSpec((tm, k), lambda i, j: (i, 0)),
            pl.BlockSpec((1, k), lambda i, j: (0, 0)),
            pl.BlockSpec((k, tn), lambda i, j: (0, 0)),
            pl.BlockSpec((k, tn), lambda i, j: (0, jnp.maximum(j - 1, 0))),
            pl.BlockSpec((k, LANES), lambda i, j: (0, 0)),
        ],
        out_specs=[
            pl.BlockSpec((tm, tn), lambda i, j: (i, j)),
            pl.BlockSpec((tm, LANES), lambda i, j: (i, 0)),
        ],
        out_shape=[jax.ShapeDtypeStruct((m, n), F32), jax.ShapeDtypeStruct((m, LANES), F32)],
        scratch_shapes=[pltpu.VMEM((tm, k), BF16)],
        compiler_params=_params(("parallel", "arbitrary")),
        name="inproj",
    )(x, norm_w, w_a, w_b, w_rope)


def _qproj_kernel(cq_ref, nw_ref, w_ref, c_ref, s1_ref, s2_ref, q_ref, *, scale):
    a = _rms(cq_ref[...], nw_ref[...]).astype(BF16)
    c, s1, s2 = c_ref[...], s1_ref[...], s2_ref[...]
    for h in range(N_HEADS):
        qh = jnp.dot(a, w_ref[:, h * QK_PAD:(h + 1) * QK_PAD], preferred_element_type=F32) * scale
        q_ref[:, h * QK_PAD:h * QK_PAD + LANES] = qh[:, :LANES].astype(BF16)
        q_ref[:, h * QK_PAD + LANES:(h + 1) * QK_PAD] = _rope128(qh[:, LANES:], c, s1, s2).astype(BF16)


def _qproj(proj, norm_w, w_q, c, s1, s2, tm):
    m = proj.shape[0]
    n = N_HEADS * QK_PAD
    tab = pl.BlockSpec((tm, LANES), lambda i: (i, 0))
    return pl.pallas_call(
        functools.partial(_qproj_kernel, scale=QK_HEAD ** -0.5 * math.log2(math.e)),
        grid=(m // tm,),
        in_specs=[
            pl.BlockSpec((tm, Q_LORA), lambda i: (i, COL_CQ // Q_LORA)),
            pl.BlockSpec((1, Q_LORA), lambda i: (0, 0)),
            pl.BlockSpec((Q_LORA, n), lambda i: (0, 0)),
            tab, tab, tab,
        ],
        out_specs=pl.BlockSpec((tm, n), lambda i: (i, 0)),
        out_shape=jax.ShapeDtypeStruct((m, n), BF16),
        compiler_params=_params(("parallel",)),
        name="qproj",
    )(proj, norm_w, w_q, c, s1, s2)


def _kvproj_kernel(ckv_ref, kr_ref, nw_ref, wk_ref, wv_ref, c_ref, s1_ref, s2_ref, k_ref, v_ref):
    a = _rms(ckv_ref[...], nw_ref[...]).astype(BF16)
    kpe = _rope128(kr_ref[...], c_ref[...], s1_ref[...], s2_ref[...]).astype(BF16)
    ones = jnp.ones((a.shape[0], V_PAD - V_HEAD), BF16)
    for h2 in range(N_HEADS // 2):
        kn = jnp.dot(a, wk_ref[:, h2 * 256:(h2 + 1) * 256], preferred_element_type=F32).astype(BF16)
        vv = jnp.dot(a, wv_ref[:, h2 * 256:(h2 + 1) * 256], preferred_element_type=F32).astype(BF16)
        for d in range(2):
            h = 2 * h2 + d
            k_ref[:, h * QK_PAD:h * QK_PAD + LANES] = kn[:, d * LANES:(d + 1) * LANES]
            k_ref[:, h * QK_PAD + LANES:(h + 1) * QK_PAD] = kpe
            v_ref[:, h * V_PAD:h * V_PAD + V_HEAD] = vv[:, d * LANES:(d + 1) * LANES]
            v_ref[:, h * V_PAD + V_HEAD:(h + 1) * V_PAD] = ones


def _kvproj(proj, krope, norm_w, w_k, w_v, c, s1, s2, tm):
    m = proj.shape[0]
    tab = pl.BlockSpec((tm, LANES), lambda i: (i, 0))
    return pl.pallas_call(
        _kvproj_kernel,
        grid=(m // tm,),
        in_specs=[
            pl.BlockSpec((tm, KV_LORA), lambda i: (i, COL_CKV // KV_LORA)),
            tab,
            pl.BlockSpec((1, KV_LORA), lambda i: (0, 0)),
            pl.BlockSpec((KV_LORA, N_HEADS * QK_NOPE), lambda i: (0, 0)),
            pl.BlockSpec((KV_LORA, N_HEADS * V_HEAD), lambda i: (0, 0)),
            tab, tab, tab,
        ],
        out_specs=[
            pl.BlockSpec((tm, N_HEADS * QK_PAD), lambda i: (i, 0)),
            pl.BlockSpec((tm, N_HEADS * V_PAD), lambda i: (i, 0)),
        ],
        out_shape=[jax.ShapeDtypeStruct((m, N_HEADS * QK_PAD), BF16),
                   jax.ShapeDtypeStruct((m, N_HEADS * V_PAD), BF16)],
        compiler_params=_params(("parallel",)),
        name="kvproj",
    )(proj, krope, norm_w, w_k, w_v, c, s1, s2)


def _attn_kernel(q_ref, k_ref, v_ref, km_ref, vm_ref, o_ref, acc_sc, m_sc, s_sc, p_sc, al_sc, *, t):
    i = pl.program_id(1)
    heads = range(ATTN_HEADS)
    nt = (((1,), (1,)), ((), ()))
    qs = [q_ref[:, g * QK_PAD:(g + 1) * QK_PAD] for g in heads]

    def chunk(ref, g, width, c):
        return ref[pl.ds(pl.multiple_of(c * t, t), t), g * width:(g + 1) * width]

    def scores(g, c, slot):
        s_sc[g, slot] = lax.dot_general(qs[g], chunk(k_ref, g, QK_PAD, c), nt, preferred_element_type=F32)

    def softmax(g, src, dst, masked):
        for r in range(t // ATTN_STRIP):
            rows = slice(r * ATTN_STRIP, (r + 1) * ATTN_STRIP)
            s = s_sc[g, src, rows, :]
            if masked:
                row = r * ATTN_STRIP + lax.broadcasted_iota(jnp.int32, s.shape, 0)
                s = jnp.where(lax.broadcasted_iota(jnp.int32, s.shape, 1) <= row, s, NEG_BIG)
            m_old = m_sc[g, rows, :]
            m_new = jnp.maximum(m_old, jnp.broadcast_to(jnp.max(s, axis=-1, keepdims=True), m_old.shape))
            p_sc[g, dst, rows, :] = jnp.concatenate(
                [jnp.exp2(s[:, c * LANES:(c + 1) * LANES] - m_new) for c in range(t // LANES)],
                axis=1).astype(BF16)
            al_sc[g, dst, rows, :] = jnp.exp2(m_old - m_new)
            m_sc[g, rows, :] = m_new

    def accumulate(g, slot, c):
        pv = jnp.dot(p_sc[g, slot], chunk(v_ref, g, V_PAD, c), preferred_element_type=F32)
        alpha = al_sc[g, slot]
        acc_sc[g] = jnp.concatenate([alpha, alpha], axis=1) * acc_sc[g] + pv

    def stage(k, src, with_next=True):
        for g in heads:
            if with_next:
                scores(g, jnp.minimum(k + 1, i - 1), 1 - src)
            accumulate(g, src, jnp.where(k == 0, i, k - 1))
            softmax(g, src, 1 - src, masked=False)

    for g in heads:
        s = lax.dot_general(qs[g], km_ref[:, g * QK_PAD:(g + 1) * QK_PAD], nt, preferred_element_type=F32)
        m0 = jnp.max(s, axis=-1, keepdims=True)
        acc_sc[g] = jnp.dot(jnp.exp2(s - m0).astype(BF16), vm_ref[:, g * V_PAD:(g + 1) * V_PAD],
                            preferred_element_type=F32)
        m_sc[g] = jnp.broadcast_to(m0, (t, LANES))
        scores(g, i, 1)
        scores(g, 0, 0)
        softmax(g, 1, 0, masked=True)

    def pair(j, _):
        stage(2 * j, 0)
        stage(2 * j + 1, 1)
        return 0

    lax.fori_loop(0, i // 2, pair, 0)
    odd = lax.rem(i, 2) == 1

    @pl.when(odd)
    def _():
        stage(i - 1, 0, with_next=False)
        for g in heads:
            accumulate(g, 1, i - 1)

    @pl.when(jnp.logical_not(odd))
    def _():
        for g in heads:
            accumulate(g, 0, jnp.maximum(i - 1, 0))

    for g in heads:
        acc = acc_sc[g]
        o_ref[:, g * V_HEAD:(g + 1) * V_HEAD] = (acc[:, :V_HEAD] / acc[:, V_HEAD:]).astype(o_ref.dtype)


def _attention(q, k, v, k_meta, v_meta, t):
    n = q.shape[0]
    g = ATTN_HEADS
    return pl.pallas_call(
        functools.partial(_attn_kernel, t=t),
        grid=(N_HEADS // g, n // t),
        in_specs=[
            pl.BlockSpec((t, g * QK_PAD), lambda h, i: (i, h)),
            pl.BlockSpec((n, g * QK_PAD), lambda h, i: (0, h)),
            pl.BlockSpec((n, g * V_PAD), lambda h, i: (0, h)),
            pl.BlockSpec((N_META, g * QK_PAD), lambda h, i: (0, h)),
            pl.BlockSpec((N_META, g * V_PAD), lambda h, i: (0, h)),
        ],
        out_specs=pl.BlockSpec((t, g * V_HEAD), lambda h, i: (i, h)),
        out_shape=jax.ShapeDtypeStruct((n, N_HEADS * V_HEAD), BF16),
        scratch_shapes=[
            pltpu.VMEM((g, t, V_PAD), F32), pltpu.VMEM((g, t, LANES), F32),
            pltpu.VMEM((g, 2, t, t), F32), pltpu.VMEM((g, 2, t, t), BF16),
            pltpu.VMEM((g, 2, t, LANES), F32),
        ],
        compiler_params=_params(("parallel", "arbitrary")),
        name="attention",
    )(q, k, v, k_meta, v_meta)


def _merge_kernel(o_ref, u_ref, halo_ref, um_ref, gm_ref, gp_ref, wo_ref, pw_ref, ps_ref, wpo_ref,
                  out_ref, ext_sc, *, tm):
    i = pl.program_id(0)
    ext_sc[0:N_META, :] = jnp.where(i == 0, um_ref[...], halo_ref[...])
    ext_sc[N_META:, :] = u_ref[...]
    y_pool = jnp.zeros((tm, D_MODEL), F32)
    for g, w in enumerate(POOL_WINDOWS):
        lo, hi = g * POOL_GROUP_DIM, (g + 1) * POOL_GROUP_DIM
        u = ext_sc[N_META:, lo:hi]
        tot = u
        for d in range(1, w):
            tot = tot + ext_sc[N_META - d:N_META - d + tm, lo:hi]
        pooled = (tot * (1.0 / w) - u).astype(BF16)
        mixed = jnp.dot(pooled, pw_ref[g], preferred_element_type=F32) * ps_ref[:, lo:hi]
        y_pool = y_pool + jnp.dot(mixed.astype(BF16), wpo_ref[lo:hi, :], preferred_element_type=F32)
    y_mla = jnp.dot(o_ref[...], wo_ref[...], preferred_element_type=F32)
    merged = jax.nn.sigmoid(gm_ref[...]) * y_mla + jax.nn.sigmoid(gp_ref[...]) * y_pool
    out_ref[...] = merged.astype(out_ref.dtype)


def _merge(o, proj, u_meta, w_o, pool_w, pool_scale, w_pool_out, tm):
    n = o.shape[0]
    hb = tm // N_META
    const = lambda i: (0, 0)
    return pl.pallas_call(
        functools.partial(_merge_kernel, tm=tm),
        grid=(n // tm,),
        in_specs=[
            pl.BlockSpec((tm, D_MODEL), lambda i: (i, 0)),
            pl.BlockSpec((tm, POOL_WIDTH), lambda i: (i, COL_POOL // POOL_WIDTH)),
            pl.BlockSpec((N_META, POOL_WIDTH),
                         lambda i: (jnp.maximum(i * hb - 1, 0), COL_POOL // POOL_WIDTH)),
            pl.BlockSpec((N_META, POOL_WIDTH), lambda i: (0, COL_POOL // POOL_WIDTH)),
            pl.BlockSpec((tm, D_MODEL), lambda i: (i, COL_GM // D_MODEL)),
            pl.BlockSpec((tm, D_MODEL), lambda i: (i, COL_GP // D_MODEL)),
            pl.BlockSpec((D_MODEL, D_MODEL), const, pipeline_mode=pl.Buffered(1)),
            pl.BlockSpec((len(POOL_WINDOWS), POOL_GROUP_DIM, POOL_GROUP_DIM), lambda i: (0, 0, 0)),
            pl.BlockSpec((1, POOL_WIDTH), const),
            pl.BlockSpec((POOL_WIDTH, D_MODEL), const, pipeline_mode=pl.Buffered(1)),
        ],
        out_specs=pl.BlockSpec((tm, D_MODEL), lambda i: (i, 0)),
        out_shape=jax.ShapeDtypeStruct((n, D_MODEL), BF16),
        scratch_shapes=[pltpu.VMEM((tm + N_META, POOL_WIDTH), F32)],
        compiler_params=_params(("arbitrary",)),
        name="merge",
    )(o, proj, proj, u_meta, proj, proj, w_o, pool_w, pool_scale, w_pool_out)


def _route_kernel(mg_ref, x_ref, wout_ref, nw_ref, wr_ref, br_ref, h_ref, bn_ref, ids_ref, wts_ref):
    h = x_ref[...] + jnp.dot(mg_ref[...], wout_ref[...], preferred_element_type=F32)
    h_ref[...] = h
    bn = _rms(h, nw_ref[...])
    bn_ref[...] = bn
    logits = jnp.dot(bn.astype(BF16), wr_ref[...], preferred_element_type=F32) + br_ref[...]
    lane = lax.broadcasted_iota(jnp.int32, logits.shape, 1)
    lane_f = lane.astype(F32)

    def first_max(vals):
        vmax = jnp.max(vals, axis=-1, keepdims=True)
        idx = jnp.min(jnp.where(vals == vmax, lane_f, float(LANES)), axis=-1, keepdims=True)
        return vmax, idx.astype(jnp.int32)

    glog = jnp.where(lane < N_GROUPS, logits, -jnp.inf)
    gmax, gidx = first_max(glog)
    p_g = 1.0 / jnp.sum(jnp.exp(glog - gmax), axis=-1, keepdims=True)
    lo = N_GROUPS + gidx * EXPERTS_PER_GROUP
    elog = jnp.where((lane >= lo) & (lane < lo + EXPERTS_PER_GROUP), logits, -jnp.inf)
    v1, i1 = first_max(elog)
    v2, i2 = first_max(jnp.where(lane == i1, -jnp.inf, elog))
    e2 = jnp.exp(v2 - v1)
    den = 1.0 + e2
    w1 = p_g * (1.0 / den)
    w2 = p_g * (e2 / den)
    ids_ref[...] = jnp.where(lane == 0, i1 - N_GROUPS, jnp.where(lane == 1, i2 - N_GROUPS, 0))
    wts_ref[...] = jnp.where(lane == 0, w1, jnp.where(lane == 1, w2, 0.0))


def _route(merged, x, w_out, norm_w, w_router, b_router, tm):
    n = x.shape[0]
    const = lambda i: (0, 0)
    row = lambda width: pl.BlockSpec((tm, width), lambda i: (i, 0))
    return pl.pallas_call(
        _route_kernel,
        grid=(n // tm,),
        in_specs=[
            row(D_MODEL), row(D_MODEL),
            pl.BlockSpec((D_MODEL, D_MODEL), const, pipeline_mode=pl.Buffered(1)),
            pl.BlockSpec((1, D_MODEL), const),
            pl.BlockSpec((D_MODEL, LANES), const),
            pl.BlockSpec((1, LANES), const),
        ],
        out_specs=[row(D_MODEL), row(D_MODEL), row(LANES), row(LANES)],
        out_shape=[jax.ShapeDtypeStruct((n, D_MODEL), F32), jax.ShapeDtypeStruct((n, D_MODEL), F32),
                   jax.ShapeDtypeStruct((n, LANES), jnp.int32), jax.ShapeDtypeStruct((n, LANES), F32)],
        compiler_params=_params(("parallel",)),
        name="route",
    )(merged, x, w_out, norm_w, w_router, b_router)


def _for_rows(n, fn):
    shift = int(math.log2(SUBLANES))
    n_groups = lax.shift_right_logical(n, shift)

    def group(g, _):
        for u in range(SUBLANES):
            fn(g, u)
        return 0

    def single(r, _):
        fn(n_groups, r)
        return 0

    lax.fori_loop(0, n_groups, group, 0)
    lax.fori_loop(0, n - n_groups * SUBLANES, single, 0)


def _experts_kernel(be_ref, nv_ref, sb_ref, tok_ref, dst_ref, bn_hbm, wg_ref, wu_ref, wd_ref, y_hbm,
                    xbuf, ybuf, wg_sc, wu_sc, wd_sc, gsem, ssem):
    b = pl.program_id(0)
    nb = pl.num_programs(0)
    slot = lax.rem(b, 2)

    def gather(blk, slt):
        base = sb_ref[blk]

        def copy(g, u):
            tok = tok_ref[base + g * SUBLANES + u]
            return pltpu.make_async_copy(bn_hbm.at[pl.ds(tok, 1)], xbuf.at[slt, g, pl.ds(u, 1)],
                                         gsem.at[slt])
        return copy

    def scatter(blk, slt):
        base = sb_ref[blk]

        def copy(g, u):
            dst = dst_ref[base + g * SUBLANES + u]
            return pltpu.make_async_copy(ybuf.at[slt, g, pl.ds(u, 1)], y_hbm.at[pl.ds(dst, 1)],
                                         ssem.at[slt])
        return copy

    def gather_start(blk, slt):
        copy = gather(blk, slt)
        _for_rows(nv_ref[blk], lambda g, u: copy(g, u).start())

    def scatter_wait(blk, slt):
        copy = scatter(blk, slt)
        _for_rows(nv_ref[blk], lambda g, u: copy(g, u).wait())

    @pl.when(b == 0)
    def _():
        xbuf[...] = jnp.zeros(xbuf.shape, xbuf.dtype)
        gather_start(0, 0)

    @pl.when(b + 1 < nb)
    def _():
        gather_start(b + 1, 1 - slot)

    @pl.when(nv_ref[b] > 0)
    def _():
        wait_copy = gather(b, slot)
        _for_rows(nv_ref[b], lambda g, u: wait_copy(g, u).wait())
        x = xbuf[slot].reshape(ROUTE_BLOCK, D_MODEL)
        gate = jnp.dot(x, wg_ref[0], preferred_element_type=F32)
        up = jnp.dot(x, wu_ref[0], preferred_element_type=F32)
        hdn = gate * jax.nn.sigmoid(gate) * up
        y = jnp.dot(hdn, wd_ref[0], preferred_element_type=F32)
        ybuf[slot] = y.reshape(ROUTE_BLOCK // SUBLANES, SUBLANES, D_MODEL)
        start_copy = scatter(b, slot)
        _for_rows(nv_ref[b], lambda g, u: start_copy(g, u).start())

    @pl.when(b >= 1)
    def _():
        scatter_wait(b - 1, 1 - slot)

    @pl.when(b == nb - 1)
    def _():
        scatter_wait(b, slot)


def _experts(block_expert, block_rows, block_base, tok, dst, bn, w_gate, w_up, w_down):
    n = bn.shape[0]
    wmap = lambda b, be, nv, sb, tk, ds: (be[b], 0, 0)
    buf = pltpu.VMEM((2, ROUTE_BLOCK // SUBLANES, SUBLANES, D_MODEL), F32)
    grid_spec = pltpu.PrefetchScalarGridSpec(
        num_scalar_prefetch=5,
        grid=(block_expert.shape[0],),
        in_specs=[
            pl.BlockSpec(memory_space=pl.ANY),
            pl.BlockSpec((1, D_MODEL, EXPERT_FF), wmap),
            pl.BlockSpec((1, D_MODEL, EXPERT_FF), wmap),
            pl.BlockSpec((1, EXPERT_FF, D_MODEL), wmap),
        ],
        out_specs=pl.BlockSpec(memory_space=pl.ANY),
        scratch_shapes=[
            buf,
            buf,
            pltpu.VMEM((D_MODEL, EXPERT_FF), BF16),
            pltpu.VMEM((D_MODEL, EXPERT_FF), BF16),
            pltpu.VMEM((EXPERT_FF, D_MODEL), BF16),
            pltpu.SemaphoreType.DMA((2,)),
            pltpu.SemaphoreType.DMA((2,)),
        ],
    )
    return pl.pallas_call(
        _experts_kernel,
        grid_spec=grid_spec,
        out_shape=jax.ShapeDtypeStruct((TOP_K * n, D_MODEL), F32),
        compiler_params=_params(("arbitrary",)),
        name="experts",
    )(block_expert, block_rows, block_base, tok, dst, bn, w_gate, w_up, w_down)


def _route_blocks(flat_e):
    a = flat_e.shape[0]
    i32 = jnp.int32
    order = jnp.argsort(flat_e).astype(i32)
    counts = jnp.sum(flat_e[:, None] == jnp.arange(N_EXPERTS, dtype=i32)[None, :], axis=0, dtype=i32)
    padded = (counts + ROUTE_BLOCK - 1) // ROUTE_BLOCK * ROUTE_BLOCK
    pad_end = jnp.cumsum(padded).astype(i32)
    pad_start = pad_end - padded
    seg_start = jnp.cumsum(counts).astype(i32) - counts
    n_blocks = -(-(a + N_EXPERTS * (ROUTE_BLOCK - 1)) // ROUTE_BLOCK)
    blk = jnp.arange(n_blocks, dtype=i32)
    be = jnp.sum(pad_end[None, :] <= (blk * ROUTE_BLOCK)[:, None], axis=1, dtype=i32)
    be = jnp.minimum(be, N_EXPERTS - 1)
    off = blk * ROUTE_BLOCK - pad_start[be]
    rows = jnp.clip(counts[be] - off, 0, ROUTE_BLOCK).astype(i32)
    base = jnp.where(rows > 0, seg_start[be] + off, 0).astype(i32)
    n_used = pad_end[-1] // ROUTE_BLOCK
    be = jnp.where(blk < n_used, be, be[jnp.maximum(n_used - 1, 0)])
    tok = order // TOP_K
    dst = (order % TOP_K) * (a // TOP_K) + tok
    return be, rows, base, tok, dst


def _final_kernel(h_ref, y0_ref, y1_ref, wts_ref, nw_ref, o_ref):
    w = wts_ref[...]
    moe = y0_ref[...] * w[:, 0:1] + y1_ref[...] * w[:, 1:2]
    o_ref[...] = _rms(h_ref[...] + moe, nw_ref[...])


def _final(h, y, wts, norm_w, tm):
    n = h.shape[0]
    return pl.pallas_call(
        _final_kernel,
        grid=(n // tm,),
        in_specs=[
            pl.BlockSpec((tm, D_MODEL), lambda i: (i, 0)),
            pl.BlockSpec((tm, D_MODEL), lambda i: (i, 0)),
            pl.BlockSpec((tm, D_MODEL), lambda i: (i + n // tm, 0)),
            pl.BlockSpec((tm, LANES), lambda i: (i, 0)),
            pl.BlockSpec((1, D_MODEL), lambda i: (0, 0)),
        ],
        out_specs=pl.BlockSpec((tm, D_MODEL), lambda i: (i, 0)),
        out_shape=jax.ShapeDtypeStruct((n, D_MODEL), F32),
        compiler_params=_params(("parallel",)),
        name="final",
    )(h, y, y, wts, norm_w)


def _rope_tables(length):
    f32 = np.float32
    inv = (f32(1.0) / np.power(f32(ROPE_THETA), np.arange(0, QK_ROPE, 2, dtype=f32) / f32(QK_ROPE))).astype(f32)
    ang = (np.arange(length, dtype=f32)[:, None] * inv[None, :]).astype(f32)
    cos, sin = np.cos(ang).astype(f32), np.sin(ang).astype(f32)
    z32 = np.zeros_like(cos)
    z64 = np.zeros((length, LANES - QK_ROPE), f32)
    c = np.concatenate([cos, cos, z64], axis=1)
    s1 = np.concatenate([-sin, z32, z64], axis=1)
    s2 = np.concatenate([z32, sin, z64], axis=1)
    return c, s1, s2


def kernel(x, meta_tokens, norm_mix_w, w_in, q_norm_w, w_uq, kv_norm_w, w_ukv, w_o_mla, pool_w,
           pool_scale, w_pool_out, w_out, norm_ffn_w, w_router_group, b_router_group,
           w_router_expert, b_router_expert, w_exp_gate, w_exp_up, w_exp_down, final_norm_w):
    assert x.shape == (1, 8192, D_MODEL) and norm_mix_w.shape[0] == 1
    n = x.shape[1]
    xr = x[0]

    wi = w_in[0]
    w_a = wi[:, :Q_LORA + KV_LORA].astype(BF16)
    w_b = wi[:, Q_LORA + KV_LORA + QK_ROPE:].astype(BF16)
    w_rope = jnp.pad(wi[:, Q_LORA + KV_LORA:Q_LORA + KV_LORA + QK_ROPE],
                     ((0, 0), (0, LANES - QK_ROPE))).astype(BF16)
    w_q = jnp.pad(w_uq[0].reshape(Q_LORA, N_HEADS, QK_HEAD),
                  ((0, 0), (0, 0), (0, QK_PAD - QK_HEAD))).reshape(Q_LORA, N_HEADS * QK_PAD).astype(BF16)
    w_kv = w_ukv[0].reshape(KV_LORA, N_HEADS, QK_NOPE + V_HEAD)
    w_k = w_kv[:, :, :QK_NOPE].reshape(KV_LORA, N_HEADS * QK_NOPE).astype(BF16)
    w_v = w_kv[:, :, QK_NOPE:].reshape(KV_LORA, N_HEADS * V_HEAD).astype(BF16)
    w_router = jnp.pad(jnp.concatenate([w_router_group[0], w_router_expert[0]], axis=1),
                       ((0, 0), (0, LANES - N_GROUPS - N_EXPERTS))).astype(BF16)
    b_router = jnp.pad(jnp.concatenate([b_router_group[0], b_router_expert[0]]),
                       (0, LANES - N_GROUPS - N_EXPERTS))[None]
    c, s1, s2 = _rope_tables(N_META + n)

    proj, krope = _inproj(xr, norm_mix_w, w_a, w_b, w_rope, tm=1024)
    proj_m, krope_m = _inproj(meta_tokens, norm_mix_w, w_a, w_b, w_rope, tm=N_META)
    q = _qproj(proj, q_norm_w, w_q, c[N_META:], s1[N_META:], s2[N_META:], tm=512)
    k, v = _kvproj(proj, krope, kv_norm_w, w_k, w_v, c[N_META:], s1[N_META:], s2[N_META:], tm=512)
    k_m, v_m = _kvproj(proj_m, krope_m, kv_norm_w, w_k, w_v, c[:N_META], s1[:N_META], s2[:N_META],
                       tm=N_META)
    o = _attention(q, k, v, k_m, v_m, t=512)

    merged = _merge(o, proj, proj_m, w_o_mla[0].astype(BF16), pool_w[0].astype(BF16), pool_scale,
                    w_pool_out[0].astype(BF16), tm=512)
    h, bn, ids, wts = _route(merged, xr, w_out[0].astype(BF16), norm_ffn_w, w_router, b_router, tm=512)

    be, rows, base, tok, dst = _route_blocks(ids[:, :TOP_K].reshape(n * TOP_K))
    y = _experts(be, rows, base, tok, dst, bn, w_exp_gate[0], w_exp_up[0], w_exp_down[0])

    out = _final(h, y, wts, final_norm_w[None], tm=512)
    return out[None]
```

```python
import functools
import math

import jax
import jax.numpy as jnp
import numpy as np
from jax import lax
from jax.experimental import pallas as pl
from jax.experimental.pallas import tpu as pltpu

F32 = jnp.float32
BF16 = jnp.bfloat16

D_MODEL = 2048
N_META = 16
EPS = 1e-6
N_HEADS = 16
QK_NOPE = 128
QK_ROPE = 64
QK_HEAD = QK_NOPE + QK_ROPE
V_HEAD = 128
Q_LORA = 512
KV_LORA = 512
ROPE_THETA = 10000.0
POOL_WINDOWS = (2, 4, 8, 16)
POOL_WIDTH = 1024
POOL_GROUP_DIM = 256
N_GROUPS = 8
EXPERTS_PER_GROUP = 8
N_EXPERTS = 64
TOP_K = 2
EXPERT_FF = 512

LANES = 128
SUBLANES = 8
QK_PAD = 256
V_PAD = 256
MAIN_COLS = Q_LORA + KV_LORA + POOL_WIDTH + 2 * D_MODEL
COL_CQ, COL_CKV, COL_POOL, COL_GM, COL_GP = 0, 512, 1024, 2048, 4096
ATTN_HEADS = 2
ATTN_STRIP = 64
ROUTE_BLOCK = 256
NEG_BIG = -1e30

VMEM_LIMIT = 56 * 1024 * 1024


def _params(sem):
    return pltpu.CompilerParams(dimension_semantics=sem, vmem_limit_bytes=VMEM_LIMIT)


def _rms(x, w):
    return x * lax.rsqrt(jnp.mean(x * x, axis=-1, keepdims=True) + EPS) * w


def _rope128(v, c, s1, s2):
    return v * c + pltpu.roll(v, 96, 1) * s1 + pltpu.roll(v, 32, 1) * s2


def _winprep_kernel(w_ref, wa_ref, wb_ref, wr_ref):
    n_a = Q_LORA + KV_LORA
    wa_ref[...] = w_ref[:, :n_a].astype(BF16)
    wr_ref[...] = w_ref[:, n_a:n_a + LANES].astype(BF16)
    wb_ref[...] = w_ref[:, n_a + QK_ROPE:].astype(BF16)


def _winprep(w, tk):
    k, n = w.shape
    n_a = Q_LORA + KV_LORA
    n_b = n - n_a - QK_ROPE
    return pl.pallas_call(
        _winprep_kernel,
        grid=(k // tk,),
        in_specs=[pl.BlockSpec((tk, n), lambda i: (i, 0))],
        out_specs=[pl.BlockSpec((tk, n_a), lambda i: (i, 0)), pl.BlockSpec((tk, n_b), lambda i: (i, 0)),
                   pl.BlockSpec((tk, LANES), lambda i: (i, 0))],
        out_shape=[jax.ShapeDtypeStruct((k, n_a), BF16), jax.ShapeDtypeStruct((k, n_b), BF16),
                   jax.ShapeDtypeStruct((k, LANES), BF16)],
        compiler_params=_params(("parallel",)),
        name="winprep",
    )(w)


def _inproj_kernel(x_ref, nw_ref, wa_ref, wb_ref, wr_ref, o_ref, kr_ref, a_sc):
    j = pl.program_id(1)

    @pl.when(j == 0)
    def _():
        a = _rms(x_ref[...], nw_ref[...]).astype(BF16)
        a_sc[...] = a
        kr_ref[...] = jnp.dot(a, wr_ref[...], preferred_element_type=F32)
        o_ref[...] = jnp.dot(a, wa_ref[...], preferred_element_type=F32)

    @pl.when(j > 0)
    def _():
        o_ref[...] = jnp.dot(a_sc[...], wb_ref[...], preferred_element_type=F32)


def _inproj(x, norm_w, w_a, w_b, w_rope, tm):
    m, k = x.shape
    tn = w_a.shape[1]
    n = tn + w_b.shape[1]
    return pl.pallas_call(
        _inproj_kernel,
        grid=(m // tm, n // tn),
        in_specs=[
            pl.BlockSpec((tm, k), lambda i, j: (i, 0)),
            pl.BlockSpec((1, k), lambda i, j: (0, 0)),
            pl.BlockSpec((k, tn), lambda i, j: (0, 0)),
            pl.BlockSpec((k, tn), lambda i, j: (0, jnp.maximum(j - 1, 0))),
            pl.BlockSpec((k, LANES), lambda i, j: (0, 0)),
        ],
        out_specs=[
            pl.BlockSpec((tm, tn), lambda i, j: (i, j)),
            pl.BlockSpec((tm, LANES), lambda i, j: (i, 0)),
        ],
        out_shape=[jax.ShapeDtypeStruct((m, n), F32), jax.ShapeDtypeStruct((m, LANES), F32)],
        scratch_shapes=[pltpu.VMEM((tm, k), BF16)],
        compiler_params=_params(("parallel", "arbitrary")),
        name="inproj",
    )(x, norm_w, w_a, w_b, w_rope)


def _qproj_kernel(cq_ref, nw_ref, w_ref, c_ref, s1_ref, s2_ref, q_ref, *, scale):
    a = _rms(cq_ref[...], nw_ref[...]).astype(BF16)
    c, s1, s2 = c_ref[...], s1_ref[...], s2_ref[...]
    for h in range(N_HEADS):
        qh = jnp.dot(a, w_ref[:, h * QK_PAD:(h + 1) * QK_PAD], preferred_element_type=F32) * scale
        q_ref[:, h * QK_PAD:h * QK_PAD + LANES] = qh[:, :LANES].astype(BF16)
        q_ref[:, h * QK_PAD + LANES:(h + 1) * QK_PAD] = _rope128(qh[:, LANES:], c, s1, s2).astype(BF16)


def _qproj(proj, norm_w, w_q, c, s1, s2, tm):
    m = proj.shape[0]
    n = N_HEADS * QK_PAD
    tab = pl.BlockSpec((tm, LANES), lambda i: (i, 0))
    return pl.pallas_call(
        functools.partial(_qproj_kernel, scale=QK_HEAD ** -0.5 * math.log2(math.e)),
        grid=(m // tm,),
        in_specs=[
            pl.BlockSpec((tm, Q_LORA), lambda i: (i, COL_CQ // Q_LORA)),
            pl.BlockSpec((1, Q_LORA), lambda i: (0, 0)),
            pl.BlockSpec((Q_LORA, n), lambda i: (0, 0)),
            tab, tab, tab,
        ],
        out_specs=pl.BlockSpec((tm, n), lambda i: (i, 0)),
        out_shape=jax.ShapeDtypeStruct((m, n), BF16),
        compiler_params=_params(("parallel",)),
        name="qproj",
    )(proj, norm_w, w_q, c, s1, s2)


def _kvproj_kernel(ckv_ref, kr_ref, nw_ref, wk_ref, wv_ref, c_ref, s1_ref, s2_ref, k_ref, v_ref):
    a = _rms(ckv_ref[...], nw_ref[...]).astype(BF16)
    kpe = _rope128(kr_ref[...], c_ref[...], s1_ref[...], s2_ref[...]).astype(BF16)
    ones = jnp.ones((a.shape[0], V_PAD - V_HEAD), BF16)
    for h2 in range(N_HEADS // 2):
        kn = jnp.dot(a, wk_ref[:, h2 * 256:(h2 + 1) * 256], preferred_element_type=F32).astype(BF16)
        vv = jnp.dot(a, wv_ref[:, h2 * 256:(h2 + 1) * 256], preferred_element_type=F32).astype(BF16)
        for d in range(2):
            h = 2 * h2 + d
            k_ref[:, h * QK_PAD:h * QK_PAD + LANES] = kn[:, d * LANES:(d + 1) * LANES]
            k_ref[:, h * QK_PAD + LANES:(h + 1) * QK_PAD] = kpe
            v_ref[:, h * V_PAD:h * V_PAD + V_HEAD] = vv[:, d * LANES:(d + 1) * LANES]
            v_ref[:, h * V_PAD + V_HEAD:(h + 1) * V_PAD] = ones


def _kvproj(proj, krope, norm_w, w_k, w_v, c, s1, s2, tm):
    m = proj.shape[0]
    tab = pl.BlockSpec((tm, LANES), lambda i: (i, 0))
    return pl.pallas_call(
        _kvproj_kernel,
        grid=(m // tm,),
        in_specs=[
            pl.BlockSpec((tm, KV_LORA), lambda i: (i, COL_CKV // KV_LORA)),
            tab,
            pl.BlockSpec((1, KV_LORA), lambda i: (0, 0)),
            pl.BlockSpec((KV_LORA, N_HEADS * QK_NOPE), lambda i: (0, 0)),
            pl.BlockSpec((KV_LORA, N_HEADS * V_HEAD), lambda i: (0, 0)),
            tab, tab, tab,
        ],
        out_specs=[
            pl.BlockSpec((tm, N_HEADS * QK_PAD), lambda i: (i, 0)),
            pl.BlockSpec((tm, N_HEADS * V_PAD), lambda i: (i, 0)),
        ],
        out_shape=[jax.ShapeDtypeStruct((m, N_HEADS * QK_PAD), BF16),
                   jax.ShapeDtypeStruct((m, N_HEADS * V_PAD), BF16)],
        compiler_params=_params(("parallel",)),
        name="kvproj",
    )(proj, krope, norm_w, w_k, w_v, c, s1, s2)


def _attn_kernel(q_ref, k_ref, v_ref, km_ref, vm_ref, o_ref, acc_sc, m_sc, s_sc, p_sc, al_sc, *, t):
    i = pl.program_id(1)
    heads = range(ATTN_HEADS)
    nt = (((1,), (1,)), ((), ()))
    qs = [q_ref[:, g * QK_PAD:(g + 1) * QK_PAD] for g in heads]

    def chunk(ref, g, width, c):
        return ref[pl.ds(pl.multiple_of(c * t, t), t), g * width:(g + 1) * width]

    def scores(g, c, slot):
        s_sc[g, slot] = lax.dot_general(qs[g], chunk(k_ref, g, QK_PAD, c), nt, preferred_element_type=F32)

    def softmax(g, src, dst, masked):
        for r in range(t // ATTN_STRIP):
            rows = slice(r * ATTN_STRIP, (r + 1) * ATTN_STRIP)
            s = s_sc[g, src, rows, :]
            if masked:
                row = r * ATTN_STRIP + lax.broadcasted_iota(jnp.int32, s.shape, 0)
                s = jnp.where(lax.broadcasted_iota(jnp.int32, s.shape, 1) <= row, s, NEG_BIG)
            m_old = m_sc[g, rows, :]
            m_new = jnp.maximum(m_old, jnp.broadcast_to(jnp.max(s, axis=-1, keepdims=True), m_old.shape))
            p_sc[g, dst, rows, :] = jnp.concatenate(
                [jnp.exp2(s[:, c * LANES:(c + 1) * LANES] - m_new) for c in range(t // LANES)],
                axis=1).astype(BF16)
            al_sc[g, dst, rows, :] = jnp.exp2(m_old - m_new)
            m_sc[g, rows, :] = m_new

    def accumulate(g, slot, c):
        pv = jnp.dot(p_sc[g, slot], chunk(v_ref, g, V_PAD, c), preferred_element_type=F32)
        alpha = al_sc[g, slot]
        acc_sc[g] = jnp.concatenate([alpha, alpha], axis=1) * acc_sc[g] + pv

    def stage(k, src, with_next=True):
        for g in heads:
            if with_next:
                scores(g, jnp.minimum(k + 1, i - 1), 1 - src)
            accumulate(g, src, jnp.where(k == 0, i, k - 1))
            softmax(g, src, 1 - src, masked=False)

    for g in heads:
        s = lax.dot_general(qs[g], km_ref[:, g * QK_PAD:(g + 1) * QK_PAD], nt, preferred_element_type=F32)
        m0 = jnp.max(s, axis=-1, keepdims=True)
        acc_sc[g] = jnp.dot(jnp.exp2(s - m0).astype(BF16), vm_ref[:, g * V_PAD:(g + 1) * V_PAD],
                            preferred_element_type=F32)
        m_sc[g] = jnp.broadcast_to(m0, (t, LANES))
        scores(g, i, 1)
        scores(g, 0, 0)
        softmax(g, 1, 0, masked=True)

    def pair(j, _):
        stage(2 * j, 0)
        stage(2 * j + 1, 1)
        return 0

    lax.fori_loop(0, i // 2, pair, 0)
    odd = lax.rem(i, 2) == 1

    @pl.when(odd)
    def _():
        stage(i - 1, 0, with_next=False)
        for g in heads:
            accumulate(g, 1, i - 1)

    @pl.when(jnp.logical_not(odd))
    def _():
        for g in heads:
            accumulate(g, 0, jnp.maximum(i - 1, 0))

    for g in heads:
        acc = acc_sc[g]
        o_ref[:, g * V_HEAD:(g + 1) * V_HEAD] = (acc[:, :V_HEAD] / acc[:, V_HEAD:]).astype(o_ref.dtype)


def _attention(q, k, v, k_meta, v_meta, t):
    n = q.shape[0]
    g = ATTN_HEADS
    return pl.pallas_call(
        functools.partial(_attn_kernel, t=t),
        grid=(N_HEADS // g, n // t),
        in_specs=[
            pl.BlockSpec((t, g * QK_PAD), lambda h, i: (i, h)),
            pl.BlockSpec((n, g * QK_PAD), lambda h, i: (0, h)),
            pl.BlockSpec((n, g * V_PAD), lambda h, i: (0, h)),
            pl.BlockSpec((N_META, g * QK_PAD), lambda h, i: (0, h)),
            pl.BlockSpec((N_META, g * V_PAD), lambda h, i: (0, h)),
        ],
        out_specs=pl.BlockSpec((t, g * V_HEAD), lambda h, i: (i, h)),
        out_shape=jax.ShapeDtypeStruct((n, N_HEADS * V_HEAD), BF16),
        scratch_shapes=[
            pltpu.VMEM((g, t, V_PAD), F32), pltpu.VMEM((g, t, LANES), F32),
            pltpu.VMEM((g, 2, t, t), F32), pltpu.VMEM((g, 2, t, t), BF16),
            pltpu.VMEM((g, 2, t, LANES), F32),
        ],
        compiler_params=_params(("parallel", "arbitrary")),
        name="attention",
    )(q, k, v, k_meta, v_meta)


def _merge_kernel(o_ref, u_ref, halo_ref, um_ref, gm_ref, gp_ref, wo_ref, pw_ref, ps_ref, wpo_ref,
                  out_ref, ext_sc, *, tm):
    i = pl.program_id(0)
    ext_sc[0:N_META, :] = jnp.where(i == 0, um_ref[...], halo_ref[...])
    ext_sc[N_META:, :] = u_ref[...]
    y_pool = jnp.zeros((tm, D_MODEL), F32)
    for g, w in enumerate(POOL_WINDOWS):
        lo, hi = g * POOL_GROUP_DIM, (g + 1) * POOL_GROUP_DIM
        u = ext_sc[N_META:, lo:hi]
        tot = u
        for d in range(1, w):
            tot = tot + ext_sc[N_META - d:N_META - d + tm, lo:hi]
        pooled = (tot * (1.0 / w) - u).astype(BF16)
        mixed = jnp.dot(pooled, pw_ref[g], preferred_element_type=F32) * ps_ref[:, lo:hi]
        y_pool = y_pool + jnp.dot(mixed.astype(BF16), wpo_ref[lo:hi, :], preferred_element_type=F32)
    y_mla = jnp.dot(o_ref[...], wo_ref[...], preferred_element_type=F32)
    merged = jax.nn.sigmoid(gm_ref[...]) * y_mla + jax.nn.sigmoid(gp_ref[...]) * y_pool
    out_ref[...] = merged.astype(out_ref.dtype)


def _merge(o, proj, u_meta, w_o, pool_w, pool_scale, w_pool_out, tm):
    n = o.shape[0]
    hb = tm // N_META
    const = lambda i: (0, 0)
    return pl.pallas_call(
        functools.partial(_merge_kernel, tm=tm),
        grid=(n // tm,),
        in_specs=[
            pl.BlockSpec((tm, D_MODEL), lambda i: (i, 0)),
            pl.BlockSpec((tm, POOL_WIDTH), lambda i: (i, COL_POOL // POOL_WIDTH)),
            pl.BlockSpec((N_META, POOL_WIDTH),
                         lambda i: (jnp.maximum(i * hb - 1, 0), COL_POOL // POOL_WIDTH)),
            pl.BlockSpec((N_META, POOL_WIDTH), lambda i: (0, COL_POOL // POOL_WIDTH)),
            pl.BlockSpec((tm, D_MODEL), lambda i: (i, COL_GM // D_MODEL)),
            pl.BlockSpec((tm, D_MODEL), lambda i: (i, COL_GP // D_MODEL)),
            pl.BlockSpec((D_MODEL, D_MODEL), const, pipeline_mode=pl.Buffered(1)),
            pl.BlockSpec((len(POOL_WINDOWS), POOL_GROUP_DIM, POOL_GROUP_DIM), lambda i: (0, 0, 0)),
            pl.BlockSpec((1, POOL_WIDTH), const),
            pl.BlockSpec((POOL_WIDTH, D_MODEL), const, pipeline_mode=pl.Buffered(1)),
        ],
        out_specs=pl.BlockSpec((tm, D_MODEL), lambda i: (i, 0)),
        out_shape=jax.ShapeDtypeStruct((n, D_MODEL), BF16),
        scratch_shapes=[pltpu.VMEM((tm + N_META, POOL_WIDTH), F32)],
        compiler_params=_params(("arbitrary",)),
        name="merge",
    )(o, proj, proj, u_meta, proj, proj, w_o, pool_w, pool_scale, w_pool_out)


def _route_kernel(mg_ref, x_ref, wout_ref, nw_ref, wr_ref, br_ref, h_ref, bn_ref, ids_ref, wts_ref):
    h = x_ref[...] + jnp.dot(mg_ref[...], wout_ref[...], preferred_element_type=F32)
    h_ref[...] = h
    bn = _rms(h, nw_ref[...])
    bn_ref[...] = bn
    logits = jnp.dot(bn.astype(BF16), wr_ref[...], preferred_element_type=F32) + br_ref[...]
    lt = logits.T
    tm = lt.shape[1]
    sub = lax.broadcasted_iota(jnp.int32, (SUBLANES, tm), 0)
    sub_f = sub.astype(F32)

    def first_max(vals):
        vmax = jnp.max(vals, axis=0, keepdims=True)
        idx = jnp.min(jnp.where(vals == vmax, sub_f, float(SUBLANES)), axis=0, keepdims=True)
        return vmax, idx.astype(jnp.int32)

    glog = lt[:N_GROUPS, :]
    gmax, gidx = first_max(glog)
    p_g = 1.0 / jnp.sum(jnp.exp(glog - gmax), axis=0, keepdims=True)
    e_in = jnp.zeros((EXPERTS_PER_GROUP, tm), F32)
    for g in range(N_GROUPS):
        lo = N_GROUPS + g * EXPERTS_PER_GROUP
        e_in = jnp.where(gidx == g, lt[lo:lo + EXPERTS_PER_GROUP, :], e_in)
    v1, i1 = first_max(e_in)
    v2, i2 = first_max(jnp.where(sub == i1, -jnp.inf, e_in))
    e2 = jnp.exp(v2 - v1)
    den = 1.0 + e2
    w1 = p_g * (1.0 / den)
    w2 = p_g * (e2 / den)
    base = gidx * EXPERTS_PER_GROUP
    ids_ref[...] = jnp.where(sub == 0, base + i1, jnp.where(sub == 1, base + i2, 0))
    wts_ref[...] = jnp.where(sub == 0, w1, jnp.where(sub == 1, w2, 0.0))


def _route(merged, x, w_out, norm_w, w_router, b_router, tm):
    n = x.shape[0]
    const = lambda i: (0, 0)
    row = lambda width: pl.BlockSpec((tm, width), lambda i: (i, 0))
    return pl.pallas_call(
        _route_kernel,
        grid=(n // tm,),
        in_specs=[
            row(D_MODEL), row(D_MODEL),
            pl.BlockSpec((D_MODEL, D_MODEL), const, pipeline_mode=pl.Buffered(1)),
            pl.BlockSpec((1, D_MODEL), const),
            pl.BlockSpec((D_MODEL, LANES), const),
            pl.BlockSpec((1, LANES), const),
        ],
        out_specs=[row(D_MODEL), row(D_MODEL),
                   pl.BlockSpec((SUBLANES, tm), lambda i: (0, i)), pl.BlockSpec((SUBLANES, tm), lambda i: (0, i))],
        out_shape=[jax.ShapeDtypeStruct((n, D_MODEL), F32), jax.ShapeDtypeStruct((n, D_MODEL), F32),
                   jax.ShapeDtypeStruct((SUBLANES, n), jnp.int32), jax.ShapeDtypeStruct((SUBLANES, n), F32)],
        compiler_params=_params(("parallel",)),
        name="route",
    )(merged, x, w_out, norm_w, w_router, b_router)


def _for_rows(n, fn):
    shift = int(math.log2(SUBLANES))
    n_groups = lax.shift_right_logical(n, shift)

    def group(g, _):
        for u in range(SUBLANES):
            fn(g, u)
        return 0

    def single(r, _):
        fn(n_groups, r)
        return 0

    lax.fori_loop(0, n_groups, group, 0)
    lax.fori_loop(0, n - n_groups * SUBLANES, single, 0)


def _experts_kernel(be_ref, nv_ref, sb_ref, tok_ref, dst_ref, bn_hbm, wg_ref, wu_ref, wd_ref, y_hbm,
                    xbuf, ybuf, wg_sc, wu_sc, wd_sc, gsem, ssem):
    b = pl.program_id(0)
    nb = pl.num_programs(0)
    slot = lax.rem(b, 2)

    def gather(blk, slt):
        base = sb_ref[blk]

        def copy(g, u):
            tok = tok_ref[base + g * SUBLANES + u]
            return pltpu.make_async_copy(bn_hbm.at[pl.ds(tok, 1)], xbuf.at[slt, g, pl.ds(u, 1)],
                                         gsem.at[slt])
        return copy

    def scatter(blk, slt):
        base = sb_ref[blk]

        def copy(g, u):
            dst = dst_ref[base + g * SUBLANES + u]
            return pltpu.make_async_copy(ybuf.at[slt, g, pl.ds(u, 1)], y_hbm.at[pl.ds(dst, 1)],
                                         ssem.at[slt])
        return copy

    def gather_start(blk, slt):
        copy = gather(blk, slt)
        _for_rows(nv_ref[blk], lambda g, u: copy(g, u).start())

    def scatter_wait(blk, slt):
        copy = scatter(blk, slt)
        _for_rows(nv_ref[blk], lambda g, u: copy(g, u).wait())

    @pl.when(b == 0)
    def _():
        xbuf[...] = jnp.zeros(xbuf.shape, xbuf.dtype)
        gather_start(0, 0)

    @pl.when(b + 1 < nb)
    def _():
        gather_start(b + 1, 1 - slot)

    @pl.when(nv_ref[b] > 0)
    def _():
        wait_copy = gather(b, slot)
        _for_rows(nv_ref[b], lambda g, u: wait_copy(g, u).wait())
        x = xbuf[slot].reshape(ROUTE_BLOCK, D_MODEL)
        gate = jnp.dot(x, wg_ref[0], preferred_element_type=F32)
        up = jnp.dot(x, wu_ref[0], preferred_element_type=F32)
        hdn = gate * jax.nn.sigmoid(gate) * up
        y = jnp.dot(hdn, wd_ref[0], preferred_element_type=F32)
        ybuf[slot] = y.reshape(ROUTE_BLOCK // SUBLANES, SUBLANES, D_MODEL)
        start_copy = scatter(b, slot)
        _for_rows(nv_ref[b], lambda g, u: start_copy(g, u).start())

    @pl.when(b >= 1)
    def _():
        scatter_wait(b - 1, 1 - slot)

    @pl.when(b == nb - 1)
    def _():
        scatter_wait(b, slot)


def _experts(block_expert, block_rows, block_base, tok, dst, bn, w_gate, w_up, w_down):
    n = bn.shape[0]
    wmap = lambda b, be, nv, sb, tk, ds: (be[b], 0, 0)
    buf = pltpu.VMEM((2, ROUTE_BLOCK // SUBLANES, SUBLANES, D_MODEL), F32)
    grid_spec = pltpu.PrefetchScalarGridSpec(
        num_scalar_prefetch=5,
        grid=(block_expert.shape[0],),
        in_specs=[
            pl.BlockSpec(memory_space=pl.ANY),
            pl.BlockSpec((1, D_MODEL, EXPERT_FF), wmap),
            pl.BlockSpec((1, D_MODEL, EXPERT_FF), wmap),
            pl.BlockSpec((1, EXPERT_FF, D_MODEL), wmap),
        ],
        out_specs=pl.BlockSpec(memory_space=pl.ANY),
        scratch_shapes=[
            buf,
            buf,
            pltpu.VMEM((D_MODEL, EXPERT_FF), BF16),
            pltpu.VMEM((D_MODEL, EXPERT_FF), BF16),
            pltpu.VMEM((EXPERT_FF, D_MODEL), BF16),
            pltpu.SemaphoreType.DMA((2,)),
            pltpu.SemaphoreType.DMA((2,)),
        ],
    )
    return pl.pallas_call(
        _experts_kernel,
        grid_spec=grid_spec,
        out_shape=jax.ShapeDtypeStruct((TOP_K * n, D_MODEL), F32),
        compiler_params=_params(("arbitrary",)),
        name="experts",
    )(block_expert, block_rows, block_base, tok, dst, bn, w_gate, w_up, w_down)


def _route_blocks(flat_e):
    a = flat_e.shape[0]
    i32 = jnp.int32
    order = jnp.argsort(flat_e).astype(i32)
    counts = jnp.sum(flat_e[:, None] == jnp.arange(N_EXPERTS, dtype=i32)[None, :], axis=0, dtype=i32)
    padded = (counts + ROUTE_BLOCK - 1) // ROUTE_BLOCK * ROUTE_BLOCK
    pad_end = jnp.cumsum(padded).astype(i32)
    pad_start = pad_end - padded
    seg_start = jnp.cumsum(counts).astype(i32) - counts
    n_blocks = -(-(a + N_EXPERTS * (ROUTE_BLOCK - 1)) // ROUTE_BLOCK)
    blk = jnp.arange(n_blocks, dtype=i32)
    be = jnp.sum(pad_end[None, :] <= (blk * ROUTE_BLOCK)[:, None], axis=1, dtype=i32)
    be = jnp.minimum(be, N_EXPERTS - 1)
    off = blk * ROUTE_BLOCK - pad_start[be]
    rows = jnp.clip(counts[be] - off, 0, ROUTE_BLOCK).astype(i32)
    base = jnp.where(rows > 0, seg_start[be] + off, 0).astype(i32)
    n_used = pad_end[-1] // ROUTE_BLOCK
    be = jnp.where(blk < n_used, be, be[jnp.maximum(n_used - 1, 0)])
    tok = order // TOP_K
    dst = (order % TOP_K) * (a // TOP_K) + tok
    return be, rows, base, tok, dst


def _final_kernel(h_ref, y0_ref, y1_ref, wts_ref, nw_ref, o_ref):
    w = wts_ref[...]
    moe = y0_ref[...] * w[:, 0:1] + y1_ref[...] * w[:, 1:2]
    o_ref[...] = _rms(h_ref[...] + moe, nw_ref[...])


def _final(h, y, wts, norm_w, tm):
    n = h.shape[0]
    return pl.pallas_call(
        _final_kernel,
        grid=(n // tm,),
        in_specs=[
            pl.BlockSpec((tm, D_MODEL), lambda i: (i, 0)),
            pl.BlockSpec((tm, D_MODEL), lambda i: (i, 0)),
            pl.BlockSpec((tm, D_MODEL), lambda i: (i + n // tm, 0)),
            pl.BlockSpec((tm, TOP_K), lambda i: (i, 0)),
            pl.BlockSpec((1, D_MODEL), lambda i: (0, 0)),
        ],
        out_specs=pl.BlockSpec((tm, D_MODEL), lambda i: (i, 0)),
        out_shape=jax.ShapeDtypeStruct((n, D_MODEL), F32),
        compiler_params=_params(("parallel",)),
        name="final",
    )(h, y, y, wts, norm_w)


def _rope_tables(length):
    f32 = np.float32
    inv = (f32(1.0) / np.power(f32(ROPE_THETA), np.arange(0, QK_ROPE, 2, dtype=f32) / f32(QK_ROPE))).astype(f32)
    ang = (np.arange(length, dtype=f32)[:, None] * inv[None, :]).astype(f32)
    cos, sin = np.cos(ang).astype(f32), np.sin(ang).astype(f32)
    z32 = np.zeros_like(cos)
    z64 = np.zeros((length, LANES - QK_ROPE), f32)
    c = np.concatenate([cos, cos, z64], axis=1)
    s1 = np.concatenate([-sin, z32, z64], axis=1)
    s2 = np.concatenate([z32, sin, z64], axis=1)
    return c, s1, s2


def kernel(x, meta_tokens, norm_mix_w, w_in, q_norm_w, w_uq, kv_norm_w, w_ukv, w_o_mla, pool_w,
           pool_scale, w_pool_out, w_out, norm_ffn_w, w_router_group, b_router_group,
           w_router_expert, b_router_expert, w_exp_gate, w_exp_up, w_exp_down, final_norm_w):
    assert x.shape == (1, 8192, D_MODEL) and norm_mix_w.shape[0] == 1
    n = x.shape[1]
    xr = x[0]

    w_a, w_b, w_rope = _winprep(w_in[0], tk=256)
    w_q = jnp.pad(w_uq[0].reshape(Q_LORA, N_HEADS, QK_HEAD),
                  ((0, 0), (0, 0), (0, QK_PAD - QK_HEAD))).reshape(Q_LORA, N_HEADS * QK_PAD).astype(BF16)
    w_kv = w_ukv[0].reshape(KV_LORA, N_HEADS, QK_NOPE + V_HEAD)
    w_k = w_kv[:, :, :QK_NOPE].reshape(KV_LORA, N_HEADS * QK_NOPE).astype(BF16)
    w_v = w_kv[:, :, QK_NOPE:].reshape(KV_LORA, N_HEADS * V_HEAD).astype(BF16)
    w_router = jnp.pad(jnp.concatenate([w_router_group[0], w_router_expert[0]], axis=1),
                       ((0, 0), (0, LANES - N_GROUPS - N_EXPERTS))).astype(BF16)
    b_router = jnp.pad(jnp.concatenate([b_router_group[0], b_router_expert[0]]),
                       (0, LANES - N_GROUPS - N_EXPERTS))[None]
    c, s1, s2 = _rope_tables(N_META + n)

    proj, krope = _inproj(xr, norm_mix_w, w_a, w_b, w_rope, tm=1024)
    proj_m, krope_m = _inproj(meta_tokens, norm_mix_w, w_a, w_b, w_rope, tm=N_META)
    q = _qproj(proj, q_norm_w, w_q, c[N_META:], s1[N_META:], s2[N_META:], tm=512)
    k, v = _kvproj(proj, krope, kv_norm_w, w_k, w_v, c[N_META:], s1[N_META:], s2[N_META:], tm=512)
    k_m, v_m = _kvproj(proj_m, krope_m, kv_norm_w, w_k, w_v, c[:N_META], s1[:N_META], s2[:N_META],
                       tm=N_META)
    o = _attention(q, k, v, k_m, v_m, t=512)

    merged = _merge(o, proj, proj_m, w_o_mla[0].astype(BF16), pool_w[0].astype(BF16), pool_scale,
                    w_pool_out[0].astype(BF16), tm=512)
    h, bn, ids, wts = _route(merged, xr, w_out[0].astype(BF16), norm_ffn_w, w_router, b_router, tm=512)

    be, rows, base, tok, dst = _route_blocks(ids[:TOP_K].T.reshape(n * TOP_K))
    wts = wts[:TOP_K].T
    y = _experts(be, rows, base, tok, dst, bn, w_exp_gate[0], w_exp_up[0], w_exp_down[0])

    out = _final(h, y, wts, final_norm_w[None], tm=512)
    return out[None]
```

```python
import functools
import math

import jax
import jax.numpy as jnp
import numpy as np
from jax import lax
from jax.experimental import pallas as pl
from jax.experimental.pallas import tpu as pltpu

F32 = jnp.float32
BF16 = jnp.bfloat16

D_MODEL = 2048
N_META = 16
EPS = 1e-6
N_HEADS = 16
QK_NOPE = 128
QK_ROPE = 64
QK_HEAD = QK_NOPE + QK_ROPE
V_HEAD = 128
Q_LORA = 512
KV_LORA = 512
ROPE_THETA = 10000.0
POOL_WINDOWS = (2, 4, 8, 16)
POOL_WIDTH = 1024
POOL_GROUP_DIM = 256
N_GROUPS = 8
EXPERTS_PER_GROUP = 8
N_EXPERTS = 64
TOP_K = 2
EXPERT_FF = 512

LANES = 128
SUBLANES = 8
QK_PAD = 256
V_PAD = 256
MAIN_COLS = Q_LORA + KV_LORA + POOL_WIDTH + 2 * D_MODEL
COL_CQ, COL_CKV, COL_POOL, COL_GM, COL_GP = 0, 512, 1024, 2048, 4096
ATTN_HEADS = 2
ATTN_STRIP = 64
ROUTE_BLOCK = 256
NEG_BIG = -1e30

VMEM_LIMIT = 56 * 1024 * 1024


def _params(sem):
    return pltpu.CompilerParams(dimension_semantics=sem, vmem_limit_bytes=VMEM_LIMIT)


def _rms(x, w):
    return x * lax.rsqrt(jnp.mean(x * x, axis=-1, keepdims=True) + EPS) * w


def _rope128(v, c, s1, s2):
    return v * c + pltpu.roll(v, 96, 1) * s1 + pltpu.roll(v, 32, 1) * s2


def _inproj_kernel(x_ref, nw_ref, wa_ref, wr_ref, wb_ref, o_ref, kr_ref, a_sc):
    j = pl.program_id(1)
    nt = (((1,), (1,)), ((), ()))

    @pl.when(j == 0)
    def _():
        a = _rms(x_ref[...], nw_ref[...]).astype(BF16)
        a_sc[...] = a
        kr_ref[...] = lax.dot_general(a, wr_ref[...], nt, preferred_element_type=F32)
        o_ref[...] = lax.dot_general(a, wa_ref[...], nt, preferred_element_type=F32)

    @pl.when(j > 0)
    def _():
        o_ref[...] = lax.dot_general(a_sc[...], wb_ref[...], nt, preferred_element_type=F32)


def _inproj(x, norm_w, w_t, tm, tn):
    m, k = x.shape
    n_a = Q_LORA + KV_LORA
    assert tn == n_a
    n = w_t.shape[0] - QK_ROPE
    return pl.pallas_call(
        _inproj_kernel,
        grid=(m // tm, n // tn),
        in_specs=[
            pl.BlockSpec((tm, k), lambda i, j: (i, 0)),
            pl.BlockSpec((1, k), lambda i, j: (0, 0)),
            pl.BlockSpec((tn, k), lambda i, j: (0, 0)),
            pl.BlockSpec((LANES, k), lambda i, j: (n_a // LANES, 0)),
            pl.BlockSpec((pl.Element(tn), pl.Element(k)),
                         lambda i, j: (pl.multiple_of(n_a + QK_ROPE + jnp.maximum(j - 1, 0) * tn, QK_ROPE), 0)),
        ],
        out_specs=[
            pl.BlockSpec((tm, tn), lambda i, j: (i, j)),
            pl.BlockSpec((tm, LANES), lambda i, j: (i, 0)),
        ],
        out_shape=[jax.ShapeDtypeStruct((m, n), F32), jax.ShapeDtypeStruct((m, LANES), F32)],
        scratch_shapes=[pltpu.VMEM((tm, k), BF16)],
        compiler_params=_params(("parallel", "arbitrary")),
        name="inproj",
    )(x, norm_w, w_t, w_t, w_t)


def _qproj_kernel(cq_ref, nw_ref, w_ref, c_ref, s1_ref, s2_ref, q_ref, *, scale):
    a = _rms(cq_ref[...], nw_ref[...]).astype(BF16)
    c, s1, s2 = c_ref[...], s1_ref[...], s2_ref[...]
    for h in range(N_HEADS):
        qh = jnp.dot(a, w_ref[:, h * QK_PAD:(h + 1) * QK_PAD], preferred_element_type=F32) * scale
        q_ref[:, h * QK_PAD:h * QK_PAD + LANES] = qh[:, :LANES].astype(BF16)
        q_ref[:, h * QK_PAD + LANES:(h + 1) * QK_PAD] = _rope128(qh[:, LANES:], c, s1, s2).astype(BF16)


def _qproj(proj, norm_w, w_q, c, s1, s2, tm):
    m = proj.shape[0]
    n = N_HEADS * QK_PAD
    tab = pl.BlockSpec((tm, LANES), lambda i: (i, 0))
    return pl.pallas_call(
        functools.partial(_qproj_kernel, scale=QK_HEAD ** -0.5 * math.log2(math.e)),
        grid=(m // tm,),
        in_specs=[
            pl.BlockSpec((tm, Q_LORA), lambda i: (i, COL_CQ // Q_LORA)),
            pl.BlockSpec((1, Q_LORA), lambda i: (0, 0)),
            pl.BlockSpec((Q_LORA, n), lambda i: (0, 0)),
            tab, tab, tab,
        ],
        out_specs=pl.BlockSpec((tm, n), lambda i: (i, 0)),
        out_shape=jax.ShapeDtypeStruct((m, n), BF16),
        compiler_params=_params(("parallel",)),
        name="qproj",
    )(proj, norm_w, w_q, c, s1, s2)


def _kvproj_kernel(ckv_ref, kr_ref, nw_ref, wk_ref, wv_ref, c_ref, s1_ref, s2_ref, k_ref, v_ref):
    a = _rms(ckv_ref[...], nw_ref[...]).astype(BF16)
    kpe = _rope128(kr_ref[...], c_ref[...], s1_ref[...], s2_ref[...]).astype(BF16)
    ones = jnp.ones((a.shape[0], V_PAD - V_HEAD), BF16)
    for h2 in range(N_HEADS // 2):
        kn = jnp.dot(a, wk_ref[:, h2 * 256:(h2 + 1) * 256], preferred_element_type=F32).astype(BF16)
        vv = jnp.dot(a, wv_ref[:, h2 * 256:(h2 + 1) * 256], preferred_element_type=F32).astype(BF16)
        for d in range(2):
            h = 2 * h2 + d
            k_ref[:, h * QK_PAD:h * QK_PAD + LANES] = kn[:, d * LANES:(d + 1) * LANES]
            k_ref[:, h * QK_PAD + LANES:(h + 1) * QK_PAD] = kpe
            v_ref[:, h * V_PAD:h * V_PAD + V_HEAD] = vv[:, d * LANES:(d + 1) * LANES]
            v_ref[:, h * V_PAD + V_HEAD:(h + 1) * V_PAD] = ones


def _kvproj(proj, krope, norm_w, w_k, w_v, c, s1, s2, tm):
    m = proj.shape[0]
    tab = pl.BlockSpec((tm, LANES), lambda i: (i, 0))
    return pl.pallas_call(
        _kvproj_kernel,
        grid=(m // tm,),
        in_specs=[
            pl.BlockSpec((tm, KV_LORA), lambda i: (i, COL_CKV // KV_LORA)),
            tab,
            pl.BlockSpec((1, KV_LORA), lambda i: (0, 0)),
            pl.BlockSpec((KV_LORA, N_HEADS * QK_NOPE), lambda i: (0, 0)),
            pl.BlockSpec((KV_LORA, N_HEADS * V_HEAD), lambda i: (0, 0)),
            tab, tab, tab,
        ],
        out_specs=[
            pl.BlockSpec((tm, N_HEADS * QK_PAD), lambda i: (i, 0)),
            pl.BlockSpec((tm, N_HEADS * V_PAD), lambda i: (i, 0)),
        ],
        out_shape=[jax.ShapeDtypeStruct((m, N_HEADS * QK_PAD), BF16),
                   jax.ShapeDtypeStruct((m, N_HEADS * V_PAD), BF16)],
        compiler_params=_params(("parallel",)),
        name="kvproj",
    )(proj, krope, norm_w, w_k, w_v, c, s1, s2)


def _attn_kernel(q_ref, k_ref, v_ref, km_ref, vm_ref, o_ref, acc_sc, m_sc, s_sc, p_sc, al_sc, *, t):
    i = pl.program_id(1)
    heads = range(ATTN_HEADS)
    nt = (((1,), (1,)), ((), ()))
    qs = [q_ref[:, g * QK_PAD:(g + 1) * QK_PAD] for g in heads]

    def chunk(ref, g, width, c):
        return ref[pl.ds(pl.multiple_of(c * t, t), t), g * width:(g + 1) * width]

    def scores(g, c, slot):
        s_sc[g, slot] = lax.dot_general(qs[g], chunk(k_ref, g, QK_PAD, c), nt, preferred_element_type=F32)

    def softmax(g, src, dst, masked):
        for r in range(t // ATTN_STRIP):
            rows = slice(r * ATTN_STRIP, (r + 1) * ATTN_STRIP)
            s = s_sc[g, src, rows, :]
            if masked:
                row = r * ATTN_STRIP + lax.broadcasted_iota(jnp.int32, s.shape, 0)
                s = jnp.where(lax.broadcasted_iota(jnp.int32, s.shape, 1) <= row, s, NEG_BIG)
            m_old = m_sc[g, rows, :]
            m_new = jnp.maximum(m_old, jnp.broadcast_to(jnp.max(s, axis=-1, keepdims=True), m_old.shape))
            p_sc[g, dst, rows, :] = jnp.concatenate(
                [jnp.exp2(s[:, c * LANES:(c + 1) * LANES] - m_new) for c in range(t // LANES)],
                axis=1).astype(BF16)
            al_sc[g, dst, rows, :] = jnp.exp2(m_old - m_new)
            m_sc[g, rows, :] = m_new

    def accumulate(g, slot, c):
        pv = jnp.dot(p_sc[g, slot], chunk(v_ref, g, V_PAD, c), preferred_element_type=F32)
        alpha = al_sc[g, slot]
        acc_sc[g] = jnp.concatenate([alpha, alpha], axis=1) * acc_sc[g] + pv

    def stage(k, src, with_next=True):
        for g in heads:
            if with_next:
                scores(g, jnp.minimum(k + 1, i - 1), 1 - src)
            accumulate(g, src, jnp.where(k == 0, i, k - 1))
            softmax(g, src, 1 - src, masked=False)

    for g in heads:
        s = lax.dot_general(qs[g], km_ref[:, g * QK_PAD:(g + 1) * QK_PAD], nt, preferred_element_type=F32)
        m0 = jnp.max(s, axis=-1, keepdims=True)
        acc_sc[g] = jnp.dot(jnp.exp2(s - m0).astype(BF16), vm_ref[:, g * V_PAD:(g + 1) * V_PAD],
                            preferred_element_type=F32)
        m_sc[g] = jnp.broadcast_to(m0, (t, LANES))
        scores(g, i, 1)
        scores(g, 0, 0)
        softmax(g, 1, 0, masked=True)

    def pair(j, _):
        stage(2 * j, 0)
        stage(2 * j + 1, 1)
        return 0

    lax.fori_loop(0, i // 2, pair, 0)
    odd = lax.rem(i, 2) == 1

    @pl.when(odd)
    def _():
        stage(i - 1, 0, with_next=False)
        for g in heads:
            accumulate(g, 1, i - 1)

    @pl.when(jnp.logical_not(odd))
    def _():
        for g in heads:
            accumulate(g, 0, jnp.maximum(i - 1, 0))

    for g in heads:
        acc = acc_sc[g]
        o_ref[:, g * V_HEAD:(g + 1) * V_HEAD] = (acc[:, :V_HEAD] / acc[:, V_HEAD:]).astype(o_ref.dtype)


def _attention(q, k, v, k_meta, v_meta, t):
    n = q.shape[0]
    g = ATTN_HEADS
    return pl.pallas_call(
        functools.partial(_attn_kernel, t=t),
        grid=(N_HEADS // g, n // t),
        in_specs=[
            pl.BlockSpec((t, g * QK_PAD), lambda h, i: (i, h)),
            pl.BlockSpec((n, g * QK_PAD), lambda h, i: (0, h)),
            pl.BlockSpec((n, g * V_PAD), lambda h, i: (0, h)),
            pl.BlockSpec((N_META, g * QK_PAD), lambda h, i: (0, h)),
            pl.BlockSpec((N_META, g * V_PAD), lambda h, i: (0, h)),
        ],
        out_specs=pl.BlockSpec((t, g * V_HEAD), lambda h, i: (i, h)),
        out_shape=jax.ShapeDtypeStruct((n, N_HEADS * V_HEAD), BF16),
        scratch_shapes=[
            pltpu.VMEM((g, t, V_PAD), F32), pltpu.VMEM((g, t, LANES), F32),
            pltpu.VMEM((g, 2, t, t), F32), pltpu.VMEM((g, 2, t, t), BF16),
            pltpu.VMEM((g, 2, t, LANES), F32),
        ],
        compiler_params=_params(("parallel", "arbitrary")),
        name="attention",
    )(q, k, v, k_meta, v_meta)


def _merge_kernel(o_ref, u_ref, halo_ref, um_ref, gm_ref, gp_ref, wo_ref, pw_ref, ps_ref, wpo_ref,
                  out_ref, ext_sc, *, tm):
    i = pl.program_id(0)
    ext_sc[0:N_META, :] = jnp.where(i == 0, um_ref[...], halo_ref[...])
    ext_sc[N_META:, :] = u_ref[...]
    y_pool = jnp.zeros((tm, D_MODEL), F32)
    for g, w in enumerate(POOL_WINDOWS):
        lo, hi = g * POOL_GROUP_DIM, (g + 1) * POOL_GROUP_DIM
        u = ext_sc[N_META:, lo:hi]
        tot = u
        for d in range(1, w):
            tot = tot + ext_sc[N_META - d:N_META - d + tm, lo:hi]
        pooled = (tot * (1.0 / w) - u).astype(BF16)
        mixed = jnp.dot(pooled, pw_ref[g], preferred_element_type=F32) * ps_ref[:, lo:hi]
        y_pool = y_pool + jnp.dot(mixed.astype(BF16), wpo_ref[lo:hi, :], preferred_element_type=F32)
    y_mla = jnp.dot(o_ref[...], wo_ref[...], preferred_element_type=F32)
    merged = jax.nn.sigmoid(gm_ref[...]) * y_mla + jax.nn.sigmoid(gp_ref[...]) * y_pool
    out_ref[...] = merged.astype(out_ref.dtype)


def _merge(o, proj, u_meta, w_o, pool_w, pool_scale, w_pool_out, tm):
    n = o.shape[0]
    hb = tm // N_META
    const = lambda i: (0, 0)
    return pl.pallas_call(
        functools.partial(_merge_kernel, tm=tm),
        grid=(n // tm,),
        in_specs=[
            pl.BlockSpec((tm, D_MODEL), lambda i: (i, 0)),
            pl.BlockSpec((tm, POOL_WIDTH), lambda i: (i, COL_POOL // POOL_WIDTH)),
            pl.BlockSpec((N_META, POOL_WIDTH),
                         lambda i: (jnp.maximum(i * hb - 1, 0), COL_POOL // POOL_WIDTH)),
            pl.BlockSpec((N_META, POOL_WIDTH), lambda i: (0, COL_POOL // POOL_WIDTH)),
            pl.BlockSpec((tm, D_MODEL), lambda i: (i, COL_GM // D_MODEL)),
            pl.BlockSpec((tm, D_MODEL), lambda i: (i, COL_GP // D_MODEL)),
            pl.BlockSpec((D_MODEL, D_MODEL), const, pipeline_mode=pl.Buffered(1)),
            pl.BlockSpec((len(POOL_WINDOWS), POOL_GROUP_DIM, POOL_GROUP_DIM), lambda i: (0, 0, 0)),
            pl.BlockSpec((1, POOL_WIDTH), const),
            pl.BlockSpec((POOL_WIDTH, D_MODEL), const, pipeline_mode=pl.Buffered(1)),
        ],
        out_specs=pl.BlockSpec((tm, D_MODEL), lambda i: (i, 0)),
        out_shape=jax.ShapeDtypeStruct((n, D_MODEL), BF16),
        scratch_shapes=[pltpu.VMEM((tm + N_META, POOL_WIDTH), F32)],
        compiler_params=_params(("arbitrary",)),
        name="merge",
    )(o, proj, proj, u_meta, proj, proj, w_o, pool_w, pool_scale, w_pool_out)


def _route_kernel(mg_ref, x_ref, wout_ref, nw_ref, wr_ref, br_ref, h_ref, bn_ref, ids_ref, wts_ref):
    h = x_ref[...] + jnp.dot(mg_ref[...], wout_ref[...], preferred_element_type=F32)
    h_ref[...] = h
    bn = _rms(h, nw_ref[...])
    bn_ref[...] = bn
    logits = jnp.dot(bn.astype(BF16), wr_ref[...], preferred_element_type=F32) + br_ref[...]
    lt = logits.T
    tm = lt.shape[1]
    sub = lax.broadcasted_iota(jnp.int32, (SUBLANES, tm), 0)
    sub_f = sub.astype(F32)

    def first_max(vals):
        vmax = jnp.max(vals, axis=0, keepdims=True)
        idx = jnp.min(jnp.where(vals == vmax, sub_f, float(SUBLANES)), axis=0, keepdims=True)
        return vmax, idx.astype(jnp.int32)

    glog = lt[:N_GROUPS, :]
    gmax, gidx = first_max(glog)
    p_g = 1.0 / jnp.sum(jnp.exp(glog - gmax), axis=0, keepdims=True)
    e_in = jnp.zeros((EXPERTS_PER_GROUP, tm), F32)
    for g in range(N_GROUPS):
        lo = N_GROUPS + g * EXPERTS_PER_GROUP
        e_in = jnp.where(gidx == g, lt[lo:lo + EXPERTS_PER_GROUP, :], e_in)
    v1, i1 = first_max(e_in)
    v2, i2 = first_max(jnp.where(sub == i1, -jnp.inf, e_in))
    e2 = jnp.exp(v2 - v1)
    den = 1.0 + e2
    w1 = p_g * (1.0 / den)
    w2 = p_g * (e2 / den)
    base = gidx * EXPERTS_PER_GROUP
    ids_ref[...] = jnp.where(sub == 0, base + i1, jnp.where(sub == 1, base + i2, 0))
    wts_ref[...] = jnp.where(sub == 0, w1, jnp.where(sub == 1, w2, 0.0))


def _route(merged, x, w_out, norm_w, w_router, b_router, tm):
    n = x.shape[0]
    const = lambda i: (0, 0)
    row = lambda width: pl.BlockSpec((tm, width), lambda i: (i, 0))
    return pl.pallas_call(
        _route_kernel,
        grid=(n // tm,),
        in_specs=[
            row(D_MODEL), row(D_MODEL),
            pl.BlockSpec((D_MODEL, D_MODEL), const, pipeline_mode=pl.Buffered(1)),
            pl.BlockSpec((1, D_MODEL), const),
            pl.BlockSpec((D_MODEL, LANES), const),
            pl.BlockSpec((1, LANES), const),
        ],
        out_specs=[row(D_MODEL), row(D_MODEL),
                   pl.BlockSpec((SUBLANES, tm), lambda i: (0, i)), pl.BlockSpec((SUBLANES, tm), lambda i: (0, i))],
        out_shape=[jax.ShapeDtypeStruct((n, D_MODEL), F32), jax.ShapeDtypeStruct((n, D_MODEL), F32),
                   jax.ShapeDtypeStruct((SUBLANES, n), jnp.int32), jax.ShapeDtypeStruct((SUBLANES, n), F32)],
        compiler_params=_params(("parallel",)),
        name="route",
    )(merged, x, w_out, norm_w, w_router, b_router)


def _for_rows(n, fn):
    shift = int(math.log2(SUBLANES))
    n_groups = lax.shift_right_logical(n, shift)

    def group(g, _):
        for u in range(SUBLANES):
            fn(g, u)
        return 0

    def single(r, _):
        fn(n_groups, r)
        return 0

    lax.fori_loop(0, n_groups, group, 0)
    lax.fori_loop(0, n - n_groups * SUBLANES, single, 0)


def _experts_kernel(be_ref, nv_ref, sb_ref, tok_ref, dst_ref, bn_hbm, wg_ref, wu_ref, wd_ref, y_hbm,
                    xbuf, ybuf, wg_sc, wu_sc, wd_sc, gsem, ssem):
    b = pl.program_id(0)
    nb = pl.num_programs(0)
    slot = lax.rem(b, 2)

    def gather(blk, slt):
        base = sb_ref[blk]

        def copy(g, u):
            tok = tok_ref[base + g * SUBLANES + u]
            return pltpu.make_async_copy(bn_hbm.at[pl.ds(tok, 1)], xbuf.at[slt, g, pl.ds(u, 1)],
                                         gsem.at[slt])
        return copy

    def scatter(blk, slt):
        base = sb_ref[blk]

        def copy(g, u):
            dst = dst_ref[base + g * SUBLANES + u]
            return pltpu.make_async_copy(ybuf.at[slt, g, pl.ds(u, 1)], y_hbm.at[pl.ds(dst, 1)],
                                         ssem.at[slt])
        return copy

    def gather_start(blk, slt):
        copy = gather(blk, slt)
        _for_rows(nv_ref[blk], lambda g, u: copy(g, u).start())

    def scatter_wait(blk, slt):
        copy = scatter(blk, slt)
        _for_rows(nv_ref[blk], lambda g, u: copy(g, u).wait())

    @pl.when(b == 0)
    def _():
        xbuf[...] = jnp.zeros(xbuf.shape, xbuf.dtype)
        gather_start(0, 0)

    @pl.when(b + 1 < nb)
    def _():
        gather_start(b + 1, 1 - slot)

    @pl.when(nv_ref[b] > 0)
    def _():
        wait_copy = gather(b, slot)
        _for_rows(nv_ref[b], lambda g, u: wait_copy(g, u).wait())
        x = xbuf[slot].reshape(ROUTE_BLOCK, D_MODEL)
        gate = jnp.dot(x, wg_ref[0], preferred_element_type=F32)
        up = jnp.dot(x, wu_ref[0], preferred_element_type=F32)
        hdn = gate * jax.nn.sigmoid(gate) * up
        y = jnp.dot(hdn, wd_ref[0], preferred_element_type=F32)
        ybuf[slot] = y.reshape(ROUTE_BLOCK // SUBLANES, SUBLANES, D_MODEL)
        start_copy = scatter(b, slot)
        _for_rows(nv_ref[b], lambda g, u: start_copy(g, u).start())

    @pl.when(b >= 1)
    def _():
        scatter_wait(b - 1, 1 - slot)

    @pl.when(b == nb - 1)
    def _():
        scatter_wait(b, slot)


def _experts(block_expert, block_rows, block_base, tok, dst, bn, w_gate, w_up, w_down):
    n = bn.shape[0]
    wmap = lambda b, be, nv, sb, tk, ds: (be[b], 0, 0)
    buf = pltpu.VMEM((2, ROUTE_BLOCK // SUBLANES, SUBLANES, D_MODEL), F32)
    grid_spec = pltpu.PrefetchScalarGridSpec(
        num_scalar_prefetch=5,
        grid=(block_expert.shape[0],),
        in_specs=[
            pl.BlockSpec(memory_space=pl.ANY),
            pl.BlockSpec((1, D_MODEL, EXPERT_FF), wmap),
            pl.BlockSpec((1, D_MODEL, EXPERT_FF), wmap),
            pl.BlockSpec((1, EXPERT_FF, D_MODEL), wmap),
        ],
        out_specs=pl.BlockSpec(memory_space=pl.ANY),
        scratch_shapes=[
            buf,
            buf,
            pltpu.VMEM((D_MODEL, EXPERT_FF), BF16),
            pltpu.VMEM((D_MODEL, EXPERT_FF), BF16),
            pltpu.VMEM((EXPERT_FF, D_MODEL), BF16),
            pltpu.SemaphoreType.DMA((2,)),
            pltpu.SemaphoreType.DMA((2,)),
        ],
    )
    return pl.pallas_call(
        _experts_kernel,
        grid_spec=grid_spec,
        out_shape=jax.ShapeDtypeStruct((TOP_K * n, D_MODEL), F32),
        compiler_params=_params(("arbitrary",)),
        name="experts",
    )(block_expert, block_rows, block_base, tok, dst, bn, w_gate, w_up, w_down)


def _route_blocks(flat_e):
    a = flat_e.shape[0]
    i32 = jnp.int32
    order = jnp.argsort(flat_e).astype(i32)
    counts = jnp.sum(flat_e[:, None] == jnp.arange(N_EXPERTS, dtype=i32)[None, :], axis=0, dtype=i32)
    padded = (counts + ROUTE_BLOCK - 1) // ROUTE_BLOCK * ROUTE_BLOCK
    pad_end = jnp.cumsum(padded).astype(i32)
    pad_start = pad_end - padded
    seg_start = jnp.cumsum(counts).astype(i32) - counts
    n_blocks = -(-(a + N_EXPERTS * (ROUTE_BLOCK - 1)) // ROUTE_BLOCK)
    blk = jnp.arange(n_blocks, dtype=i32)
    be = jnp.sum(pad_end[None, :] <= (blk * ROUTE_BLOCK)[:, None], axis=1, dtype=i32)
    be = jnp.minimum(be, N_EXPERTS - 1)
    off = blk * ROUTE_BLOCK - pad_start[be]
    rows = jnp.clip(counts[be] - off, 0, ROUTE_BLOCK).astype(i32)
    base = jnp.where(rows > 0, seg_start[be] + off, 0).astype(i32)
    n_used = pad_end[-1] // ROUTE_BLOCK
    be = jnp.where(blk < n_used, be, be[jnp.maximum(n_used - 1, 0)])
    tok = order // TOP_K
    dst = (order % TOP_K) * (a // TOP_K) + tok
    return be, rows, base, tok, dst


def _final_kernel(h_ref, y0_ref, y1_ref, wts_ref, nw_ref, o_ref):
    w = wts_ref[...]
    moe = y0_ref[...] * w[:, 0:1] + y1_ref[...] * w[:, 1:2]
    o_ref[...] = _rms(h_ref[...] + moe, nw_ref[...])


def _final(h, y, wts, norm_w, tm):
    n = h.shape[0]
    return pl.pallas_call(
        _final_kernel,
        grid=(n // tm,),
        in_specs=[
            pl.BlockSpec((tm, D_MODEL), lambda i: (i, 0)),
            pl.BlockSpec((tm, D_MODEL), lambda i: (i, 0)),
            pl.BlockSpec((tm, D_MODEL), lambda i: (i + n // tm, 0)),
            pl.BlockSpec((tm, TOP_K), lambda i: (i, 0)),
            pl.BlockSpec((1, D_MODEL), lambda i: (0, 0)),
        ],
        out_specs=pl.BlockSpec((tm, D_MODEL), lambda i: (i, 0)),
        out_shape=jax.ShapeDtypeStruct((n, D_MODEL), F32),
        compiler_params=_params(("parallel",)),
        name="final",
    )(h, y, y, wts, norm_w)


def _rope_tables(length):
    f32 = np.float32
    inv = (f32(1.0) / np.power(f32(ROPE_THETA), np.arange(0, QK_ROPE, 2, dtype=f32) / f32(QK_ROPE))).astype(f32)
    ang = (np.arange(length, dtype=f32)[:, None] * inv[None, :]).astype(f32)
    cos, sin = np.cos(ang).astype(f32), np.sin(ang).astype(f32)
    z32 = np.zeros_like(cos)
    z64 = np.zeros((length, LANES - QK_ROPE), f32)
    c = np.concatenate([cos, cos, z64], axis=1)
    s1 = np.concatenate([-sin, z32, z64], axis=1)
    s2 = np.concatenate([z32, sin, z64], axis=1)
    return c, s1, s2


def kernel(x, meta_tokens, norm_mix_w, w_in, q_norm_w, w_uq, kv_norm_w, w_ukv, w_o_mla, pool_w,
           pool_scale, w_pool_out, w_out, norm_ffn_w, w_router_group, b_router_group,
           w_router_expert, b_router_expert, w_exp_gate, w_exp_up, w_exp_down, final_norm_w):
    assert x.shape == (1, 8192, D_MODEL) and norm_mix_w.shape[0] == 1
    n = x.shape[1]
    xr = x[0]

    w_in_t = jnp.swapaxes(w_in[0], 0, 1).astype(BF16)
    w_q = jnp.pad(w_uq[0].reshape(Q_LORA, N_HEADS, QK_HEAD),
                  ((0, 0), (0, 0), (0, QK_PAD - QK_HEAD))).reshape(Q_LORA, N_HEADS * QK_PAD).astype(BF16)
    w_kv = w_ukv[0].reshape(KV_LORA, N_HEADS, QK_NOPE + V_HEAD)
    w_k = w_kv[:, :, :QK_NOPE].reshape(KV_LORA, N_HEADS * QK_NOPE).astype(BF16)
    w_v = w_kv[:, :, QK_NOPE:].reshape(KV_LORA, N_HEADS * V_HEAD).astype(BF16)
    w_router = jnp.pad(jnp.concatenate([w_router_group[0], w_router_expert[0]], axis=1),
                       ((0, 0), (0, LANES - N_GROUPS - N_EXPERTS))).astype(BF16)
    b_router = jnp.pad(jnp.concatenate([b_router_group[0], b_router_expert[0]]),
                       (0, LANES - N_GROUPS - N_EXPERTS))[None]
    c, s1, s2 = _rope_tables(N_META + n)

    proj, krope = _inproj(xr, norm_mix_w, w_in_t, tm=1024, tn=1024)
    proj_m, krope_m = _inproj(meta_tokens, norm_mix_w, w_in_t, tm=N_META, tn=1024)
    q = _qproj(proj, q_norm_w, w_q, c[N_META:], s1[N_META:], s2[N_META:], tm=512)
    k, v = _kvproj(proj, krope, kv_norm_w, w_k, w_v, c[N_META:], s1[N_META:], s2[N_META:], tm=512)
    k_m, v_m = _kvproj(proj_m, krope_m, kv_norm_w, w_k, w_v, c[:N_META], s1[:N_META], s2[:N_META],
                       tm=N_META)
    o = _attention(q, k, v, k_m, v_m, t=512)

    merged = _merge(o, proj, proj_m, w_o_mla[0].astype(BF16), pool_w[0].astype(BF16), pool_scale,
                    w_pool_out[0].astype(BF16), tm=512)
    h, bn, ids, wts = _route(merged, xr, w_out[0].astype(BF16), norm_ffn_w, w_router, b_router, tm=512)

    be, rows, base, tok, dst = _route_blocks(ids[:TOP_K].T.reshape(n * TOP_K))
    wts = wts[:TOP_K].T
    y = _experts(be, rows, base, tok, dst, bn, w_exp_gate[0], w_exp_up[0], w_exp_down[0])

    out = _final(h, y, wts, final_norm_w[None], tm=512)
    return out[None]
```

```python
import functools
import math

import jax
import jax.numpy as jnp
import numpy as np
from jax import lax
from jax.experimental import pallas as pl
from jax.experimental.pallas import tpu as pltpu

F32 = jnp.float32
BF16 = jnp.bfloat16

D_MODEL = 2048
N_META = 16
EPS = 1e-6
N_HEADS = 16
QK_NOPE = 128
QK_ROPE = 64
QK_HEAD = QK_NOPE + QK_ROPE
V_HEAD = 128
Q_LORA = 512
KV_LORA = 512
ROPE_THETA = 10000.0
POOL_WINDOWS = (2, 4, 8, 16)
POOL_WIDTH = 1024
POOL_GROUP_DIM = 256
N_GROUPS = 8
EXPERTS_PER_GROUP = 8
N_EXPERTS = 64
TOP_K = 2
EXPERT_FF = 512

LANES = 128
SUBLANES = 8
QK_PAD = 256
V_PAD = 256
MAIN_COLS = Q_LORA + KV_LORA + POOL_WIDTH + 2 * D_MODEL
COL_CQ, COL_CKV, COL_POOL, COL_GM, COL_GP = 0, 512, 1024, 2048, 4096
ATTN_HEADS = 2
ATTN_STRIP = 64
ROUTE_BLOCK = 256
NEG_BIG = -1e30

VMEM_LIMIT = 56 * 1024 * 1024


def _params(sem):
    return pltpu.CompilerParams(dimension_semantics=sem, vmem_limit_bytes=VMEM_LIMIT)


def _rms(x, w):
    return x * lax.rsqrt(jnp.mean(x * x, axis=-1, keepdims=True) + EPS) * w


def _rope128(v, c, s1, s2):
    return v * c + pltpu.roll(v, 96, 1) * s1 + pltpu.roll(v, 32, 1) * s2


def _inproj_kernel(x_ref, nw_ref, wa_ref, wr_ref, wb_ref, o_ref, kr_ref, a_sc):
    j = pl.program_id(1)
    nt = (((1,), (1,)), ((), ()))

    @pl.when(j == 0)
    def _():
        a = _rms(x_ref[...], nw_ref[...]).astype(BF16)
        a_sc[...] = a
        kr_ref[...] = lax.dot_general(a, wr_ref[...], nt, preferred_element_type=F32)
        o_ref[...] = lax.dot_general(a, wa_ref[...], nt, preferred_element_type=F32)

    @pl.when(j > 0)
    def _():
        o_ref[...] = lax.dot_general(a_sc[...], wb_ref[...], nt, preferred_element_type=F32)


def _inproj(x, norm_w, w_t, tm, tn):
    m, k = x.shape
    n_a = Q_LORA + KV_LORA
    assert tn == n_a
    n = w_t.shape[0] - QK_ROPE
    return pl.pallas_call(
        _inproj_kernel,
        grid=(m // tm, n // tn),
        in_specs=[
            pl.BlockSpec((tm, k), lambda i, j: (i, 0)),
            pl.BlockSpec((1, k), lambda i, j: (0, 0)),
            pl.BlockSpec((tn, k), lambda i, j: (0, 0)),
            pl.BlockSpec((LANES, k), lambda i, j: (n_a // LANES, 0)),
            pl.BlockSpec((pl.Element(tn), pl.Element(k)),
                         lambda i, j: (pl.multiple_of(n_a + QK_ROPE + jnp.maximum(j - 1, 0) * tn, QK_ROPE), 0)),
        ],
        out_specs=[
            pl.BlockSpec((tm, tn), lambda i, j: (i, j)),
            pl.BlockSpec((tm, LANES), lambda i, j: (i, 0)),
        ],
        out_shape=[jax.ShapeDtypeStruct((m, n), F32), jax.ShapeDtypeStruct((m, LANES), F32)],
        scratch_shapes=[pltpu.VMEM((tm, k), BF16)],
        compiler_params=_params(("parallel", "arbitrary")),
        name="inproj",
    )(x, norm_w, w_t, w_t, w_t)


def _qproj_kernel(cq_ref, nw_ref, w_ref, c_ref, s1_ref, s2_ref, q_ref, *, scale):
    a = _rms(cq_ref[...], nw_ref[...]).astype(BF16)
    c, s1, s2 = c_ref[...], s1_ref[...], s2_ref[...]
    for h in range(N_HEADS):
        qh = jnp.dot(a, w_ref[:, h * QK_PAD:(h + 1) * QK_PAD], preferred_element_type=F32) * scale
        q_ref[:, h * QK_PAD:h * QK_PAD + LANES] = qh[:, :LANES].astype(BF16)
        q_ref[:, h * QK_PAD + LANES:(h + 1) * QK_PAD] = _rope128(qh[:, LANES:], c, s1, s2).astype(BF16)


def _qproj(proj, norm_w, w_q, c, s1, s2, tm):
    m = proj.shape[0]
    n = N_HEADS * QK_PAD
    tab = pl.BlockSpec((tm, LANES), lambda i: (i, 0))
    return pl.pallas_call(
        functools.partial(_qproj_kernel, scale=QK_HEAD ** -0.5 * math.log2(math.e)),
        grid=(m // tm,),
        in_specs=[
            pl.BlockSpec((tm, Q_LORA), lambda i: (i, COL_CQ // Q_LORA)),
            pl.BlockSpec((1, Q_LORA), lambda i: (0, 0)),
            pl.BlockSpec((Q_LORA, n), lambda i: (0, 0)),
            tab, tab, tab,
        ],
        out_specs=pl.BlockSpec((tm, n), lambda i: (i, 0)),
        out_shape=jax.ShapeDtypeStruct((m, n), BF16),
        compiler_params=_params(("parallel",)),
        name="qproj",
    )(proj, norm_w, w_q, c, s1, s2)


def _kvproj_kernel(ckv_ref, kr_ref, nw_ref, wk_ref, wv_ref, c_ref, s1_ref, s2_ref, k_ref, v_ref):
    a = _rms(ckv_ref[...], nw_ref[...]).astype(BF16)
    kpe = _rope128(kr_ref[...], c_ref[...], s1_ref[...], s2_ref[...]).astype(BF16)
    ones = jnp.ones((a.shape[0], V_PAD - V_HEAD), BF16)
    for h2 in range(N_HEADS // 2):
        kn = jnp.dot(a, wk_ref[:, h2 * 256:(h2 + 1) * 256], preferred_element_type=F32).astype(BF16)
        vv = jnp.dot(a, wv_ref[:, h2 * 256:(h2 + 1) * 256], preferred_element_type=F32).astype(BF16)
        for d in range(2):
            h = 2 * h2 + d
            k_ref[:, h * QK_PAD:h * QK_PAD + LANES] = kn[:, d * LANES:(d + 1) * LANES]
            k_ref[:, h * QK_PAD + LANES:(h + 1) * QK_PAD] = kpe
            v_ref[:, h * V_PAD:h * V_PAD + V_HEAD] = vv[:, d * LANES:(d + 1) * LANES]
            v_ref[:, h * V_PAD + V_HEAD:(h + 1) * V_PAD] = ones


def _kvproj(proj, krope, norm_w, w_k, w_v, c, s1, s2, tm):
    m = proj.shape[0]
    tab = pl.BlockSpec((tm, LANES), lambda i: (i, 0))
    return pl.pallas_call(
        _kvproj_kernel,
        grid=(m // tm,),
        in_specs=[
            pl.BlockSpec((tm, KV_LORA), lambda i: (i, COL_CKV // KV_LORA)),
            tab,
            pl.BlockSpec((1, KV_LORA), lambda i: (0, 0)),
            pl.BlockSpec((KV_LORA, N_HEADS * QK_NOPE), lambda i: (0, 0)),
            pl.BlockSpec((KV_LORA, N_HEADS * V_HEAD), lambda i: (0, 0)),
            tab, tab, tab,
        ],
        out_specs=[
            pl.BlockSpec((tm, N_HEADS * QK_PAD), lambda i: (i, 0)),
            pl.BlockSpec((tm, N_HEADS * V_PAD), lambda i: (i, 0)),
        ],
        out_shape=[jax.ShapeDtypeStruct((m, N_HEADS * QK_PAD), BF16),
                   jax.ShapeDtypeStruct((m, N_HEADS * V_PAD), BF16)],
        compiler_params=_params(("parallel",)),
        name="kvproj",
    )(proj, krope, norm_w, w_k, w_v, c, s1, s2)


def _attn_kernel(q_ref, qn_ref, k_ref, v_ref, km_ref, vm_ref, o_ref,
                 acc_sc, acc0_sc, m_sc, s_sc, p_sc, al_sc, *, t):
    i = pl.program_id(1)
    heads = range(ATTN_HEADS)
    nt = (((1,), (1,)), ((), ()))
    qs = [q_ref[:, g * QK_PAD:(g + 1) * QK_PAD] for g in heads]

    def chunk(ref, g, width, c):
        return ref[pl.ds(pl.multiple_of(c * t, t), t), g * width:(g + 1) * width]

    def scores(g, c, slot, q=None):
        q = qs[g] if q is None else q
        s_sc[g, slot] = lax.dot_general(q, chunk(k_ref, g, QK_PAD, c), nt, preferred_element_type=F32)

    def first_scores(i_tile, q_of):
        for g in heads:
            scores(g, i_tile, 1, q_of(g))
            scores(g, 0, 0, q_of(g))

    def softmax(g, src, dst, masked):
        for r in range(t // ATTN_STRIP):
            rows = slice(r * ATTN_STRIP, (r + 1) * ATTN_STRIP)
            s = s_sc[g, src, rows, :]
            if masked:
                row = r * ATTN_STRIP + lax.broadcasted_iota(jnp.int32, s.shape, 0)
                s = jnp.where(lax.broadcasted_iota(jnp.int32, s.shape, 1) <= row, s, NEG_BIG)
            m_old = m_sc[g, rows, :]
            m_new = jnp.maximum(m_old, jnp.broadcast_to(jnp.max(s, axis=-1, keepdims=True), m_old.shape))
            p_sc[g, dst, rows, :] = jnp.concatenate(
                [jnp.exp2(s[:, c * LANES:(c + 1) * LANES] - m_new) for c in range(t // LANES)],
                axis=1).astype(BF16)
            al_sc[g, dst, rows, :] = jnp.exp2(m_old - m_new)
            m_sc[g, rows, :] = m_new

    def accumulate(g, slot, c):
        pv = jnp.dot(p_sc[g, slot], chunk(v_ref, g, V_PAD, c), preferred_element_type=F32)
        alpha = al_sc[g, slot]
        acc_sc[g] = jnp.concatenate([alpha, alpha], axis=1) * acc_sc[g] + pv

    def stage(k, src, with_next=True):
        for g in heads:
            if with_next:
                scores(g, jnp.minimum(k + 1, i - 1), 1 - src)
            accumulate(g, src, jnp.where(k == 0, i, k - 1))
            softmax(g, src, 1 - src, masked=False)

    def tile_start(i_tile, q_of):
        first_scores(i_tile, q_of)
        for g in heads:
            s = lax.dot_general(q_of(g), km_ref[:, g * QK_PAD:(g + 1) * QK_PAD], nt, preferred_element_type=F32)
            m0 = jnp.max(s, axis=-1, keepdims=True)
            acc0_sc[g] = jnp.dot(jnp.exp2(s - m0).astype(BF16), vm_ref[:, g * V_PAD:(g + 1) * V_PAD],
                                 preferred_element_type=F32)
            m_sc[g] = jnp.broadcast_to(m0, (t, LANES))
            softmax(g, 1, 0, masked=True)

    def tile_end(last_slot, last_chunk):
        for g in heads:
            accumulate(g, last_slot, last_chunk)
            acc = acc_sc[g]
            o_ref[:, g * V_HEAD:(g + 1) * V_HEAD] = (acc[:, :V_HEAD] / acc[:, V_HEAD:]).astype(o_ref.dtype)
        tile_start(jnp.minimum(i + 1, pl.num_programs(1) - 1),
                   lambda g: qn_ref[:, g * QK_PAD:(g + 1) * QK_PAD])

    @pl.when(i == 0)
    def _():
        tile_start(i, lambda g: qs[g])

    for g in heads:
        acc_sc[g] = acc0_sc[g]

    def quad(j, _):
        for u in range(4):
            stage(4 * j + u, u % 2)
        return 0

    n_quads = lax.shift_right_logical(i, 2)
    lax.fori_loop(0, n_quads, quad, 0)

    @pl.when((i & 2) != 0)
    def _():
        stage(4 * n_quads, 0)
        stage(4 * n_quads + 1, 1)

    odd = (i & 1) == 1

    @pl.when(odd)
    def _():
        stage(i - 1, 0, with_next=False)
        tile_end(1, i - 1)

    @pl.when(jnp.logical_not(odd))
    def _():
        tile_end(0, jnp.maximum(i - 1, 0))


def _attention(q, k, v, k_meta, v_meta, t):
    n = q.shape[0]
    g = ATTN_HEADS
    return pl.pallas_call(
        functools.partial(_attn_kernel, t=t),
        grid=(N_HEADS // g, n // t),
        in_specs=[
            pl.BlockSpec((t, g * QK_PAD), lambda h, i: (i, h)),
            pl.BlockSpec((t, g * QK_PAD), lambda h, i: (jnp.minimum(i + 1, n // t - 1), h)),
            pl.BlockSpec((n, g * QK_PAD), lambda h, i: (0, h)),
            pl.BlockSpec((n, g * V_PAD), lambda h, i: (0, h)),
            pl.BlockSpec((N_META, g * QK_PAD), lambda h, i: (0, h)),
            pl.BlockSpec((N_META, g * V_PAD), lambda h, i: (0, h)),
        ],
        out_specs=pl.BlockSpec((t, g * V_HEAD), lambda h, i: (i, h)),
        out_shape=jax.ShapeDtypeStruct((n, N_HEADS * V_HEAD), BF16),
        scratch_shapes=[
            pltpu.VMEM((g, t, V_PAD), F32), pltpu.VMEM((g, t, V_PAD), F32), pltpu.VMEM((g, t, LANES), F32),
            pltpu.VMEM((g, 2, t, t), F32), pltpu.VMEM((g, 2, t, t), BF16),
            pltpu.VMEM((g, 2, t, LANES), F32),
        ],
        compiler_params=_params(("parallel", "arbitrary")),
        name="attention",
    )(q, q, k, v, k_meta, v_meta)


def _merge_kernel(o_ref, u_ref, halo_ref, um_ref, gm_ref, gp_ref, wo_ref, pw_ref, ps_ref, wpo_ref,
                  out_ref, ext_sc, *, tm):
    i = pl.program_id(0)
    ext_sc[0:N_META, :] = jnp.where(i == 0, um_ref[...], halo_ref[...])
    ext_sc[N_META:, :] = u_ref[...]
    y_pool = jnp.zeros((tm, D_MODEL), F32)
    for g, w in enumerate(POOL_WINDOWS):
        lo, hi = g * POOL_GROUP_DIM, (g + 1) * POOL_GROUP_DIM
        u = ext_sc[N_META:, lo:hi]
        tot = u
        for d in range(1, w):
            tot = tot + ext_sc[N_META - d:N_META - d + tm, lo:hi]
        pooled = (tot * (1.0 / w) - u).astype(BF16)
        mixed = jnp.dot(pooled, pw_ref[g], preferred_element_type=F32) * ps_ref[:, lo:hi]
        y_pool = y_pool + jnp.dot(mixed.astype(BF16), wpo_ref[lo:hi, :], preferred_element_type=F32)
    y_mla = jnp.dot(o_ref[...], wo_ref[...], preferred_element_type=F32)
    merged = jax.nn.sigmoid(gm_ref[...]) * y_mla + jax.nn.sigmoid(gp_ref[...]) * y_pool
    out_ref[...] = merged.astype(out_ref.dtype)


def _merge(o, proj, u_meta, w_o, pool_w, pool_scale, w_pool_out, tm):
    n = o.shape[0]
    hb = tm // N_META
    const = lambda i: (0, 0)
    return pl.pallas_call(
        functools.partial(_merge_kernel, tm=tm),
        grid=(n // tm,),
        in_specs=[
            pl.BlockSpec((tm, D_MODEL), lambda i: (i, 0)),
            pl.BlockSpec((tm, POOL_WIDTH), lambda i: (i, COL_POOL // POOL_WIDTH)),
            pl.BlockSpec((N_META, POOL_WIDTH),
                         lambda i: (jnp.maximum(i * hb - 1, 0), COL_POOL // POOL_WIDTH)),
            pl.BlockSpec((N_META, POOL_WIDTH), lambda i: (0, COL_POOL // POOL_WIDTH)),
            pl.BlockSpec((tm, D_MODEL), lambda i: (i, COL_GM // D_MODEL)),
            pl.BlockSpec((tm, D_MODEL), lambda i: (i, COL_GP // D_MODEL)),
            pl.BlockSpec((D_MODEL, D_MODEL), const, pipeline_mode=pl.Buffered(1)),
            pl.BlockSpec((len(POOL_WINDOWS), POOL_GROUP_DIM, POOL_GROUP_DIM), lambda i: (0, 0, 0)),
            pl.BlockSpec((1, POOL_WIDTH), const),
            pl.BlockSpec((POOL_WIDTH, D_MODEL), const, pipeline_mode=pl.Buffered(1)),
        ],
        out_specs=pl.BlockSpec((tm, D_MODEL), lambda i: (i, 0)),
        out_shape=jax.ShapeDtypeStruct((n, D_MODEL), BF16),
        scratch_shapes=[pltpu.VMEM((tm + N_META, POOL_WIDTH), F32)],
        compiler_params=_params(("arbitrary",)),
        name="merge",
    )(o, proj, proj, u_meta, proj, proj, w_o, pool_w, pool_scale, w_pool_out)


def _route_kernel(mg_ref, x_ref, wout_ref, nw_ref, wr_ref, br_ref, h_ref, bn_ref, ids_ref, wts_ref):
    h = x_ref[...] + jnp.dot(mg_ref[...], wout_ref[...], preferred_element_type=F32)
    h_ref[...] = h
    bn = _rms(h, nw_ref[...])
    bn_ref[...] = bn
    logits = jnp.dot(bn.astype(BF16), wr_ref[...], preferred_element_type=F32) + br_ref[...]
    lt = logits.T
    tm = lt.shape[1]
    sub = lax.broadcasted_iota(jnp.int32, (SUBLANES, tm), 0)
    sub_f = sub.astype(F32)

    def first_max(vals):
        vmax = jnp.max(vals, axis=0, keepdims=True)
        idx = jnp.min(jnp.where(vals == vmax, sub_f, float(SUBLANES)), axis=0, keepdims=True)
        return vmax, idx.astype(jnp.int32)

    glog = lt[:N_GROUPS, :]
    gmax, gidx = first_max(glog)
    p_g = 1.0 / jnp.sum(jnp.exp(glog - gmax), axis=0, keepdims=True)
    e_in = jnp.zeros((EXPERTS_PER_GROUP, tm), F32)
    for g in range(N_GROUPS):
        lo = N_GROUPS + g * EXPERTS_PER_GROUP
        e_in = jnp.where(gidx == g, lt[lo:lo + EXPERTS_PER_GROUP, :], e_in)
    v1, i1 = first_max(e_in)
    v2, i2 = first_max(jnp.where(sub == i1, -jnp.inf, e_in))
    e2 = jnp.exp(v2 - v1)
    den = 1.0 + e2
    w1 = p_g * (1.0 / den)
    w2 = p_g * (e2 / den)
    base = gidx * EXPERTS_PER_GROUP
    ids_ref[...] = jnp.where(sub == 0, base + i1, jnp.where(sub == 1, base + i2, 0))
    wts_ref[...] = jnp.where(sub == 0, w1, jnp.where(sub == 1, w2, 0.0))


def _route(merged, x, w_out, norm_w, w_router, b_router, tm):
    n = x.shape[0]
    const = lambda i: (0, 0)
    row = lambda width: pl.BlockSpec((tm, width), lambda i: (i, 0))
    return pl.pallas_call(
        _route_kernel,
        grid=(n // tm,),
        in_specs=[
            row(D_MODEL), row(D_MODEL),
            pl.BlockSpec((D_MODEL, D_MODEL), const, pipeline_mode=pl.Buffered(1)),
            pl.BlockSpec((1, D_MODEL), const),
            pl.BlockSpec((D_MODEL, LANES), const),
            pl.BlockSpec((1, LANES), const),
        ],
        out_specs=[row(D_MODEL), row(D_MODEL),
                   pl.BlockSpec((SUBLANES, tm), lambda i: (0, i)), pl.BlockSpec((SUBLANES, tm), lambda i: (0, i))],
        out_shape=[jax.ShapeDtypeStruct((n, D_MODEL), F32), jax.ShapeDtypeStruct((n, D_MODEL), F32),
                   jax.ShapeDtypeStruct((SUBLANES, n), jnp.int32), jax.ShapeDtypeStruct((SUBLANES, n), F32)],
        compiler_params=_params(("parallel",)),
        name="route",
    )(merged, x, w_out, norm_w, w_router, b_router)


def _for_rows(n, fn):
    shift = int(math.log2(SUBLANES))
    n_groups = lax.shift_right_logical(n, shift)

    def group(g, _):
        for u in range(SUBLANES):
            fn(g, u)
        return 0

    def single(r, _):
        fn(n_groups, r)
        return 0

    lax.fori_loop(0, n_groups, group, 0)
    lax.fori_loop(0, n - n_groups * SUBLANES, single, 0)


def _experts_kernel(be_ref, nv_ref, sb_ref, tok_ref, dst_ref, bn_hbm, wg_ref, wu_ref, wd_ref, y_hbm,
                    xbuf, ybuf, wg_sc, wu_sc, wd_sc, gsem, ssem):
    b = pl.program_id(0)
    nb = pl.num_programs(0)
    slot = lax.rem(b, 2)

    def gather(blk, slt):
        base = sb_ref[blk]

        def copy(g, u):
            tok = tok_ref[base + g * SUBLANES + u]
            return pltpu.make_async_copy(bn_hbm.at[pl.ds(tok, 1)], xbuf.at[slt, g, pl.ds(u, 1)],
                                         gsem.at[slt])
        return copy

    def scatter(blk, slt):
        base = sb_ref[blk]

        def copy(g, u):
            dst = dst_ref[base + g * SUBLANES + u]
            return pltpu.make_async_copy(ybuf.at[slt, g, pl.ds(u, 1)], y_hbm.at[pl.ds(dst, 1)],
                                         ssem.at[slt])
        return copy

    def gather_start(blk, slt):
        copy = gather(blk, slt)
        _for_rows(nv_ref[blk], lambda g, u: copy(g, u).start())

    def scatter_wait(blk, slt):
        copy = scatter(blk, slt)
        _for_rows(nv_ref[blk], lambda g, u: copy(g, u).wait())

    @pl.when(b == 0)
    def _():
        xbuf[...] = jnp.zeros(xbuf.shape, xbuf.dtype)
        gather_start(0, 0)

    @pl.when(b + 1 < nb)
    def _():
        gather_start(b + 1, 1 - slot)

    @pl.when(nv_ref[b] > 0)
    def _():
        wait_copy = gather(b, slot)
        _for_rows(nv_ref[b], lambda g, u: wait_copy(g, u).wait())
        x = xbuf[slot].reshape(ROUTE_BLOCK, D_MODEL)
        gate = jnp.dot(x, wg_ref[0], preferred_element_type=F32)
        up = jnp.dot(x, wu_ref[0], preferred_element_type=F32)
        hdn = gate * jax.nn.sigmoid(gate) * up
        y = jnp.dot(hdn, wd_ref[0], preferred_element_type=F32)
        ybuf[slot] = y.reshape(ROUTE_BLOCK // SUBLANES, SUBLANES, D_MODEL)
        start_copy = scatter(b, slot)
        _for_rows(nv_ref[b], lambda g, u: start_copy(g, u).start())

    @pl.when(b >= 1)
    def _():
        scatter_wait(b - 1, 1 - slot)

    @pl.when(b == nb - 1)
    def _():
        scatter_wait(b, slot)


def _experts(block_expert, block_rows, block_base, tok, dst, bn, w_gate, w_up, w_down):
    n = bn.shape[0]
    wmap = lambda b, be, nv, sb, tk, ds: (be[b], 0, 0)
    buf = pltpu.VMEM((2, ROUTE_BLOCK // SUBLANES, SUBLANES, D_MODEL), F32)
    grid_spec = pltpu.PrefetchScalarGridSpec(
        num_scalar_prefetch=5,
        grid=(block_expert.shape[0],),
        in_specs=[
            pl.BlockSpec(memory_space=pl.ANY),
            pl.BlockSpec((1, D_MODEL, EXPERT_FF), wmap),
            pl.BlockSpec((1, D_MODEL, EXPERT_FF), wmap),
            pl.BlockSpec((1, EXPERT_FF, D_MODEL), wmap),
        ],
        out_specs=pl.BlockSpec(memory_space=pl.ANY),
        scratch_shapes=[
            buf,
            buf,
            pltpu.VMEM((D_MODEL, EXPERT_FF), BF16),
            pltpu.VMEM((D_MODEL, EXPERT_FF), BF16),
            pltpu.VMEM((EXPERT_FF, D_MODEL), BF16),
            pltpu.SemaphoreType.DMA((2,)),
            pltpu.SemaphoreType.DMA((2,)),
        ],
    )
    return pl.pallas_call(
        _experts_kernel,
        grid_spec=grid_spec,
        out_shape=jax.ShapeDtypeStruct((TOP_K * n, D_MODEL), F32),
        compiler_params=_params(("arbitrary",)),
        name="experts",
    )(block_expert, block_rows, block_base, tok, dst, bn, w_gate, w_up, w_down)


def _route_blocks(flat_e):
    a = flat_e.shape[0]
    i32 = jnp.int32
    order = jnp.argsort(flat_e).astype(i32)
    counts = jnp.sum(flat_e[:, None] == jnp.arange(N_EXPERTS, dtype=i32)[None, :], axis=0, dtype=i32)
    padded = (counts + ROUTE_BLOCK - 1) // ROUTE_BLOCK * ROUTE_BLOCK
    pad_end = jnp.cumsum(padded).astype(i32)
    pad_start = pad_end - padded
    seg_start = jnp.cumsum(counts).astype(i32) - counts
    n_blocks = -(-(a + N_EXPERTS * (ROUTE_BLOCK - 1)) // ROUTE_BLOCK)
    blk = jnp.arange(n_blocks, dtype=i32)
    be = jnp.sum(pad_end[None, :] <= (blk * ROUTE_BLOCK)[:, None], axis=1, dtype=i32)
    be = jnp.minimum(be, N_EXPERTS - 1)
    off = blk * ROUTE_BLOCK - pad_start[be]
    rows = jnp.clip(counts[be] - off, 0, ROUTE_BLOCK).astype(i32)
    base = jnp.where(rows > 0, seg_start[be] + off, 0).astype(i32)
    n_used = pad_end[-1] // ROUTE_BLOCK
    be = jnp.where(blk < n_used, be, be[jnp.maximum(n_used - 1, 0)])
    tok = order // TOP_K
    dst = (order % TOP_K) * (a // TOP_K) + tok
    return be, rows, base, tok, dst


def _final_kernel(h_ref, y0_ref, y1_ref, wts_ref, nw_ref, o_ref):
    w = wts_ref[...]
    moe = y0_ref[...] * w[:, 0:1] + y1_ref[...] * w[:, 1:2]
    o_ref[...] = _rms(h_ref[...] + moe, nw_ref[...])


def _final(h, y, wts, norm_w, tm):
    n = h.shape[0]
    return pl.pallas_call(
        _final_kernel,
        grid=(n // tm,),
        in_specs=[
            pl.BlockSpec((tm, D_MODEL), lambda i: (i, 0)),
            pl.BlockSpec((tm, D_MODEL), lambda i: (i, 0)),
            pl.BlockSpec((tm, D_MODEL), lambda i: (i + n // tm, 0)),
            pl.BlockSpec((tm, TOP_K), lambda i: (i, 0)),
            pl.BlockSpec((1, D_MODEL), lambda i: (0, 0)),
        ],
        out_specs=pl.BlockSpec((tm, D_MODEL), lambda i: (i, 0)),
        out_shape=jax.ShapeDtypeStruct((n, D_MODEL), F32),
        compiler_params=_params(("parallel",)),
        name="final",
    )(h, y, y, wts, norm_w)


def _rope_tables(length):
    f32 = np.float32
    inv = (f32(1.0) / np.power(f32(ROPE_THETA), np.arange(0, QK_ROPE, 2, dtype=f32) / f32(QK_ROPE))).astype(f32)
    ang = (np.arange(length, dtype=f32)[:, None] * inv[None, :]).astype(f32)
    cos, sin = np.cos(ang).astype(f32), np.sin(ang).astype(f32)
    z32 = np.zeros_like(cos)
    z64 = np.zeros((length, LANES - QK_ROPE), f32)
    c = np.concatenate([cos, cos, z64], axis=1)
    s1 = np.concatenate([-sin, z32, z64], axis=1)
    s2 = np.concatenate([z32, sin, z64], axis=1)
    return c, s1, s2


def kernel(x, meta_tokens, norm_mix_w, w_in, q_norm_w, w_uq, kv_norm_w, w_ukv, w_o_mla, pool_w,
           pool_scale, w_pool_out, w_out, norm_ffn_w, w_router_group, b_router_group,
           w_router_expert, b_router_expert, w_exp_gate, w_exp_up, w_exp_down, final_norm_w):
    assert x.shape == (1, 8192, D_MODEL) and norm_mix_w.shape[0] == 1
    n = x.shape[1]
    xr = x[0]

    w_in_t = jnp.swapaxes(w_in[0], 0, 1).astype(BF16)
    w_q = jnp.pad(w_uq[0].reshape(Q_LORA, N_HEADS, QK_HEAD),
                  ((0, 0), (0, 0), (0, QK_PAD - QK_HEAD))).reshape(Q_LORA, N_HEADS * QK_PAD).astype(BF16)
    w_kv = w_ukv[0].reshape(KV_LORA, N_HEADS, QK_NOPE + V_HEAD)
    w_k = w_kv[:, :, :QK_NOPE].reshape(KV_LORA, N_HEADS * QK_NOPE).astype(BF16)
    w_v = w_kv[:, :, QK_NOPE:].reshape(KV_LORA, N_HEADS * V_HEAD).astype(BF16)
    w_router = jnp.pad(jnp.concatenate([w_router_group[0], w_router_expert[0]], axis=1),
                       ((0, 0), (0, LANES - N_GROUPS - N_EXPERTS))).astype(BF16)
    b_router = jnp.pad(jnp.concatenate([b_router_group[0], b_router_expert[0]]),
                       (0, LANES - N_GROUPS - N_EXPERTS))[None]
    c, s1, s2 = _rope_tables(N_META + n)

    proj, krope = _inproj(xr, norm_mix_w, w_in_t, tm=1024, tn=1024)
    proj_m, krope_m = _inproj(meta_tokens, norm_mix_w, w_in_t, tm=N_META, tn=1024)
    q = _qproj(proj, q_norm_w, w_q, c[N_META:], s1[N_META:], s2[N_META:], tm=512)
    k, v = _kvproj(proj, krope, kv_norm_w, w_k, w_v, c[N_META:], s1[N_META:], s2[N_META:], tm=512)
    k_m, v_m = _kvproj(proj_m, krope_m, kv_norm_w, w_k, w_v, c[:N_META], s1[:N_META], s2[:N_META],
                       tm=N_META)
    o = _attention(q, k, v, k_m, v_m, t=512)

    merged = _merge(o, proj, proj_m, w_o_mla[0].astype(BF16), pool_w[0].astype(BF16), pool_scale,
                    w_pool_out[0].astype(BF16), tm=512)
    h, bn, ids, wts = _route(merged, xr, w_out[0].astype(BF16), norm_ffn_w, w_router, b_router, tm=512)

    be, rows, base, tok, dst = _route_blocks(ids[:TOP_K].T.reshape(n * TOP_K))
    wts = wts[:TOP_K].T
    y = _experts(be, rows, base, tok, dst, bn, w_exp_gate[0], w_exp_up[0], w_exp_down[0])

    out = _final(h, y, wts, final_norm_w[None], tm=512)
    return out[None]
```

```python
import functools
import math

import jax
import jax.numpy as jnp
import numpy as np
from jax import lax
from jax.experimental import pallas as pl
from jax.experimental.pallas import tpu as pltpu

F32 = jnp.float32
BF16 = jnp.bfloat16

D_MODEL = 2048
N_META = 16
EPS = 1e-6
N_HEADS = 16
QK_NOPE = 128
QK_ROPE = 64
QK_HEAD = QK_NOPE + QK_ROPE
V_HEAD = 128
Q_LORA = 512
KV_LORA = 512
ROPE_THETA = 10000.0
POOL_WINDOWS = (2, 4, 8, 16)
POOL_WIDTH = 1024
POOL_GROUP_DIM = 256
N_GROUPS = 8
EXPERTS_PER_GROUP = 8
N_EXPERTS = 64
TOP_K = 2
EXPERT_FF = 512

LANES = 128
SUBLANES = 8
QK_PAD = 256
V_PAD = 256
MAIN_COLS = Q_LORA + KV_LORA + POOL_WIDTH + 2 * D_MODEL
COL_CQ, COL_CKV, COL_POOL, COL_GM, COL_GP = 0, 512, 1024, 2048, 4096
ATTN_HEADS = 2
ATTN_UNROLL = 8
ATTN_STRIP = 64
ROUTE_BLOCK = 256
NEG_BIG = -1e30

VMEM_LIMIT = 56 * 1024 * 1024


def _params(sem):
    return pltpu.CompilerParams(dimension_semantics=sem, vmem_limit_bytes=VMEM_LIMIT)


def _rms(x, w):
    return x * lax.rsqrt(jnp.mean(x * x, axis=-1, keepdims=True) + EPS) * w


def _rope128(v, c, s1, s2):
    return v * c + pltpu.roll(v, 96, 1) * s1 + pltpu.roll(v, 32, 1) * s2


def _inproj_kernel(x_ref, nw_ref, wa_ref, wr_ref, wb_ref, o_ref, kr_ref, a_sc):
    j = pl.program_id(1)
    nt = (((1,), (1,)), ((), ()))

    @pl.when(j == 0)
    def _():
        a = _rms(x_ref[...], nw_ref[...]).astype(BF16)
        a_sc[...] = a
        kr_ref[...] = lax.dot_general(a, wr_ref[...], nt, preferred_element_type=F32)
        o_ref[...] = lax.dot_general(a, wa_ref[...], nt, preferred_element_type=F32)

    @pl.when(j > 0)
    def _():
        o_ref[...] = lax.dot_general(a_sc[...], wb_ref[...], nt, preferred_element_type=F32)


def _inproj(x, norm_w, w_t, tm, tn):
    m, k = x.shape
    n_a = Q_LORA + KV_LORA
    assert tn == n_a
    n = w_t.shape[0] - QK_ROPE
    return pl.pallas_call(
        _inproj_kernel,
        grid=(m // tm, n // tn),
        in_specs=[
            pl.BlockSpec((tm, k), lambda i, j: (i, 0)),
            pl.BlockSpec((1, k), lambda i, j: (0, 0)),
            pl.BlockSpec((tn, k), lambda i, j: (0, 0)),
            pl.BlockSpec((LANES, k), lambda i, j: (n_a // LANES, 0)),
            pl.BlockSpec((pl.Element(tn), pl.Element(k)),
                         lambda i, j: (pl.multiple_of(n_a + QK_ROPE + jnp.maximum(j - 1, 0) * tn, QK_ROPE), 0)),
        ],
        out_specs=[
            pl.BlockSpec((tm, tn), lambda i, j: (i, j)),
            pl.BlockSpec((tm, LANES), lambda i, j: (i, 0)),
        ],
        out_shape=[jax.ShapeDtypeStruct((m, n), F32), jax.ShapeDtypeStruct((m, LANES), F32)],
        scratch_shapes=[pltpu.VMEM((tm, k), BF16)],
        compiler_params=_params(("parallel", "arbitrary")),
        name="inproj",
    )(x, norm_w, w_t, w_t, w_t)


def _qproj_kernel(cq_ref, nw_ref, w_ref, c_ref, s1_ref, s2_ref, q_ref, *, scale):
    a = _rms(cq_ref[...], nw_ref[...]).astype(BF16)
    c, s1, s2 = c_ref[...], s1_ref[...], s2_ref[...]
    for h in range(N_HEADS):
        qh = jnp.dot(a, w_ref[:, h * QK_PAD:(h + 1) * QK_PAD], preferred_element_type=F32) * scale
        q_ref[:, h * QK_PAD:h * QK_PAD + LANES] = qh[:, :LANES].astype(BF16)
        q_ref[:, h * QK_PAD + LANES:(h + 1) * QK_PAD] = _rope128(qh[:, LANES:], c, s1, s2).astype(BF16)


def _qproj(proj, norm_w, w_q, c, s1, s2, tm):
    m = proj.shape[0]
    n = N_HEADS * QK_PAD
    tab = pl.BlockSpec((tm, LANES), lambda i: (i, 0))
    return pl.pallas_call(
        functools.partial(_qproj_kernel, scale=QK_HEAD ** -0.5 * math.log2(math.e)),
        grid=(m // tm,),
        in_specs=[
            pl.BlockSpec((tm, Q_LORA), lambda i: (i, COL_CQ // Q_LORA)),
            pl.BlockSpec((1, Q_LORA), lambda i: (0, 0)),
            pl.BlockSpec((Q_LORA, n), lambda i: (0, 0)),
            tab, tab, tab,
        ],
        out_specs=pl.BlockSpec((tm, n), lambda i: (i, 0)),
        out_shape=jax.ShapeDtypeStruct((m, n), BF16),
        compiler_params=_params(("parallel",)),
        name="qproj",
    )(proj, norm_w, w_q, c, s1, s2)


def _kvproj_kernel(ckv_ref, kr_ref, nw_ref, wk_ref, wv_ref, c_ref, s1_ref, s2_ref, k_ref, v_ref):
    a = _rms(ckv_ref[...], nw_ref[...]).astype(BF16)
    kpe = _rope128(kr_ref[...], c_ref[...], s1_ref[...], s2_ref[...]).astype(BF16)
    ones = jnp.ones((a.shape[0], V_PAD - V_HEAD), BF16)
    for h2 in range(N_HEADS // 2):
        kn = jnp.dot(a, wk_ref[:, h2 * 256:(h2 + 1) * 256], preferred_element_type=F32).astype(BF16)
        vv = jnp.dot(a, wv_ref[:, h2 * 256:(h2 + 1) * 256], preferred_element_type=F32).astype(BF16)
        for d in range(2):
            h = 2 * h2 + d
            k_ref[:, h * QK_PAD:h * QK_PAD + LANES] = kn[:, d * LANES:(d + 1) * LANES]
            k_ref[:, h * QK_PAD + LANES:(h + 1) * QK_PAD] = kpe
            v_ref[:, h * V_PAD:h * V_PAD + V_HEAD] = vv[:, d * LANES:(d + 1) * LANES]
            v_ref[:, h * V_PAD + V_HEAD:(h + 1) * V_PAD] = ones


def _kvproj(proj, krope, norm_w, w_k, w_v, c, s1, s2, tm):
    m = proj.shape[0]
    tab = pl.BlockSpec((tm, LANES), lambda i: (i, 0))
    return pl.pallas_call(
        _kvproj_kernel,
        grid=(m // tm,),
        in_specs=[
            pl.BlockSpec((tm, KV_LORA), lambda i: (i, COL_CKV // KV_LORA)),
            tab,
            pl.BlockSpec((1, KV_LORA), lambda i: (0, 0)),
            pl.BlockSpec((KV_LORA, N_HEADS * QK_NOPE), lambda i: (0, 0)),
            pl.BlockSpec((KV_LORA, N_HEADS * V_HEAD), lambda i: (0, 0)),
            tab, tab, tab,
        ],
        out_specs=[
            pl.BlockSpec((tm, N_HEADS * QK_PAD), lambda i: (i, 0)),
            pl.BlockSpec((tm, N_HEADS * V_PAD), lambda i: (i, 0)),
        ],
        out_shape=[jax.ShapeDtypeStruct((m, N_HEADS * QK_PAD), BF16),
                   jax.ShapeDtypeStruct((m, N_HEADS * V_PAD), BF16)],
        compiler_params=_params(("parallel",)),
        name="kvproj",
    )(proj, krope, norm_w, w_k, w_v, c, s1, s2)


def _attn_kernel(q_ref, qn_ref, k_ref, v_ref, km_ref, vm_ref, o_ref,
                 acc_sc, acc0_sc, m_sc, s_sc, p_sc, al_sc, *, t):
    i = pl.program_id(1)
    heads = range(ATTN_HEADS)
    nt = (((1,), (1,)), ((), ()))
    qs = [q_ref[:, g * QK_PAD:(g + 1) * QK_PAD] for g in heads]

    def chunk(ref, g, width, c):
        return ref[pl.ds(pl.multiple_of(c * t, t), t), g * width:(g + 1) * width]

    def scores(g, c, slot, q=None):
        q = qs[g] if q is None else q
        s_sc[g, slot] = lax.dot_general(q, chunk(k_ref, g, QK_PAD, c), nt, preferred_element_type=F32)

    def first_scores(i_tile, q_of):
        for g in heads:
            scores(g, i_tile, 1, q_of(g))
            scores(g, 0, 0, q_of(g))

    def softmax(g, src, dst, masked):
        for r in range(t // ATTN_STRIP):
            rows = slice(r * ATTN_STRIP, (r + 1) * ATTN_STRIP)
            s = s_sc[g, src, rows, :]
            if masked:
                row = r * ATTN_STRIP + lax.broadcasted_iota(jnp.int32, s.shape, 0)
                s = jnp.where(lax.broadcasted_iota(jnp.int32, s.shape, 1) <= row, s, NEG_BIG)
            m_old = m_sc[g, rows, :]
            m_new = jnp.maximum(m_old, jnp.broadcast_to(jnp.max(s, axis=-1, keepdims=True), m_old.shape))
            p_sc[g, dst, rows, :] = jnp.concatenate(
                [jnp.exp2(s[:, c * LANES:(c + 1) * LANES] - m_new) for c in range(t // LANES)],
                axis=1).astype(BF16)
            al_sc[g, dst, rows, :] = jnp.exp2(m_old - m_new)
            m_sc[g, rows, :] = m_new

    def accumulate(g, slot, c):
        pv = jnp.dot(p_sc[g, slot], chunk(v_ref, g, V_PAD, c), preferred_element_type=F32)
        alpha = al_sc[g, slot]
        acc_sc[g] = jnp.concatenate([alpha, alpha], axis=1) * acc_sc[g] + pv

    def stage(k, src, with_next=True):
        for g in heads:
            if with_next:
                scores(g, jnp.minimum(k + 1, i - 1), 1 - src)
            accumulate(g, src, jnp.where(k == 0, i, k - 1))
            softmax(g, src, 1 - src, masked=False)

    def tile_start(i_tile, q_of):
        first_scores(i_tile, q_of)
        for g in heads:
            s = lax.dot_general(q_of(g), km_ref[:, g * QK_PAD:(g + 1) * QK_PAD], nt, preferred_element_type=F32)
            m0 = jnp.max(s, axis=-1, keepdims=True)
            acc0_sc[g] = jnp.dot(jnp.exp2(s - m0).astype(BF16), vm_ref[:, g * V_PAD:(g + 1) * V_PAD],
                                 preferred_element_type=F32)
            m_sc[g] = jnp.broadcast_to(m0, (t, LANES))
            softmax(g, 1, 0, masked=True)

    def tile_end(last_slot, last_chunk):
        for g in heads:
            accumulate(g, last_slot, last_chunk)
            acc = acc_sc[g]
            o_ref[:, g * V_HEAD:(g + 1) * V_HEAD] = (acc[:, :V_HEAD] / acc[:, V_HEAD:]).astype(o_ref.dtype)
        tile_start(jnp.minimum(i + 1, pl.num_programs(1) - 1),
                   lambda g: qn_ref[:, g * QK_PAD:(g + 1) * QK_PAD])

    @pl.when(i == 0)
    def _():
        tile_start(i, lambda g: qs[g])

    for g in heads:
        acc_sc[g] = acc0_sc[g]

    def run_stages(first, count):
        for u in range(count):
            stage(first + u, u % 2)

    def trip(j, _):
        run_stages(ATTN_UNROLL * j, ATTN_UNROLL)
        return 0

    n_trips = lax.shift_right_logical(i, int(math.log2(ATTN_UNROLL)))
    lax.fori_loop(0, n_trips, trip, 0)
    done = n_trips * ATTN_UNROLL
    width = ATTN_UNROLL // 2
    while width >= 2:
        @pl.when((i & width) != 0)
        def _(done=done, width=width):
            run_stages(done, width)

        done = done + (i & width)
        width //= 2

    odd = (i & 1) == 1

    @pl.when(odd)
    def _():
        stage(i - 1, 0, with_next=False)
        tile_end(1, i - 1)

    @pl.when(jnp.logical_not(odd))
    def _():
        tile_end(0, jnp.maximum(i - 1, 0))


def _attention(q, k, v, k_meta, v_meta, t):
    n = q.shape[0]
    g = ATTN_HEADS
    return pl.pallas_call(
        functools.partial(_attn_kernel, t=t),
        grid=(N_HEADS // g, n // t),
        in_specs=[
            pl.BlockSpec((t, g * QK_PAD), lambda h, i: (i, h)),
            pl.BlockSpec((t, g * QK_PAD), lambda h, i: (jnp.minimum(i + 1, n // t - 1), h)),
            pl.BlockSpec((n, g * QK_PAD), lambda h, i: (0, h)),
            pl.BlockSpec((n, g * V_PAD), lambda h, i: (0, h)),
            pl.BlockSpec((N_META, g * QK_PAD), lambda h, i: (0, h)),
            pl.BlockSpec((N_META, g * V_PAD), lambda h, i: (0, h)),
        ],
        out_specs=pl.BlockSpec((t, g * V_HEAD), lambda h, i: (i, h)),
        out_shape=jax.ShapeDtypeStruct((n, N_HEADS * V_HEAD), BF16),
        scratch_shapes=[
            pltpu.VMEM((g, t, V_PAD), F32), pltpu.VMEM((g, t, V_PAD), F32), pltpu.VMEM((g, t, LANES), F32),
            pltpu.VMEM((g, 2, t, t), F32), pltpu.VMEM((g, 2, t, t), BF16),
            pltpu.VMEM((g, 2, t, LANES), F32),
        ],
        compiler_params=_params(("parallel", "arbitrary")),
        name="attention",
    )(q, q, k, v, k_meta, v_meta)


def _merge_kernel(o_ref, u_ref, halo_ref, um_ref, gm_ref, gp_ref, wo_ref, pw_ref, ps_ref, wpo_ref,
                  out_ref, ext_sc, *, tm):
    i = pl.program_id(0)
    ext_sc[0:N_META, :] = jnp.where(i == 0, um_ref[...], halo_ref[...])
    ext_sc[N_META:, :] = u_ref[...]
    y_pool = jnp.zeros((tm, D_MODEL), F32)
    for g, w in enumerate(POOL_WINDOWS):
        lo, hi = g * POOL_GROUP_DIM, (g + 1) * POOL_GROUP_DIM
        u = ext_sc[N_META:, lo:hi]
        tot = u
        for d in range(1, w):
            tot = tot + ext_sc[N_META - d:N_META - d + tm, lo:hi]
        pooled = tot * (1.0 / w) - u
        mixed = jnp.dot(pooled, pw_ref[g], preferred_element_type=F32) * ps_ref[:, lo:hi]
        y_pool = y_pool + jnp.dot(mixed, wpo_ref[lo:hi, :], preferred_element_type=F32)
    y_mla = jnp.dot(o_ref[...].astype(F32), wo_ref[...], preferred_element_type=F32)
    merged = jax.nn.sigmoid(gm_ref[...]) * y_mla + jax.nn.sigmoid(gp_ref[...]) * y_pool
    out_ref[...] = merged.astype(out_ref.dtype)


def _merge(o, proj, u_meta, w_o, pool_w, pool_scale, w_pool_out, tm):
    n = o.shape[0]
    hb = tm // N_META
    const = lambda i: (0, 0)
    return pl.pallas_call(
        functools.partial(_merge_kernel, tm=tm),
        grid=(n // tm,),
        in_specs=[
            pl.BlockSpec((tm, D_MODEL), lambda i: (i, 0)),
            pl.BlockSpec((tm, POOL_WIDTH), lambda i: (i, COL_POOL // POOL_WIDTH)),
            pl.BlockSpec((N_META, POOL_WIDTH),
                         lambda i: (jnp.maximum(i * hb - 1, 0), COL_POOL // POOL_WIDTH)),
            pl.BlockSpec((N_META, POOL_WIDTH), lambda i: (0, COL_POOL // POOL_WIDTH)),
            pl.BlockSpec((tm, D_MODEL), lambda i: (i, COL_GM // D_MODEL)),
            pl.BlockSpec((tm, D_MODEL), lambda i: (i, COL_GP // D_MODEL)),
            pl.BlockSpec((D_MODEL, D_MODEL), const, pipeline_mode=pl.Buffered(1)),
            pl.BlockSpec((len(POOL_WINDOWS), POOL_GROUP_DIM, POOL_GROUP_DIM), lambda i: (0, 0, 0),
                         pipeline_mode=pl.Buffered(1)),
            pl.BlockSpec((1, POOL_WIDTH), const),
            pl.BlockSpec((POOL_WIDTH, D_MODEL), const, pipeline_mode=pl.Buffered(1)),
        ],
        out_specs=pl.BlockSpec((tm, D_MODEL), lambda i: (i, 0)),
        out_shape=jax.ShapeDtypeStruct((n, D_MODEL), BF16),
        scratch_shapes=[pltpu.VMEM((tm + N_META, POOL_WIDTH), F32)],
        compiler_params=_params(("arbitrary",)),
        name="merge",
    )(o, proj, proj, u_meta, proj, proj, w_o, pool_w, pool_scale, w_pool_out)


def _route_kernel(mg_ref, x_ref, wout_ref, nw_ref, wr_ref, br_ref, h_ref, bn_ref, ids_ref, wts_ref):
    h = x_ref[...] + jnp.dot(mg_ref[...].astype(F32), wout_ref[...], preferred_element_type=F32)
    h_ref[...] = h
    bn = _rms(h, nw_ref[...])
    bn_ref[...] = bn
    logits = jnp.dot(bn.astype(BF16), wr_ref[...], preferred_element_type=F32) + br_ref[...]
    lt = logits.T
    tm = lt.shape[1]
    sub = lax.broadcasted_iota(jnp.int32, (SUBLANES, tm), 0)
    sub_f = sub.astype(F32)

    def first_max(vals):
        vmax = jnp.max(vals, axis=0, keepdims=True)
        idx = jnp.min(jnp.where(vals == vmax, sub_f, float(SUBLANES)), axis=0, keepdims=True)
        return vmax, idx.astype(jnp.int32)

    glog = lt[:N_GROUPS, :]
    gmax, gidx = first_max(glog)
    p_g = 1.0 / jnp.sum(jnp.exp(glog - gmax), axis=0, keepdims=True)
    e_in = jnp.zeros((EXPERTS_PER_GROUP, tm), F32)
    for g in range(N_GROUPS):
        lo = N_GROUPS + g * EXPERTS_PER_GROUP
        e_in = jnp.where(gidx == g, lt[lo:lo + EXPERTS_PER_GROUP, :], e_in)
    v1, i1 = first_max(e_in)
    v2, i2 = first_max(jnp.where(sub == i1, -jnp.inf, e_in))
    e2 = jnp.exp(v2 - v1)
    den = 1.0 + e2
    w1 = p_g * (1.0 / den)
    w2 = p_g * (e2 / den)
    base = gidx * EXPERTS_PER_GROUP
    ids_ref[...] = jnp.where(sub == 0, base + i1, jnp.where(sub == 1, base + i2, 0))
    wts_ref[...] = jnp.where(sub == 0, w1, jnp.where(sub == 1, w2, 0.0))


def _route(merged, x, w_out, norm_w, w_router, b_router, tm):
    n = x.shape[0]
    const = lambda i: (0, 0)
    row = lambda width: pl.BlockSpec((tm, width), lambda i: (i, 0))
    return pl.pallas_call(
        _route_kernel,
        grid=(n // tm,),
        in_specs=[
            row(D_MODEL), row(D_MODEL),
            pl.BlockSpec((D_MODEL, D_MODEL), const, pipeline_mode=pl.Buffered(1)),
            pl.BlockSpec((1, D_MODEL), const),
            pl.BlockSpec((D_MODEL, LANES), const),
            pl.BlockSpec((1, LANES), const),
        ],
        out_specs=[row(D_MODEL), row(D_MODEL),
                   pl.BlockSpec((SUBLANES, tm), lambda i: (0, i)), pl.BlockSpec((SUBLANES, tm), lambda i: (0, i))],
        out_shape=[jax.ShapeDtypeStruct((n, D_MODEL), F32), jax.ShapeDtypeStruct((n, D_MODEL), F32),
                   jax.ShapeDtypeStruct((SUBLANES, n), jnp.int32), jax.ShapeDtypeStruct((SUBLANES, n), F32)],
        compiler_params=_params(("parallel",)),
        name="route",
    )(merged, x, w_out, norm_w, w_router, b_router)


def _for_rows(n, fn):
    shift = int(math.log2(SUBLANES))
    n_groups = lax.shift_right_logical(n, shift)

    def group(g, _):
        for u in range(SUBLANES):
            fn(g, u)
        return 0

    def single(r, _):
        fn(n_groups, r)
        return 0

    lax.fori_loop(0, n_groups, group, 0)
    lax.fori_loop(0, n - n_groups * SUBLANES, single, 0)


def _experts_kernel(be_ref, nv_ref, sb_ref, tok_ref, dst_ref, bn_hbm, wg_ref, wu_ref, wd_ref, y_hbm,
                    xbuf, ybuf, wg_sc, wu_sc, wd_sc, gsem, ssem):
    b = pl.program_id(0)
    nb = pl.num_programs(0)
    slot = lax.rem(b, 2)

    def gather(blk, slt):
        base = sb_ref[blk]

        def copy(g, u):
            tok = tok_ref[base + g * SUBLANES + u]
            return pltpu.make_async_copy(bn_hbm.at[pl.ds(tok, 1)], xbuf.at[slt, g, pl.ds(u, 1)],
                                         gsem.at[slt])
        return copy

    def scatter(blk, slt):
        base = sb_ref[blk]

        def copy(g, u):
            dst = dst_ref[base + g * SUBLANES + u]
            return pltpu.make_async_copy(ybuf.at[slt, g, pl.ds(u, 1)], y_hbm.at[pl.ds(dst, 1)],
                                         ssem.at[slt])
        return copy

    def gather_start(blk, slt):
        copy = gather(blk, slt)
        _for_rows(nv_ref[blk], lambda g, u: copy(g, u).start())

    def scatter_wait(blk, slt):
        copy = scatter(blk, slt)
        _for_rows(nv_ref[blk], lambda g, u: copy(g, u).wait())

    @pl.when(b == 0)
    def _():
        xbuf[...] = jnp.zeros(xbuf.shape, xbuf.dtype)
        gather_start(0, 0)

    @pl.when(b + 1 < nb)
    def _():
        gather_start(b + 1, 1 - slot)

    @pl.when(nv_ref[b] > 0)
    def _():
        wait_copy = gather(b, slot)
        _for_rows(nv_ref[b], lambda g, u: wait_copy(g, u).wait())
        x = xbuf[slot].reshape(ROUTE_BLOCK, D_MODEL)
        gate = jnp.dot(x, wg_ref[0], preferred_element_type=F32)
        up = jnp.dot(x, wu_ref[0], preferred_element_type=F32)
        hdn = gate * jax.nn.sigmoid(gate) * up
        y = jnp.dot(hdn, wd_ref[0], preferred_element_type=F32)
        ybuf[slot] = y.reshape(ROUTE_BLOCK // SUBLANES, SUBLANES, D_MODEL)
        start_copy = scatter(b, slot)
        _for_rows(nv_ref[b], lambda g, u: start_copy(g, u).start())

    @pl.when(b >= 1)
    def _():
        scatter_wait(b - 1, 1 - slot)

    @pl.when(b == nb - 1)
    def _():
        scatter_wait(b, slot)


def _experts(block_expert, block_rows, block_base, tok, dst, bn, w_gate, w_up, w_down):
    n = bn.shape[0]
    wmap = lambda b, be, nv, sb, tk, ds: (be[b], 0, 0)
    buf = pltpu.VMEM((2, ROUTE_BLOCK // SUBLANES, SUBLANES, D_MODEL), F32)
    grid_spec = pltpu.PrefetchScalarGridSpec(
        num_scalar_prefetch=5,
        grid=(block_expert.shape[0],),
        in_specs=[
            pl.BlockSpec(memory_space=pl.ANY),
            pl.BlockSpec((1, D_MODEL, EXPERT_FF), wmap),
            pl.BlockSpec((1, D_MODEL, EXPERT_FF), wmap),
            pl.BlockSpec((1, EXPERT_FF, D_MODEL), wmap),
        ],
        out_specs=pl.BlockSpec(memory_space=pl.ANY),
        scratch_shapes=[
            buf,
            buf,
            pltpu.VMEM((D_MODEL, EXPERT_FF), BF16),
            pltpu.VMEM((D_MODEL, EXPERT_FF), BF16),
            pltpu.VMEM((EXPERT_FF, D_MODEL), BF16),
            pltpu.SemaphoreType.DMA((2,)),
            pltpu.SemaphoreType.DMA((2,)),
        ],
    )
    return pl.pallas_call(
        _experts_kernel,
        grid_spec=grid_spec,
        out_shape=jax.ShapeDtypeStruct((TOP_K * n, D_MODEL), F32),
        compiler_params=_params(("arbitrary",)),
        name="experts",
    )(block_expert, block_rows, block_base, tok, dst, bn, w_gate, w_up, w_down)


def _route_blocks(flat_e):
    a = flat_e.shape[0]
    i32 = jnp.int32
    order = jnp.argsort(flat_e).astype(i32)
    counts = jnp.sum(flat_e[:, None] == jnp.arange(N_EXPERTS, dtype=i32)[None, :], axis=0, dtype=i32)
    padded = (counts + ROUTE_BLOCK - 1) // ROUTE_BLOCK * ROUTE_BLOCK
    pad_end = jnp.cumsum(padded).astype(i32)
    pad_start = pad_end - padded
    seg_start = jnp.cumsum(counts).astype(i32) - counts
    n_blocks = -(-(a + N_EXPERTS * (ROUTE_BLOCK - 1)) // ROUTE_BLOCK)
    blk = jnp.arange(n_blocks, dtype=i32)
    be = jnp.sum(pad_end[None, :] <= (blk * ROUTE_BLOCK)[:, None], axis=1, dtype=i32)
    be = jnp.minimum(be, N_EXPERTS - 1)
    off = blk * ROUTE_BLOCK - pad_start[be]
    rows = jnp.clip(counts[be] - off, 0, ROUTE_BLOCK).astype(i32)
    base = jnp.where(rows > 0, seg_start[be] + off, 0).astype(i32)
    n_used = pad_end[-1] // ROUTE_BLOCK
    be = jnp.where(blk < n_used, be, be[jnp.maximum(n_used - 1, 0)])
    tok = order // TOP_K
    dst = (order % TOP_K) * (a // TOP_K) + tok
    return be, rows, base, tok, dst


def _final_kernel(h_ref, y0_ref, y1_ref, wts_ref, nw_ref, o_ref):
    w = wts_ref[...]
    moe = y0_ref[...] * w[:, 0:1] + y1_ref[...] * w[:, 1:2]
    o_ref[...] = _rms(h_ref[...] + moe, nw_ref[...])


def _final(h, y, wts, norm_w, tm):
    n = h.shape[0]
    return pl.pallas_call(
        _final_kernel,
        grid=(n // tm,),
        in_specs=[
            pl.BlockSpec((tm, D_MODEL), lambda i: (i, 0)),
            pl.BlockSpec((tm, D_MODEL), lambda i: (i, 0)),
            pl.BlockSpec((tm, D_MODEL), lambda i: (i + n // tm, 0)),
            pl.BlockSpec((tm, TOP_K), lambda i: (i, 0)),
            pl.BlockSpec((1, D_MODEL), lambda i: (0, 0)),
        ],
        out_specs=pl.BlockSpec((tm, D_MODEL), lambda i: (i, 0)),
        out_shape=jax.ShapeDtypeStruct((n, D_MODEL), F32),
        compiler_params=_params(("parallel",)),
        name="final",
    )(h, y, y, wts, norm_w)


def _rope_tables(length):
    f32 = np.float32
    inv = (f32(1.0) / np.power(f32(ROPE_THETA), np.arange(0, QK_ROPE, 2, dtype=f32) / f32(QK_ROPE))).astype(f32)
    ang = (np.arange(length, dtype=f32)[:, None] * inv[None, :]).astype(f32)
    cos, sin = np.cos(ang).astype(f32), np.sin(ang).astype(f32)
    z32 = np.zeros_like(cos)
    z64 = np.zeros((length, LANES - QK_ROPE), f32)
    c = np.concatenate([cos, cos, z64], axis=1)
    s1 = np.concatenate([-sin, z32, z64], axis=1)
    s2 = np.concatenate([z32, sin, z64], axis=1)
    return c, s1, s2


def kernel(x, meta_tokens, norm_mix_w, w_in, q_norm_w, w_uq, kv_norm_w, w_ukv, w_o_mla, pool_w,
           pool_scale, w_pool_out, w_out, norm_ffn_w, w_router_group, b_router_group,
           w_router_expert, b_router_expert, w_exp_gate, w_exp_up, w_exp_down, final_norm_w):
    assert x.shape == (1, 8192, D_MODEL) and norm_mix_w.shape[0] == 1
    n = x.shape[1]
    xr = x[0]

    w_in_t = jnp.swapaxes(w_in[0], 0, 1).astype(BF16)
    w_q = jnp.pad(w_uq[0].reshape(Q_LORA, N_HEADS, QK_HEAD),
                  ((0, 0), (0, 0), (0, QK_PAD - QK_HEAD))).reshape(Q_LORA, N_HEADS * QK_PAD).astype(BF16)
    w_kv = w_ukv[0].reshape(KV_LORA, N_HEADS, QK_NOPE + V_HEAD)
    w_k = w_kv[:, :, :QK_NOPE].reshape(KV_LORA, N_HEADS * QK_NOPE).astype(BF16)
    w_v = w_kv[:, :, QK_NOPE:].reshape(KV_LORA, N_HEADS * V_HEAD).astype(BF16)
    w_router = jnp.pad(jnp.concatenate([w_router_group[0], w_router_expert[0]], axis=1),
                       ((0, 0), (0, LANES - N_GROUPS - N_EXPERTS))).astype(BF16)
    b_router = jnp.pad(jnp.concatenate([b_router_group[0], b_router_expert[0]]),
                       (0, LANES - N_GROUPS - N_EXPERTS))[None]
    c, s1, s2 = _rope_tables(N_META + n)

    proj, krope = _inproj(xr, norm_mix_w, w_in_t, tm=1024, tn=1024)
    proj_m, krope_m = _inproj(meta_tokens, norm_mix_w, w_in_t, tm=N_META, tn=1024)
    q = _qproj(proj, q_norm_w, w_q, c[N_META:], s1[N_META:], s2[N_META:], tm=512)
    k, v = _kvproj(proj, krope, kv_norm_w, w_k, w_v, c[N_META:], s1[N_META:], s2[N_META:], tm=512)
    k_m, v_m = _kvproj(proj_m, krope_m, kv_norm_w, w_k, w_v, c[:N_META], s1[:N_META], s2[:N_META],
                       tm=N_META)
    o = _attention(q, k, v, k_m, v_m, t=512)

    merged = _merge(o, proj, proj_m, w_o_mla[0], pool_w[0], pool_scale, w_pool_out[0], tm=256)
    h, bn, ids, wts = _route(merged, xr, w_out[0], norm_ffn_w, w_router, b_router, tm=512)

    be, rows, base, tok, dst = _route_blocks(ids[:TOP_K].T.reshape(n * TOP_K))
    wts = wts[:TOP_K].T
    y = _experts(be, rows, base, tok, dst, bn, w_exp_gate[0], w_exp_up[0], w_exp_down[0])

    out = _final(h, y, wts, final_norm_w[None], tm=512)
    return out[None]
```

```python
import functools
import math

import jax
import jax.numpy as jnp
import numpy as np
from jax import lax
from jax.experimental import pallas as pl
from jax.experimental.pallas import tpu as pltpu

F32 = jnp.float32
BF16 = jnp.bfloat16

D_MODEL = 2048
N_META = 16
EPS = 1e-6
N_HEADS = 16
QK_NOPE = 128
QK_ROPE = 64
QK_HEAD = QK_NOPE + QK_ROPE
V_HEAD = 128
Q_LORA = 512
KV_LORA = 512
ROPE_THETA = 10000.0
POOL_WINDOWS = (2, 4, 8, 16)
POOL_WIDTH = 1024
POOL_GROUP_DIM = 256
N_GROUPS = 8
EXPERTS_PER_GROUP = 8
N_EXPERTS = 64
TOP_K = 2
EXPERT_FF = 512

LANES = 128
SUBLANES = 8
QK_PAD = 256
V_PAD = 256
MAIN_COLS = Q_LORA + KV_LORA + POOL_WIDTH + 2 * D_MODEL
COL_CQ, COL_CKV, COL_POOL, COL_GM, COL_GP = 0, 512, 1024, 2048, 4096
ATTN_HEADS = 2
ATTN_UNROLL = 8
ATTN_STRIP = 64
ROUTE_BLOCK = 288
NEG_BIG = -1e30

VMEM_LIMIT = 56 * 1024 * 1024


def _params(sem):
    return pltpu.CompilerParams(dimension_semantics=sem, vmem_limit_bytes=VMEM_LIMIT)


def _rms(x, w):
    return x * lax.rsqrt(jnp.mean(x * x, axis=-1, keepdims=True) + EPS) * w


def _rope128(v, c, s1, s2):
    return v * c + pltpu.roll(v, 96, 1) * s1 + pltpu.roll(v, 32, 1) * s2


def _inproj_kernel(x_ref, nw_ref, wa_ref, wr_ref, wb_ref, o_ref, kr_ref, a_sc):
    j = pl.program_id(1)
    nt = (((1,), (1,)), ((), ()))

    @pl.when(j == 0)
    def _():
        a = _rms(x_ref[...], nw_ref[...]).astype(BF16)
        a_sc[...] = a
        kr_ref[...] = lax.dot_general(a, wr_ref[...], nt, preferred_element_type=F32)
        o_ref[...] = lax.dot_general(a, wa_ref[...], nt, preferred_element_type=F32)

    @pl.when(j > 0)
    def _():
        o_ref[...] = lax.dot_general(a_sc[...], wb_ref[...], nt, preferred_element_type=F32)


def _inproj(x, norm_w, w_t, tm, tn):
    m, k = x.shape
    n_a = Q_LORA + KV_LORA
    assert tn == n_a
    n = w_t.shape[0] - QK_ROPE
    return pl.pallas_call(
        _inproj_kernel,
        grid=(m // tm, n // tn),
        in_specs=[
            pl.BlockSpec((tm, k), lambda i, j: (i, 0)),
            pl.BlockSpec((1, k), lambda i, j: (0, 0)),
            pl.BlockSpec((tn, k), lambda i, j: (0, 0)),
            pl.BlockSpec((LANES, k), lambda i, j: (n_a // LANES, 0)),
            pl.BlockSpec((pl.Element(tn), pl.Element(k)),
                         lambda i, j: (pl.multiple_of(n_a + QK_ROPE + jnp.maximum(j - 1, 0) * tn, QK_ROPE), 0)),
        ],
        out_specs=[
            pl.BlockSpec((tm, tn), lambda i, j: (i, j)),
            pl.BlockSpec((tm, LANES), lambda i, j: (i, 0)),
        ],
        out_shape=[jax.ShapeDtypeStruct((m, n), F32), jax.ShapeDtypeStruct((m, LANES), F32)],
        scratch_shapes=[pltpu.VMEM((tm, k), BF16)],
        compiler_params=_params(("parallel", "arbitrary")),
        name="inproj",
    )(x, norm_w, w_t, w_t, w_t)


def _qproj_kernel(cq_ref, nw_ref, w_ref, c_ref, s1_ref, s2_ref, q_ref, *, scale):
    a = _rms(cq_ref[...], nw_ref[...]).astype(BF16)
    c, s1, s2 = c_ref[...], s1_ref[...], s2_ref[...]
    for h in range(N_HEADS):
        qh = jnp.dot(a, w_ref[:, h * QK_PAD:(h + 1) * QK_PAD], preferred_element_type=F32) * scale
        q_ref[:, h * QK_PAD:h * QK_PAD + LANES] = qh[:, :LANES].astype(BF16)
        q_ref[:, h * QK_PAD + LANES:(h + 1) * QK_PAD] = _rope128(qh[:, LANES:], c, s1, s2).astype(BF16)


def _qproj(proj, norm_w, w_q, c, s1, s2, tm):
    m = proj.shape[0]
    n = N_HEADS * QK_PAD
    tab = pl.BlockSpec((tm, LANES), lambda i: (i, 0))
    return pl.pallas_call(
        functools.partial(_qproj_kernel, scale=QK_HEAD ** -0.5 * math.log2(math.e)),
        grid=(m // tm,),
        in_specs=[
            pl.BlockSpec((tm, Q_LORA), lambda i: (i, COL_CQ // Q_LORA)),
            pl.BlockSpec((1, Q_LORA), lambda i: (0, 0)),
            pl.BlockSpec((Q_LORA, n), lambda i: (0, 0)),
            tab, tab, tab,
        ],
        out_specs=pl.BlockSpec((tm, n), lambda i: (i, 0)),
        out_shape=jax.ShapeDtypeStruct((m, n), BF16),
        compiler_params=_params(("parallel",)),
        name="qproj",
    )(proj, norm_w, w_q, c, s1, s2)


def _kvproj_kernel(ckv_ref, kr_ref, nw_ref, wk_ref, wv_ref, c_ref, s1_ref, s2_ref, k_ref, v_ref):
    a = _rms(ckv_ref[...], nw_ref[...]).astype(BF16)
    kpe = _rope128(kr_ref[...], c_ref[...], s1_ref[...], s2_ref[...]).astype(BF16)
    ones = jnp.ones((a.shape[0], V_PAD - V_HEAD), BF16)
    for h2 in range(N_HEADS // 2):
        kn = jnp.dot(a, wk_ref[:, h2 * 256:(h2 + 1) * 256], preferred_element_type=F32).astype(BF16)
        vv = jnp.dot(a, wv_ref[:, h2 * 256:(h2 + 1) * 256], preferred_element_type=F32).astype(BF16)
        for d in range(2):
            h = 2 * h2 + d
            k_ref[:, h * QK_PAD:h * QK_PAD + LANES] = kn[:, d * LANES:(d + 1) * LANES]
            k_ref[:, h * QK_PAD + LANES:(h + 1) * QK_PAD] = kpe
            v_ref[:, h * V_PAD:h * V_PAD + V_HEAD] = vv[:, d * LANES:(d + 1) * LANES]
            v_ref[:, h * V_PAD + V_HEAD:(h + 1) * V_PAD] = ones


def _kvproj(proj, krope, norm_w, w_k, w_v, c, s1, s2, tm):
    m = proj.shape[0]
    tab = pl.BlockSpec((tm, LANES), lambda i: (i, 0))
    return pl.pallas_call(
        _kvproj_kernel,
        grid=(m // tm,),
        in_specs=[
            pl.BlockSpec((tm, KV_LORA), lambda i: (i, COL_CKV // KV_LORA)),
            tab,
            pl.BlockSpec((1, KV_LORA), lambda i: (0, 0)),
            pl.BlockSpec((KV_LORA, N_HEADS * QK_NOPE), lambda i: (0, 0)),
            pl.BlockSpec((KV_LORA, N_HEADS * V_HEAD), lambda i: (0, 0)),
            tab, tab, tab,
        ],
        out_specs=[
            pl.BlockSpec((tm, N_HEADS * QK_PAD), lambda i: (i, 0)),
            pl.BlockSpec((tm, N_HEADS * V_PAD), lambda i: (i, 0)),
        ],
        out_shape=[jax.ShapeDtypeStruct((m, N_HEADS * QK_PAD), BF16),
                   jax.ShapeDtypeStruct((m, N_HEADS * V_PAD), BF16)],
        compiler_params=_params(("parallel",)),
        name="kvproj",
    )(proj, krope, norm_w, w_k, w_v, c, s1, s2)


def _attn_kernel(q_ref, qn_ref, k_ref, v_ref, km_ref, vm_ref, o_ref,
                 acc_sc, acc0_sc, m_sc, s_sc, p_sc, al_sc, *, t):
    i = pl.program_id(1)
    heads = range(ATTN_HEADS)
    nt = (((1,), (1,)), ((), ()))
    qs = [q_ref[:, g * QK_PAD:(g + 1) * QK_PAD] for g in heads]

    def chunk(ref, g, width, c):
        return ref[pl.ds(pl.multiple_of(c * t, t), t), g * width:(g + 1) * width]

    def scores(g, c, slot, q=None):
        q = qs[g] if q is None else q
        s_sc[g, slot] = lax.dot_general(q, chunk(k_ref, g, QK_PAD, c), nt, preferred_element_type=F32)

    def first_scores(i_tile, q_of):
        for g in heads:
            scores(g, i_tile, 1, q_of(g))
            scores(g, 0, 0, q_of(g))

    def softmax(g, src, dst, masked):
        for r in range(t // ATTN_STRIP):
            rows = slice(r * ATTN_STRIP, (r + 1) * ATTN_STRIP)
            s = s_sc[g, src, rows, :]
            if masked:
                row = r * ATTN_STRIP + lax.broadcasted_iota(jnp.int32, s.shape, 0)
                s = jnp.where(lax.broadcasted_iota(jnp.int32, s.shape, 1) <= row, s, NEG_BIG)
            m_old = m_sc[g, rows, :]
            m_new = jnp.maximum(m_old, jnp.broadcast_to(jnp.max(s, axis=-1, keepdims=True), m_old.shape))
            p_sc[g, dst, rows, :] = jnp.concatenate(
                [jnp.exp2(s[:, c * LANES:(c + 1) * LANES] - m_new) for c in range(t // LANES)],
                axis=1).astype(BF16)
            al_sc[g, dst, rows, :] = jnp.exp2(m_old - m_new)
            m_sc[g, rows, :] = m_new

    def accumulate(g, slot, c):
        pv = jnp.dot(p_sc[g, slot], chunk(v_ref, g, V_PAD, c), preferred_element_type=F32)
        alpha = al_sc[g, slot]
        acc_sc[g] = jnp.concatenate([alpha, alpha], axis=1) * acc_sc[g] + pv

    def stage(k, src, with_next=True):
        for g in heads:
            if with_next:
                scores(g, jnp.minimum(k + 1, i - 1), 1 - src)
            accumulate(g, src, jnp.where(k == 0, i, k - 1))
            softmax(g, src, 1 - src, masked=False)

    def tile_start(i_tile, q_of):
        first_scores(i_tile, q_of)
        for g in heads:
            s = lax.dot_general(q_of(g), km_ref[:, g * QK_PAD:(g + 1) * QK_PAD], nt, preferred_element_type=F32)
            m0 = jnp.max(s, axis=-1, keepdims=True)
            acc0_sc[g] = jnp.dot(jnp.exp2(s - m0).astype(BF16), vm_ref[:, g * V_PAD:(g + 1) * V_PAD],
                                 preferred_element_type=F32)
            m_sc[g] = jnp.broadcast_to(m0, (t, LANES))
            softmax(g, 1, 0, masked=True)

    def tile_end(last_slot, last_chunk):
        for g in heads:
            accumulate(g, last_slot, last_chunk)
            acc = acc_sc[g]
            o_ref[:, g * V_HEAD:(g + 1) * V_HEAD] = (acc[:, :V_HEAD] / acc[:, V_HEAD:]).astype(o_ref.dtype)
        tile_start(jnp.minimum(i + 1, pl.num_programs(1) - 1),
                   lambda g: qn_ref[:, g * QK_PAD:(g + 1) * QK_PAD])

    @pl.when(i == 0)
    def _():
        tile_start(i, lambda g: qs[g])

    for g in heads:
        acc_sc[g] = acc0_sc[g]

    def run_stages(first, count):
        for u in range(count):
            stage(first + u, u % 2)

    def trip(j, _):
        run_stages(ATTN_UNROLL * j, ATTN_UNROLL)
        return 0

    n_trips = lax.shift_right_logical(i, int(math.log2(ATTN_UNROLL)))
    lax.fori_loop(0, n_trips, trip, 0)
    done = n_trips * ATTN_UNROLL
    width = ATTN_UNROLL // 2
    while width >= 2:
        @pl.when((i & width) != 0)
        def _(done=done, width=width):
            run_stages(done, width)

        done = done + (i & width)
        width //= 2

    odd = (i & 1) == 1

    @pl.when(odd)
    def _():
        stage(i - 1, 0, with_next=False)
        tile_end(1, i - 1)

    @pl.when(jnp.logical_not(odd))
    def _():
        tile_end(0, jnp.maximum(i - 1, 0))


def _attention(q, k, v, k_meta, v_meta, t):
    n = q.shape[0]
    g = ATTN_HEADS
    return pl.pallas_call(
        functools.partial(_attn_kernel, t=t),
        grid=(N_HEADS // g, n // t),
        in_specs=[
            pl.BlockSpec((t, g * QK_PAD), lambda h, i: (i, h)),
            pl.BlockSpec((t, g * QK_PAD), lambda h, i: (jnp.minimum(i + 1, n // t - 1), h)),
            pl.BlockSpec((n, g * QK_PAD), lambda h, i: (0, h)),
            pl.BlockSpec((n, g * V_PAD), lambda h, i: (0, h)),
            pl.BlockSpec((N_META, g * QK_PAD), lambda h, i: (0, h)),
            pl.BlockSpec((N_META, g * V_PAD), lambda h, i: (0, h)),
        ],
        out_specs=pl.BlockSpec((t, g * V_HEAD), lambda h, i: (i, h)),
        out_shape=jax.ShapeDtypeStruct((n, N_HEADS * V_HEAD), BF16),
        scratch_shapes=[
            pltpu.VMEM((g, t, V_PAD), F32), pltpu.VMEM((g, t, V_PAD), F32), pltpu.VMEM((g, t, LANES), F32),
            pltpu.VMEM((g, 2, t, t), F32), pltpu.VMEM((g, 2, t, t), BF16),
            pltpu.VMEM((g, 2, t, LANES), F32),
        ],
        compiler_params=_params(("parallel", "arbitrary")),
        name="attention",
    )(q, q, k, v, k_meta, v_meta)


def _merge_kernel(o_ref, u_ref, halo_ref, um_ref, gm_ref, gp_ref, wo_ref, pw_ref, ps_ref, wpo_ref,
                  out_ref, ext_sc, *, tm):
    i = pl.program_id(0)
    ext_sc[0:N_META, :] = jnp.where(i == 0, um_ref[...], halo_ref[...])
    ext_sc[N_META:, :] = u_ref[...]
    y_pool = jnp.zeros((tm, D_MODEL), F32)
    for g, w in enumerate(POOL_WINDOWS):
        lo, hi = g * POOL_GROUP_DIM, (g + 1) * POOL_GROUP_DIM
        u = ext_sc[N_META:, lo:hi]
        tot = u
        for d in range(1, w):
            tot = tot + ext_sc[N_META - d:N_META - d + tm, lo:hi]
        pooled = tot * (1.0 / w) - u
        mixed = jnp.dot(pooled, pw_ref[g], preferred_element_type=F32) * ps_ref[:, lo:hi]
        y_pool = y_pool + jnp.dot(mixed, wpo_ref[lo:hi, :], preferred_element_type=F32)
    y_mla = jnp.dot(o_ref[...].astype(F32), wo_ref[...], preferred_element_type=F32)
    merged = jax.nn.sigmoid(gm_ref[...]) * y_mla + jax.nn.sigmoid(gp_ref[...]) * y_pool
    out_ref[...] = merged.astype(out_ref.dtype)


def _merge(o, proj, u_meta, w_o, pool_w, pool_scale, w_pool_out, tm):
    n = o.shape[0]
    hb = tm // N_META
    const = lambda i: (0, 0)
    return pl.pallas_call(
        functools.partial(_merge_kernel, tm=tm),
        grid=(n // tm,),
        in_specs=[
            pl.BlockSpec((tm, D_MODEL), lambda i: (i, 0)),
            pl.BlockSpec((tm, POOL_WIDTH), lambda i: (i, COL_POOL // POOL_WIDTH)),
            pl.BlockSpec((N_META, POOL_WIDTH),
                         lambda i: (jnp.maximum(i * hb - 1, 0), COL_POOL // POOL_WIDTH)),
            pl.BlockSpec((N_META, POOL_WIDTH), lambda i: (0, COL_POOL // POOL_WIDTH)),
            pl.BlockSpec((tm, D_MODEL), lambda i: (i, COL_GM // D_MODEL)),
            pl.BlockSpec((tm, D_MODEL), lambda i: (i, COL_GP // D_MODEL)),
            pl.BlockSpec((D_MODEL, D_MODEL), const, pipeline_mode=pl.Buffered(1)),
            pl.BlockSpec((len(POOL_WINDOWS), POOL_GROUP_DIM, POOL_GROUP_DIM), lambda i: (0, 0, 0),
                         pipeline_mode=pl.Buffered(1)),
            pl.BlockSpec((1, POOL_WIDTH), const),
            pl.BlockSpec((POOL_WIDTH, D_MODEL), const, pipeline_mode=pl.Buffered(1)),
        ],
        out_specs=pl.BlockSpec((tm, D_MODEL), lambda i: (i, 0)),
        out_shape=jax.ShapeDtypeStruct((n, D_MODEL), BF16),
        scratch_shapes=[pltpu.VMEM((tm + N_META, POOL_WIDTH), F32)],
        compiler_params=_params(("arbitrary",)),
        name="merge",
    )(o, proj, proj, u_meta, proj, proj, w_o, pool_w, pool_scale, w_pool_out)


def _route_kernel(mg_ref, x_ref, wout_ref, nw_ref, wr_ref, br_ref, h_ref, bn_ref, ids_ref, wts_ref):
    h = x_ref[...] + jnp.dot(mg_ref[...].astype(F32), wout_ref[...], preferred_element_type=F32)
    h_ref[...] = h
    bn = _rms(h, nw_ref[...])
    bn_ref[...] = bn
    logits = jnp.dot(bn.astype(BF16), wr_ref[...], preferred_element_type=F32) + br_ref[...]
    lt = logits.T
    tm = lt.shape[1]
    sub = lax.broadcasted_iota(jnp.int32, (SUBLANES, tm), 0)
    sub_f = sub.astype(F32)

    def first_max(vals):
        vmax = jnp.max(vals, axis=0, keepdims=True)
        idx = jnp.min(jnp.where(vals == vmax, sub_f, float(SUBLANES)), axis=0, keepdims=True)
        return vmax, idx.astype(jnp.int32)

    glog = lt[:N_GROUPS, :]
    gmax, gidx = first_max(glog)
    p_g = 1.0 / jnp.sum(jnp.exp(glog - gmax), axis=0, keepdims=True)
    e_in = jnp.zeros((EXPERTS_PER_GROUP, tm), F32)
    for g in range(N_GROUPS):
        lo = N_GROUPS + g * EXPERTS_PER_GROUP
        e_in = jnp.where(gidx == g, lt[lo:lo + EXPERTS_PER_GROUP, :], e_in)
    v1, i1 = first_max(e_in)
    v2, i2 = first_max(jnp.where(sub == i1, -jnp.inf, e_in))
    e2 = jnp.exp(v2 - v1)
    den = 1.0 + e2
    w1 = p_g * (1.0 / den)
    w2 = p_g * (e2 / den)
    base = gidx * EXPERTS_PER_GROUP
    ids_ref[...] = jnp.where(sub == 0, base + i1, jnp.where(sub == 1, base + i2, 0))
    wts_ref[...] = jnp.where(sub == 0, w1, jnp.where(sub == 1, w2, 0.0))


def _route(merged, x, w_out, norm_w, w_router, b_router, tm):
    n = x.shape[0]
    const = lambda i: (0, 0)
    row = lambda width: pl.BlockSpec((tm, width), lambda i: (i, 0))
    return pl.pallas_call(
        _route_kernel,
        grid=(n // tm,),
        in_specs=[
            row(D_MODEL), row(D_MODEL),
            pl.BlockSpec((D_MODEL, D_MODEL), const, pipeline_mode=pl.Buffered(1)),
            pl.BlockSpec((1, D_MODEL), const),
            pl.BlockSpec((D_MODEL, LANES), const),
            pl.BlockSpec((1, LANES), const),
        ],
        out_specs=[row(D_MODEL), row(D_MODEL),
                   pl.BlockSpec((SUBLANES, tm), lambda i: (0, i)), pl.BlockSpec((SUBLANES, tm), lambda i: (0, i))],
        out_shape=[jax.ShapeDtypeStruct((n, D_MODEL), F32), jax.ShapeDtypeStruct((n, D_MODEL), F32),
                   jax.ShapeDtypeStruct((SUBLANES, n), jnp.int32), jax.ShapeDtypeStruct((SUBLANES, n), F32)],
        compiler_params=_params(("parallel",)),
        name="route",
    )(merged, x, w_out, norm_w, w_router, b_router)


def _for_rows(n, fn):
    shift = int(math.log2(SUBLANES))
    n_groups = lax.shift_right_logical(n, shift)

    def group(g, _):
        for u in range(SUBLANES):
            fn(g, u)
        return 0

    def single(r, _):
        fn(n_groups, r)
        return 0

    lax.fori_loop(0, n_groups, group, 0)
    lax.fori_loop(0, n - n_groups * SUBLANES, single, 0)


def _experts_kernel(be_ref, nv_ref, sb_ref, tok_ref, dst_ref, bn_hbm, wg_ref, wu_ref, wd_ref, y_hbm,
                    xbuf, ybuf, wg_sc, wu_sc, wd_sc, gsem, ssem):
    b = pl.program_id(0)
    nb = pl.num_programs(0)
    slot = lax.rem(b, 2)

    def gather(blk, slt):
        base = sb_ref[blk]

        def copy(g, u):
            tok = tok_ref[base + g * SUBLANES + u]
            return pltpu.make_async_copy(bn_hbm.at[pl.ds(tok, 1)], xbuf.at[slt, g, pl.ds(u, 1)],
                                         gsem.at[slt])
        return copy

    def scatter(blk, slt):
        base = sb_ref[blk]

        def copy(g, u):
            dst = dst_ref[base + g * SUBLANES + u]
            return pltpu.make_async_copy(ybuf.at[slt, g, pl.ds(u, 1)], y_hbm.at[pl.ds(dst, 1)],
                                         ssem.at[slt])
        return copy

    def gather_start(blk, slt):
        copy = gather(blk, slt)
        _for_rows(nv_ref[blk], lambda g, u: copy(g, u).start())

    def scatter_wait(blk, slt):
        copy = scatter(blk, slt)
        _for_rows(nv_ref[blk], lambda g, u: copy(g, u).wait())

    @pl.when(b == 0)
    def _():
        xbuf[...] = jnp.zeros(xbuf.shape, xbuf.dtype)
        gather_start(0, 0)

    @pl.when(b + 1 < nb)
    def _():
        gather_start(b + 1, 1 - slot)

    @pl.when(nv_ref[b] > 0)
    def _():
        wait_copy = gather(b, slot)
        _for_rows(nv_ref[b], lambda g, u: wait_copy(g, u).wait())
        x = xbuf[slot].reshape(ROUTE_BLOCK, D_MODEL)
        gate = jnp.dot(x, wg_ref[0], preferred_element_type=F32)
        up = jnp.dot(x, wu_ref[0], preferred_element_type=F32)
        hdn = gate * jax.nn.sigmoid(gate) * up
        y = jnp.dot(hdn, wd_ref[0], preferred_element_type=F32)
        ybuf[slot] = y.reshape(ROUTE_BLOCK // SUBLANES, SUBLANES, D_MODEL)
        start_copy = scatter(b, slot)
        _for_rows(nv_ref[b], lambda g, u: start_copy(g, u).start())

    @pl.when(b >= 1)
    def _():
        scatter_wait(b - 1, 1 - slot)

    @pl.when(b == nb - 1)
    def _():
        scatter_wait(b, slot)


def _experts(block_expert, block_rows, block_base, tok, dst, bn, w_gate, w_up, w_down):
    n = bn.shape[0]
    wmap = lambda b, be, nv, sb, tk, ds: (be[b], 0, 0)
    buf = pltpu.VMEM((2, ROUTE_BLOCK // SUBLANES, SUBLANES, D_MODEL), F32)
    grid_spec = pltpu.PrefetchScalarGridSpec(
        num_scalar_prefetch=5,
        grid=(block_expert.shape[0],),
        in_specs=[
            pl.BlockSpec(memory_space=pl.ANY),
            pl.BlockSpec((1, D_MODEL, EXPERT_FF), wmap),
            pl.BlockSpec((1, D_MODEL, EXPERT_FF), wmap),
            pl.BlockSpec((1, EXPERT_FF, D_MODEL), wmap),
        ],
        out_specs=pl.BlockSpec(memory_space=pl.ANY),
        scratch_shapes=[
            buf,
            buf,
            pltpu.VMEM((D_MODEL, EXPERT_FF), BF16),
            pltpu.VMEM((D_MODEL, EXPERT_FF), BF16),
            pltpu.VMEM((EXPERT_FF, D_MODEL), BF16),
            pltpu.SemaphoreType.DMA((2,)),
            pltpu.SemaphoreType.DMA((2,)),
        ],
    )
    return pl.pallas_call(
        _experts_kernel,
        grid_spec=grid_spec,
        out_shape=jax.ShapeDtypeStruct((TOP_K * n, D_MODEL), F32),
        compiler_params=_params(("arbitrary",)),
        name="experts",
    )(block_expert, block_rows, block_base, tok, dst, bn, w_gate, w_up, w_down)


def _route_blocks(flat_e):
    a = flat_e.shape[0]
    i32 = jnp.int32
    order = jnp.argsort(flat_e).astype(i32)
    counts = jnp.sum(flat_e[:, None] == jnp.arange(N_EXPERTS, dtype=i32)[None, :], axis=0, dtype=i32)
    padded = (counts + ROUTE_BLOCK - 1) // ROUTE_BLOCK * ROUTE_BLOCK
    pad_end = jnp.cumsum(padded).astype(i32)
    pad_start = pad_end - padded
    seg_start = jnp.cumsum(counts).astype(i32) - counts
    n_blocks = -(-(a + N_EXPERTS * (ROUTE_BLOCK - 1)) // ROUTE_BLOCK)
    blk = jnp.arange(n_blocks, dtype=i32)
    be = jnp.sum(pad_end[None, :] <= (blk * ROUTE_BLOCK)[:, None], axis=1, dtype=i32)
    be = jnp.minimum(be, N_EXPERTS - 1)
    off = blk * ROUTE_BLOCK - pad_start[be]
    rows = jnp.clip(counts[be] - off, 0, ROUTE_BLOCK).astype(i32)
    base = jnp.where(rows > 0, seg_start[be] + off, 0).astype(i32)
    n_used = pad_end[-1] // ROUTE_BLOCK
    be = jnp.where(blk < n_used, be, be[jnp.maximum(n_used - 1, 0)])
    tok = order // TOP_K
    dst = (order % TOP_K) * (a // TOP_K) + tok
    return be, rows, base, tok, dst


def _final_kernel(h_ref, y0_ref, y1_ref, wts_ref, nw_ref, o_ref):
    w = wts_ref[...]
    moe = y0_ref[...] * w[:, 0:1] + y1_ref[...] * w[:, 1:2]
    o_ref[...] = _rms(h_ref[...] + moe, nw_ref[...])


def _final(h, y, wts, norm_w, tm):
    n = h.shape[0]
    return pl.pallas_call(
        _final_kernel,
        grid=(n // tm,),
        in_specs=[
            pl.BlockSpec((tm, D_MODEL), lambda i: (i, 0)),
            pl.BlockSpec((tm, D_MODEL), lambda i: (i, 0)),
            pl.BlockSpec((tm, D_MODEL), lambda i: (i + n // tm, 0)),
            pl.BlockSpec((tm, TOP_K), lambda i: (i, 0)),
            pl.BlockSpec((1, D_MODEL), lambda i: (0, 0)),
        ],
        out_specs=pl.BlockSpec((tm, D_MODEL), lambda i: (i, 0)),
        out_shape=jax.ShapeDtypeStruct((n, D_MODEL), F32),
        compiler_params=_params(("parallel",)),
        name="final",
    )(h, y, y, wts, norm_w)


def _rope_tables(length):
    f32 = np.float32
    inv = (f32(1.0) / np.power(f32(ROPE_THETA), np.arange(0, QK_ROPE, 2, dtype=f32) / f32(QK_ROPE))).astype(f32)
    ang = (np.arange(length, dtype=f32)[:, None] * inv[None, :]).astype(f32)
    cos, sin = np.cos(ang).astype(f32), np.sin(ang).astype(f32)
    z32 = np.zeros_like(cos)
    z64 = np.zeros((length, LANES - QK_ROPE), f32)
    c = np.concatenate([cos, cos, z64], axis=1)
    s1 = np.concatenate([-sin, z32, z64], axis=1)
    s2 = np.concatenate([z32, sin, z64], axis=1)
    return c, s1, s2


def kernel(x, meta_tokens, norm_mix_w, w_in, q_norm_w, w_uq, kv_norm_w, w_ukv, w_o_mla, pool_w,
           pool_scale, w_pool_out, w_out, norm_ffn_w, w_router_group, b_router_group,
           w_router_expert, b_router_expert, w_exp_gate, w_exp_up, w_exp_down, final_norm_w):
    assert x.shape == (1, 8192, D_MODEL) and norm_mix_w.shape[0] == 1
    n = x.shape[1]
    xr = x[0]

    w_in_t = jnp.swapaxes(w_in[0], 0, 1).astype(BF16)
    w_q = jnp.pad(w_uq[0].reshape(Q_LORA, N_HEADS, QK_HEAD),
                  ((0, 0), (0, 0), (0, QK_PAD - QK_HEAD))).reshape(Q_LORA, N_HEADS * QK_PAD).astype(BF16)
    w_kv = w_ukv[0].reshape(KV_LORA, N_HEADS, QK_NOPE + V_HEAD)
    w_k = w_kv[:, :, :QK_NOPE].reshape(KV_LORA, N_HEADS * QK_NOPE).astype(BF16)
    w_v = w_kv[:, :, QK_NOPE:].reshape(KV_LORA, N_HEADS * V_HEAD).astype(BF16)
    w_router = jnp.pad(jnp.concatenate([w_router_group[0], w_router_expert[0]], axis=1),
                       ((0, 0), (0, LANES - N_GROUPS - N_EXPERTS))).astype(BF16)
    b_router = jnp.pad(jnp.concatenate([b_router_group[0], b_router_expert[0]]),
                       (0, LANES - N_GROUPS - N_EXPERTS))[None]
    c, s1, s2 = _rope_tables(N_META + n)

    proj, krope = _inproj(xr, norm_mix_w, w_in_t, tm=1024, tn=1024)
    proj_m, krope_m = _inproj(meta_tokens, norm_mix_w, w_in_t, tm=N_META, tn=1024)
    q = _qproj(proj, q_norm_w, w_q, c[N_META:], s1[N_META:], s2[N_META:], tm=512)
    k, v = _kvproj(proj, krope, kv_norm_w, w_k, w_v, c[N_META:], s1[N_META:], s2[N_META:], tm=512)
    k_m, v_m = _kvproj(proj_m, krope_m, kv_norm_w, w_k, w_v, c[:N_META], s1[:N_META], s2[:N_META],
                       tm=N_META)
    o = _attention(q, k, v, k_m, v_m, t=512)

    merged = _merge(o, proj, proj_m, w_o_mla[0], pool_w[0], pool_scale, w_pool_out[0], tm=256)
    h, bn, ids, wts = _route(merged, xr, w_out[0], norm_ffn_w, w_router, b_router, tm=512)

    be, rows, base, tok, dst = _route_blocks(ids[:TOP_K].T.reshape(n * TOP_K))
    wts = wts[:TOP_K].T
    y = _experts(be, rows, base, tok, dst, bn, w_exp_gate[0], w_exp_up[0], w_exp_down[0])

    out = _final(h, y, wts, final_norm_w[None], tm=512)
    return out[None]
```

```python
import functools
import math

import jax
import jax.numpy as jnp
import numpy as np
from jax import lax
from jax.experimental import pallas as pl
from jax.experimental.pallas import tpu as pltpu

F32 = jnp.float32
BF16 = jnp.bfloat16

D_MODEL = 2048
N_META = 16
EPS = 1e-6
N_HEADS = 16
QK_NOPE = 128
QK_ROPE = 64
QK_HEAD = QK_NOPE + QK_ROPE
V_HEAD = 128
Q_LORA = 512
KV_LORA = 512
ROPE_THETA = 10000.0
POOL_WINDOWS = (2, 4, 8, 16)
POOL_WIDTH = 1024
POOL_GROUP_DIM = 256
N_GROUPS = 8
EXPERTS_PER_GROUP = 8
N_EXPERTS = 64
TOP_K = 2
EXPERT_FF = 512

LANES = 128
SUBLANES = 8
QK_PAD = 256
V_PAD = 256
MAIN_COLS = Q_LORA + KV_LORA + POOL_WIDTH + 2 * D_MODEL
COL_CQ, COL_CKV, COL_POOL, COL_GM, COL_GP = 0, 512, 1024, 2048, 4096
ATTN_HEADS = 2
ATTN_UNROLL = 8
ATTN_STRIP = 64
ROUTE_BLOCK = 288
NEG_BIG = -1e30

VMEM_LIMIT = 56 * 1024 * 1024


def _params(sem):
    return pltpu.CompilerParams(dimension_semantics=sem, vmem_limit_bytes=VMEM_LIMIT)


def _rms(x, w):
    return x * lax.rsqrt(jnp.mean(x * x, axis=-1, keepdims=True) + EPS) * w


def _rope128(v, c, s1, s2):
    return v * c + pltpu.roll(v, 96, 1) * s1 + pltpu.roll(v, 32, 1) * s2


def _inproj_kernel(x_ref, nw_ref, wa_ref, wr_ref, wb_ref, o_ref, kr_ref, a_sc):
    j = pl.program_id(1)
    nt = (((1,), (1,)), ((), ()))

    @pl.when(j == 0)
    def _():
        a = _rms(x_ref[...], nw_ref[...]).astype(BF16)
        a_sc[...] = a
        kr_ref[...] = lax.dot_general(a, wr_ref[...], nt, preferred_element_type=F32)
        o_ref[...] = lax.dot_general(a, wa_ref[...], nt, preferred_element_type=F32)

    @pl.when(j > 0)
    def _():
        o_ref[...] = lax.dot_general(a_sc[...], wb_ref[...], nt, preferred_element_type=F32)


def _inproj(x, norm_w, w_t, tm, tn):
    m, k = x.shape
    n_a = Q_LORA + KV_LORA
    assert tn == n_a
    n = w_t.shape[0] - QK_ROPE
    return pl.pallas_call(
        _inproj_kernel,
        grid=(m // tm, n // tn),
        in_specs=[
            pl.BlockSpec((tm, k), lambda i, j: (i, 0)),
            pl.BlockSpec((1, k), lambda i, j: (0, 0)),
            pl.BlockSpec((tn, k), lambda i, j: (0, 0)),
            pl.BlockSpec((LANES, k), lambda i, j: (n_a // LANES, 0)),
            pl.BlockSpec((pl.Element(tn), pl.Element(k)),
                         lambda i, j: (pl.multiple_of(n_a + QK_ROPE + jnp.maximum(j - 1, 0) * tn, QK_ROPE), 0)),
        ],
        out_specs=[
            pl.BlockSpec((tm, tn), lambda i, j: (i, j)),
            pl.BlockSpec((tm, LANES), lambda i, j: (i, 0)),
        ],
        out_shape=[jax.ShapeDtypeStruct((m, n), F32), jax.ShapeDtypeStruct((m, LANES), F32)],
        scratch_shapes=[pltpu.VMEM((tm, k), BF16)],
        compiler_params=_params(("parallel", "arbitrary")),
        name="inproj",
    )(x, norm_w, w_t, w_t, w_t)


def _qproj_kernel(cq_ref, nw_ref, w_ref, c_ref, s1_ref, s2_ref, q_ref, *, scale):
    a = _rms(cq_ref[...], nw_ref[...]).astype(BF16)
    c, s1, s2 = c_ref[...], s1_ref[...], s2_ref[...]
    for h in range(N_HEADS):
        qh = jnp.dot(a, w_ref[:, h * QK_PAD:(h + 1) * QK_PAD], preferred_element_type=F32) * scale
        q_ref[:, h * QK_PAD:h * QK_PAD + LANES] = qh[:, :LANES].astype(BF16)
        q_ref[:, h * QK_PAD + LANES:(h + 1) * QK_PAD] = _rope128(qh[:, LANES:], c, s1, s2).astype(BF16)


def _qproj(proj, norm_w, w_q, c, s1, s2, tm):
    m = proj.shape[0]
    n = N_HEADS * QK_PAD
    tab = pl.BlockSpec((tm, LANES), lambda i: (i, 0))
    return pl.pallas_call(
        functools.partial(_qproj_kernel, scale=QK_HEAD ** -0.5 * math.log2(math.e)),
        grid=(m // tm,),
        in_specs=[
            pl.BlockSpec((tm, Q_LORA), lambda i: (i, COL_CQ // Q_LORA)),
            pl.BlockSpec((1, Q_LORA), lambda i: (0, 0)),
            pl.BlockSpec((Q_LORA, n), lambda i: (0, 0)),
            tab, tab, tab,
        ],
        out_specs=pl.BlockSpec((tm, n), lambda i: (i, 0)),
        out_shape=jax.ShapeDtypeStruct((m, n), BF16),
        compiler_params=_params(("parallel",)),
        name="qproj",
    )(proj, norm_w, w_q, c, s1, s2)


def _kvproj_kernel(ckv_ref, kr_ref, nw_ref, wk_ref, wv_ref, c_ref, s1_ref, s2_ref, k_ref, v_ref):
    a = _rms(ckv_ref[...], nw_ref[...]).astype(BF16)
    kpe = _rope128(kr_ref[...], c_ref[...], s1_ref[...], s2_ref[...]).astype(BF16)
    ones = jnp.ones((a.shape[0], V_PAD - V_HEAD), BF16)
    for h2 in range(N_HEADS // 2):
        kn = jnp.dot(a, wk_ref[:, h2 * 256:(h2 + 1) * 256], preferred_element_type=F32).astype(BF16)
        vv = jnp.dot(a, wv_ref[:, h2 * 256:(h2 + 1) * 256], preferred_element_type=F32).astype(BF16)
        for d in range(2):
            h = 2 * h2 + d
            k_ref[:, h * QK_PAD:h * QK_PAD + LANES] = kn[:, d * LANES:(d + 1) * LANES]
            k_ref[:, h * QK_PAD + LANES:(h + 1) * QK_PAD] = kpe
            v_ref[:, h * V_PAD:h * V_PAD + V_HEAD] = vv[:, d * LANES:(d + 1) * LANES]
            v_ref[:, h * V_PAD + V_HEAD:(h + 1) * V_PAD] = ones


def _kvproj(proj, krope, norm_w, w_k, w_v, c, s1, s2, tm):
    m = proj.shape[0]
    tab = pl.BlockSpec((tm, LANES), lambda i: (i, 0))
    return pl.pallas_call(
        _kvproj_kernel,
        grid=(m // tm,),
        in_specs=[
            pl.BlockSpec((tm, KV_LORA), lambda i: (i, COL_CKV // KV_LORA)),
            tab,
            pl.BlockSpec((1, KV_LORA), lambda i: (0, 0)),
            pl.BlockSpec((KV_LORA, N_HEADS * QK_NOPE), lambda i: (0, 0)),
            pl.BlockSpec((KV_LORA, N_HEADS * V_HEAD), lambda i: (0, 0)),
            tab, tab, tab,
        ],
        out_specs=[
            pl.BlockSpec((tm, N_HEADS * QK_PAD), lambda i: (i, 0)),
            pl.BlockSpec((tm, N_HEADS * V_PAD), lambda i: (i, 0)),
        ],
        out_shape=[jax.ShapeDtypeStruct((m, N_HEADS * QK_PAD), BF16),
                   jax.ShapeDtypeStruct((m, N_HEADS * V_PAD), BF16)],
        compiler_params=_params(("parallel",)),
        name="kvproj",
    )(proj, krope, norm_w, w_k, w_v, c, s1, s2)


def _attn_kernel(q_ref, qn_ref, k_ref, v_ref, km_ref, vm_ref, o_ref,
                 acc_sc, acc0_sc, m_sc, s_sc, p_sc, al_sc, *, t):
    i = pl.program_id(1)
    heads = range(ATTN_HEADS)
    nt = (((1,), (1,)), ((), ()))
    qs = [q_ref[:, g * QK_PAD:(g + 1) * QK_PAD] for g in heads]

    def chunk(ref, g, width, c):
        return ref[pl.ds(pl.multiple_of(c * t, t), t), g * width:(g + 1) * width]

    def scores(g, c, slot, q=None):
        q = qs[g] if q is None else q
        s_sc[g, slot] = lax.dot_general(q, chunk(k_ref, g, QK_PAD, c), nt, preferred_element_type=F32)

    def first_scores(i_tile, q_of):
        for g in heads:
            scores(g, i_tile, 1, q_of(g))
            scores(g, 0, 0, q_of(g))

    def softmax(g, src, dst, masked):
        for r in range(t // ATTN_STRIP):
            rows = slice(r * ATTN_STRIP, (r + 1) * ATTN_STRIP)
            s = s_sc[g, src, rows, :]
            if masked:
                row = r * ATTN_STRIP + lax.broadcasted_iota(jnp.int32, s.shape, 0)
                s = jnp.where(lax.broadcasted_iota(jnp.int32, s.shape, 1) <= row, s, NEG_BIG)
            m_old = m_sc[g, rows, :]
            m_new = jnp.maximum(m_old, jnp.broadcast_to(jnp.max(s, axis=-1, keepdims=True), m_old.shape))
            p_sc[g, dst, rows, :] = jnp.concatenate(
                [jnp.exp2(s[:, c * LANES:(c + 1) * LANES] - m_new) for c in range(t // LANES)],
                axis=1).astype(BF16)
            al_sc[g, dst, rows, :] = jnp.exp2(m_old - m_new)
            m_sc[g, rows, :] = m_new

    def accumulate(g, slot, c):
        pv = jnp.dot(p_sc[g, slot], chunk(v_ref, g, V_PAD, c), preferred_element_type=F32)
        alpha = al_sc[g, slot]
        acc_sc[g] = jnp.concatenate([alpha, alpha], axis=1) * acc_sc[g] + pv

    def stage(k, src, with_next=True):
        for g in heads:
            if with_next:
                scores(g, jnp.minimum(k + 1, i - 1), 1 - src)
            accumulate(g, src, jnp.where(k == 0, i, k - 1))
            softmax(g, src, 1 - src, masked=False)

    def tile_start(i_tile, q_of):
        first_scores(i_tile, q_of)
        for g in heads:
            s = lax.dot_general(q_of(g), km_ref[:, g * QK_PAD:(g + 1) * QK_PAD], nt, preferred_element_type=F32)
            m0 = jnp.max(s, axis=-1, keepdims=True)
            acc0_sc[g] = jnp.dot(jnp.exp2(s - m0).astype(BF16), vm_ref[:, g * V_PAD:(g + 1) * V_PAD],
                                 preferred_element_type=F32)
            m_sc[g] = jnp.broadcast_to(m0, (t, LANES))
            softmax(g, 1, 0, masked=True)

    def tile_end(last_slot, last_chunk):
        for g in heads:
            accumulate(g, last_slot, last_chunk)
            acc = acc_sc[g]
            o_ref[:, g * V_HEAD:(g + 1) * V_HEAD] = (acc[:, :V_HEAD] / acc[:, V_HEAD:]).astype(o_ref.dtype)
        tile_start(jnp.minimum(i + 1, pl.num_programs(1) - 1),
                   lambda g: qn_ref[:, g * QK_PAD:(g + 1) * QK_PAD])

    @pl.when(i == 0)
    def _():
        tile_start(i, lambda g: qs[g])

    for g in heads:
        acc_sc[g] = acc0_sc[g]

    def run_stages(first, count):
        for u in range(count):
            stage(first + u, u % 2)

    def trip(j, _):
        run_stages(ATTN_UNROLL * j, ATTN_UNROLL)
        return 0

    n_trips = lax.shift_right_logical(i, int(math.log2(ATTN_UNROLL)))
    lax.fori_loop(0, n_trips, trip, 0)
    done = n_trips * ATTN_UNROLL
    width = ATTN_UNROLL // 2
    while width >= 2:
        @pl.when((i & width) != 0)
        def _(done=done, width=width):
            run_stages(done, width)

        done = done + (i & width)
        width //= 2

    odd = (i & 1) == 1

    @pl.when(odd)
    def _():
        stage(i - 1, 0, with_next=False)
        tile_end(1, i - 1)

    @pl.when(jnp.logical_not(odd))
    def _():
        tile_end(0, jnp.maximum(i - 1, 0))


def _attention(q, k, v, k_meta, v_meta, t):
    n = q.shape[0]
    g = ATTN_HEADS
    return pl.pallas_call(
        functools.partial(_attn_kernel, t=t),
        grid=(N_HEADS // g, n // t),
        in_specs=[
            pl.BlockSpec((t, g * QK_PAD), lambda h, i: (i, h)),
            pl.BlockSpec((t, g * QK_PAD), lambda h, i: (jnp.minimum(i + 1, n // t - 1), h)),
            pl.BlockSpec((n, g * QK_PAD), lambda h, i: (0, h)),
            pl.BlockSpec((n, g * V_PAD), lambda h, i: (0, h)),
            pl.BlockSpec((N_META, g * QK_PAD), lambda h, i: (0, h)),
            pl.BlockSpec((N_META, g * V_PAD), lambda h, i: (0, h)),
        ],
        out_specs=pl.BlockSpec((t, g * V_HEAD), lambda h, i: (i, h)),
        out_shape=jax.ShapeDtypeStruct((n, N_HEADS * V_HEAD), BF16),
        scratch_shapes=[
            pltpu.VMEM((g, t, V_PAD), F32), pltpu.VMEM((g, t, V_PAD), F32), pltpu.VMEM((g, t, LANES), F32),
            pltpu.VMEM((g, 2, t, t), F32), pltpu.VMEM((g, 2, t, t), BF16),
            pltpu.VMEM((g, 2, t, LANES), F32),
        ],
        compiler_params=_params(("parallel", "arbitrary")),
        name="attention",
    )(q, q, k, v, k_meta, v_meta)


def _merge_kernel(o_ref, u_ref, halo_ref, um_ref, gm_ref, gp_ref, wo_ref, pw_ref, ps_ref, wpo_ref,
                  out_ref, ext_sc, *, tm):
    i = pl.program_id(0)
    ext_sc[0:N_META, :] = jnp.where(i == 0, um_ref[...], halo_ref[...])
    ext_sc[N_META:, :] = u_ref[...]
    y_pool = jnp.zeros((tm, D_MODEL), F32)
    for g, w in enumerate(POOL_WINDOWS):
        lo, hi = g * POOL_GROUP_DIM, (g + 1) * POOL_GROUP_DIM
        u = ext_sc[N_META:, lo:hi]
        tot = u
        for d in range(1, w):
            tot = tot + ext_sc[N_META - d:N_META - d + tm, lo:hi]
        pooled = tot * (1.0 / w) - u
        mixed = jnp.dot(pooled, pw_ref[g], preferred_element_type=F32) * ps_ref[:, lo:hi]
        y_pool = y_pool + jnp.dot(mixed, wpo_ref[lo:hi, :], preferred_element_type=F32)
    y_mla = jnp.dot(o_ref[...].astype(F32), wo_ref[...], preferred_element_type=F32)
    merged = jax.nn.sigmoid(gm_ref[...]) * y_mla + jax.nn.sigmoid(gp_ref[...]) * y_pool
    out_ref[...] = merged.astype(out_ref.dtype)


def _merge(o, proj, u_meta, w_o, pool_w, pool_scale, w_pool_out, tm):
    n = o.shape[0]
    hb = tm // N_META
    const = lambda i: (0, 0)
    return pl.pallas_call(
        functools.partial(_merge_kernel, tm=tm),
        grid=(n // tm,),
        in_specs=[
            pl.BlockSpec((tm, D_MODEL), lambda i: (i, 0)),
            pl.BlockSpec((tm, POOL_WIDTH), lambda i: (i, COL_POOL // POOL_WIDTH)),
            pl.BlockSpec((N_META, POOL_WIDTH),
                         lambda i: (jnp.maximum(i * hb - 1, 0), COL_POOL // POOL_WIDTH)),
            pl.BlockSpec((N_META, POOL_WIDTH), lambda i: (0, COL_POOL // POOL_WIDTH)),
            pl.BlockSpec((tm, D_MODEL), lambda i: (i, COL_GM // D_MODEL)),
            pl.BlockSpec((tm, D_MODEL), lambda i: (i, COL_GP // D_MODEL)),
            pl.BlockSpec((D_MODEL, D_MODEL), const, pipeline_mode=pl.Buffered(1)),
            pl.BlockSpec((len(POOL_WINDOWS), POOL_GROUP_DIM, POOL_GROUP_DIM), lambda i: (0, 0, 0),
                         pipeline_mode=pl.Buffered(1)),
            pl.BlockSpec((1, POOL_WIDTH), const),
            pl.BlockSpec((POOL_WIDTH, D_MODEL), const, pipeline_mode=pl.Buffered(1)),
        ],
        out_specs=pl.BlockSpec((tm, D_MODEL), lambda i: (i, 0)),
        out_shape=jax.ShapeDtypeStruct((n, D_MODEL), BF16),
        scratch_shapes=[pltpu.VMEM((tm + N_META, POOL_WIDTH), F32)],
        compiler_params=_params(("arbitrary",)),
        name="merge",
    )(o, proj, proj, u_meta, proj, proj, w_o, pool_w, pool_scale, w_pool_out)


def _route_kernel(mg_ref, x_ref, wout_ref, nw_ref, wr_ref, br_ref, h_ref, bn_ref, ids_ref, wts_ref):
    h = x_ref[...] + jnp.dot(mg_ref[...].astype(F32), wout_ref[...], preferred_element_type=F32)
    h_ref[...] = h
    bn = _rms(h, nw_ref[...])
    half = D_MODEL // 2
    lo = lax.bitcast_convert_type(bn[:, :half].astype(BF16).astype(F32), jnp.uint32)
    hi = lax.bitcast_convert_type(bn[:, half:].astype(BF16).astype(F32), jnp.uint32)
    bn_ref[...] = (hi & jnp.uint32(0xFFFF0000)) | lax.shift_right_logical(lo, jnp.uint32(16))
    logits = jnp.dot(bn.astype(BF16), wr_ref[...], preferred_element_type=F32) + br_ref[...]
    lt = logits.T
    tm = lt.shape[1]
    sub = lax.broadcasted_iota(jnp.int32, (SUBLANES, tm), 0)
    sub_f = sub.astype(F32)

    def first_max(vals):
        vmax = jnp.max(vals, axis=0, keepdims=True)
        idx = jnp.min(jnp.where(vals == vmax, sub_f, float(SUBLANES)), axis=0, keepdims=True)
        return vmax, idx.astype(jnp.int32)

    glog = lt[:N_GROUPS, :]
    gmax, gidx = first_max(glog)
    p_g = 1.0 / jnp.sum(jnp.exp(glog - gmax), axis=0, keepdims=True)
    e_in = jnp.zeros((EXPERTS_PER_GROUP, tm), F32)
    for g in range(N_GROUPS):
        lo = N_GROUPS + g * EXPERTS_PER_GROUP
        e_in = jnp.where(gidx == g, lt[lo:lo + EXPERTS_PER_GROUP, :], e_in)
    v1, i1 = first_max(e_in)
    v2, i2 = first_max(jnp.where(sub == i1, -jnp.inf, e_in))
    e2 = jnp.exp(v2 - v1)
    den = 1.0 + e2
    w1 = p_g * (1.0 / den)
    w2 = p_g * (e2 / den)
    base = gidx * EXPERTS_PER_GROUP
    ids_ref[...] = jnp.where(sub == 0, base + i1, jnp.where(sub == 1, base + i2, 0))
    wts_ref[...] = jnp.where(sub == 0, w1, jnp.where(sub == 1, w2, 0.0))


def _route(merged, x, w_out, norm_w, w_router, b_router, tm):
    n = x.shape[0]
    const = lambda i: (0, 0)
    row = lambda width: pl.BlockSpec((tm, width), lambda i: (i, 0))
    return pl.pallas_call(
        _route_kernel,
        grid=(n // tm,),
        in_specs=[
            row(D_MODEL), row(D_MODEL),
            pl.BlockSpec((D_MODEL, D_MODEL), const, pipeline_mode=pl.Buffered(1)),
            pl.BlockSpec((1, D_MODEL), const),
            pl.BlockSpec((D_MODEL, LANES), const),
            pl.BlockSpec((1, LANES), const),
        ],
        out_specs=[row(D_MODEL), row(D_MODEL // 2),
                   pl.BlockSpec((SUBLANES, tm), lambda i: (0, i)), pl.BlockSpec((SUBLANES, tm), lambda i: (0, i))],
        out_shape=[jax.ShapeDtypeStruct((n, D_MODEL), F32), jax.ShapeDtypeStruct((n, D_MODEL // 2), jnp.uint32),
                   jax.ShapeDtypeStruct((SUBLANES, n), jnp.int32), jax.ShapeDtypeStruct((SUBLANES, n), F32)],
        compiler_params=_params(("parallel",)),
        name="route",
    )(merged, x, w_out, norm_w, w_router, b_router)


def _for_rows(n, fn):
    shift = int(math.log2(SUBLANES))
    n_groups = lax.shift_right_logical(n, shift)

    def group(g, _):
        for u in range(SUBLANES):
            fn(g, u)
        return 0

    def single(r, _):
        fn(n_groups, r)
        return 0

    lax.fori_loop(0, n_groups, group, 0)
    lax.fori_loop(0, n - n_groups * SUBLANES, single, 0)


def _experts_kernel(be_ref, nv_ref, sb_ref, tok_ref, dst_ref, bn_hbm, wg_ref, wu_ref, wd_ref, y_hbm,
                    xbuf, ybuf, gsem, ssem):
    b = pl.program_id(0)
    nb = pl.num_programs(0)
    slot = lax.rem(b, 2)

    def gather(blk, slt):
        base = sb_ref[blk]

        def copy(g, u):
            tok = tok_ref[base + g * SUBLANES + u]
            return pltpu.make_async_copy(bn_hbm.at[pl.ds(tok, 1)], xbuf.at[slt, g, pl.ds(u, 1)],
                                         gsem.at[slt])
        return copy

    def scatter(blk, slt):
        base = sb_ref[blk]

        def copy(g, u):
            dst = dst_ref[base + g * SUBLANES + u]
            return pltpu.make_async_copy(ybuf.at[slt, g, pl.ds(u, 1)], y_hbm.at[pl.ds(dst, 1)],
                                         ssem.at[slt])
        return copy

    def gather_start(blk, slt):
        copy = gather(blk, slt)
        _for_rows(nv_ref[blk], lambda g, u: copy(g, u).start())

    def scatter_wait(blk, slt):
        copy = scatter(blk, slt)
        _for_rows(nv_ref[blk], lambda g, u: copy(g, u).wait())

    @pl.when(b == 0)
    def _():
        xbuf[...] = jnp.zeros(xbuf.shape, xbuf.dtype)
        gather_start(0, 0)

    @pl.when(b + 1 < nb)
    def _():
        gather_start(b + 1, 1 - slot)

    @pl.when(nv_ref[b] > 0)
    def _():
        wait_copy = gather(b, slot)
        _for_rows(nv_ref[b], lambda g, u: wait_copy(g, u).wait())
        xw = xbuf[slot].reshape(ROUTE_BLOCK, D_MODEL // 2)
        x = jnp.concatenate(
            [lax.bitcast_convert_type(lax.shift_left(xw, jnp.uint32(16)), F32),
             lax.bitcast_convert_type(xw & jnp.uint32(0xFFFF0000), F32)], axis=1)
        gate = jnp.dot(x, wg_ref[0], preferred_element_type=F32)
        up = jnp.dot(x, wu_ref[0], preferred_element_type=F32)
        hdn = gate * jax.nn.sigmoid(gate) * up
        y = jnp.dot(hdn, wd_ref[0], preferred_element_type=F32)
        ybuf[slot] = y.reshape(ROUTE_BLOCK // SUBLANES, SUBLANES, D_MODEL)
        start_copy = scatter(b, slot)
        _for_rows(nv_ref[b], lambda g, u: start_copy(g, u).start())

    @pl.when(b >= 1)
    def _():
        scatter_wait(b - 1, 1 - slot)

    @pl.when(b == nb - 1)
    def _():
        scatter_wait(b, slot)


def _experts(block_expert, block_rows, block_base, tok, dst, bn, w_gate, w_up, w_down):
    n = bn.shape[0]
    wmap = lambda b, be, nv, sb, tk, ds: (be[b], 0, 0)
    groups = ROUTE_BLOCK // SUBLANES
    grid_spec = pltpu.PrefetchScalarGridSpec(
        num_scalar_prefetch=5,
        grid=(block_expert.shape[0],),
        in_specs=[
            pl.BlockSpec(memory_space=pl.ANY),
            pl.BlockSpec((1, D_MODEL, EXPERT_FF), wmap),
            pl.BlockSpec((1, D_MODEL, EXPERT_FF), wmap),
            pl.BlockSpec((1, EXPERT_FF, D_MODEL), wmap),
        ],
        out_specs=pl.BlockSpec(memory_space=pl.ANY),
        scratch_shapes=[
            pltpu.VMEM((2, groups, SUBLANES, D_MODEL // 2), jnp.uint32),
            pltpu.VMEM((2, groups, SUBLANES, D_MODEL), F32),
            pltpu.SemaphoreType.DMA((2,)),
            pltpu.SemaphoreType.DMA((2,)),
        ],
    )
    return pl.pallas_call(
        _experts_kernel,
        grid_spec=grid_spec,
        out_shape=jax.ShapeDtypeStruct((TOP_K * n, D_MODEL), F32),
        compiler_params=_params(("arbitrary",)),
        name="experts",
    )(block_expert, block_rows, block_base, tok, dst, bn, w_gate, w_up, w_down)


def _route_blocks(flat_e):
    a = flat_e.shape[0]
    i32 = jnp.int32
    order = jnp.argsort(flat_e).astype(i32)
    counts = jnp.sum(flat_e[:, None] == jnp.arange(N_EXPERTS, dtype=i32)[None, :], axis=0, dtype=i32)
    padded = (counts + ROUTE_BLOCK - 1) // ROUTE_BLOCK * ROUTE_BLOCK
    pad_end = jnp.cumsum(padded).astype(i32)
    pad_start = pad_end - padded
    seg_start = jnp.cumsum(counts).astype(i32) - counts
    n_blocks = -(-(a + N_EXPERTS * (ROUTE_BLOCK - 1)) // ROUTE_BLOCK)
    blk = jnp.arange(n_blocks, dtype=i32)
    be = jnp.sum(pad_end[None, :] <= (blk * ROUTE_BLOCK)[:, None], axis=1, dtype=i32)
    be = jnp.minimum(be, N_EXPERTS - 1)
    off = blk * ROUTE_BLOCK - pad_start[be]
    rows = jnp.clip(counts[be] - off, 0, ROUTE_BLOCK).astype(i32)
    base = jnp.where(rows > 0, seg_start[be] + off, 0).astype(i32)
    n_used = pad_end[-1] // ROUTE_BLOCK
    be = jnp.where(blk < n_used, be, be[jnp.maximum(n_used - 1, 0)])
    tok = order // TOP_K
    dst = (order % TOP_K) * (a // TOP_K) + tok
    return be, rows, base, tok, dst


def _final_kernel(h_ref, y0_ref, y1_ref, wts_ref, nw_ref, o_ref):
    w = wts_ref[...]
    moe = y0_ref[...] * w[:, 0:1] + y1_ref[...] * w[:, 1:2]
    o_ref[...] = _rms(h_ref[...] + moe, nw_ref[...])


def _final(h, y, wts, norm_w, tm):
    n = h.shape[0]
    return pl.pallas_call(
        _final_kernel,
        grid=(n // tm,),
        in_specs=[
            pl.BlockSpec((tm, D_MODEL), lambda i: (i, 0)),
            pl.BlockSpec((tm, D_MODEL), lambda i: (i, 0)),
            pl.BlockSpec((tm, D_MODEL), lambda i: (i + n // tm, 0)),
            pl.BlockSpec((tm, TOP_K), lambda i: (i, 0)),
            pl.BlockSpec((1, D_MODEL), lambda i: (0, 0)),
        ],
        out_specs=pl.BlockSpec((tm, D_MODEL), lambda i: (i, 0)),
        out_shape=jax.ShapeDtypeStruct((n, D_MODEL), F32),
        compiler_params=_params(("parallel",)),
        name="final",
    )(h, y, y, wts, norm_w)


def _rope_tables(length):
    f32 = np.float32
    inv = (f32(1.0) / np.power(f32(ROPE_THETA), np.arange(0, QK_ROPE, 2, dtype=f32) / f32(QK_ROPE))).astype(f32)
    ang = (np.arange(length, dtype=f32)[:, None] * inv[None, :]).astype(f32)
    cos, sin = np.cos(ang).astype(f32), np.sin(ang).astype(f32)
    z32 = np.zeros_like(cos)
    z64 = np.zeros((length, LANES - QK_ROPE), f32)
    c = np.concatenate([cos, cos, z64], axis=1)
    s1 = np.concatenate([-sin, z32, z64], axis=1)
    s2 = np.concatenate([z32, sin, z64], axis=1)
    return c, s1, s2


def kernel(x, meta_tokens, norm_mix_w, w_in, q_norm_w, w_uq, kv_norm_w, w_ukv, w_o_mla, pool_w,
           pool_scale, w_pool_out, w_out, norm_ffn_w, w_router_group, b_router_group,
           w_router_expert, b_router_expert, w_exp_gate, w_exp_up, w_exp_down, final_norm_w):
    assert x.shape == (1, 8192, D_MODEL) and norm_mix_w.shape[0] == 1
    n = x.shape[1]
    xr = x[0]

    w_in_t = jnp.swapaxes(w_in[0], 0, 1).astype(BF16)
    w_q = jnp.pad(w_uq[0].reshape(Q_LORA, N_HEADS, QK_HEAD),
                  ((0, 0), (0, 0), (0, QK_PAD - QK_HEAD))).reshape(Q_LORA, N_HEADS * QK_PAD).astype(BF16)
    w_kv = w_ukv[0].reshape(KV_LORA, N_HEADS, QK_NOPE + V_HEAD)
    w_k = w_kv[:, :, :QK_NOPE].reshape(KV_LORA, N_HEADS * QK_NOPE).astype(BF16)
    w_v = w_kv[:, :, QK_NOPE:].reshape(KV_LORA, N_HEADS * V_HEAD).astype(BF16)
    w_router = jnp.pad(jnp.concatenate([w_router_group[0], w_router_expert[0]], axis=1),
                       ((0, 0), (0, LANES - N_GROUPS - N_EXPERTS))).astype(BF16)
    b_router = jnp.pad(jnp.concatenate([b_router_group[0], b_router_expert[0]]),
                       (0, LANES - N_GROUPS - N_EXPERTS))[None]
    c, s1, s2 = _rope_tables(N_META + n)

    proj, krope = _inproj(xr, norm_mix_w, w_in_t, tm=1024, tn=1024)
    proj_m, krope_m = _inproj(meta_tokens, norm_mix_w, w_in_t, tm=N_META, tn=1024)
    q = _qproj(proj, q_norm_w, w_q, c[N_META:], s1[N_META:], s2[N_META:], tm=512)
    k, v = _kvproj(proj, krope, kv_norm_w, w_k, w_v, c[N_META:], s1[N_META:], s2[N_META:], tm=512)
    k_m, v_m = _kvproj(proj_m, krope_m, kv_norm_w, w_k, w_v, c[:N_META], s1[:N_META], s2[:N_META],
                       tm=N_META)
    o = _attention(q, k, v, k_m, v_m, t=512)

    merged = _merge(o, proj, proj_m, w_o_mla[0], pool_w[0], pool_scale, w_pool_out[0], tm=256)
    h, bn, ids, wts = _route(merged, xr, w_out[0], norm_ffn_w, w_router, b_router, tm=512)

    be, rows, base, tok, dst = _route_blocks(ids[:TOP_K].T.reshape(n * TOP_K))
    wts = wts[:TOP_K].T
    y = _experts(be, rows, base, tok, dst, bn, w_exp_gate[0], w_exp_up[0], w_exp_down[0])

    out = _final(h, y, wts, final_norm_w[None], tm=512)
    return out[None]
```

```python
import functools
import math

import jax
import jax.numpy as jnp
import numpy as np
from jax import lax
from jax.experimental import pallas as pl
from jax.experimental.pallas import tpu as pltpu

F32 = jnp.float32
BF16 = jnp.bfloat16

D_MODEL = 2048
N_META = 16
EPS = 1e-6
N_HEADS = 16
QK_NOPE = 128
QK_ROPE = 64
QK_HEAD = QK_NOPE + QK_ROPE
V_HEAD = 128
Q_LORA = 512
KV_LORA = 512
ROPE_THETA = 10000.0
POOL_WINDOWS = (2, 4, 8, 16)
POOL_WIDTH = 1024
POOL_GROUP_DIM = 256
N_GROUPS = 8
EXPERTS_PER_GROUP = 8
N_EXPERTS = 64
TOP_K = 2
EXPERT_FF = 512

LANES = 128
SUBLANES = 8
QK_PAD = 256
V_PAD = 256
MAIN_COLS = Q_LORA + KV_LORA + POOL_WIDTH + 2 * D_MODEL
COL_CQ, COL_CKV, COL_POOL, COL_GM, COL_GP = 0, 512, 1024, 2048, 4096
ATTN_HEADS = 2
ATTN_UNROLL = 8
ATTN_STRIP = 64
ROUTE_BLOCK = 288
NEG_BIG = -1e30

VMEM_LIMIT = 56 * 1024 * 1024


def _params(sem):
    return pltpu.CompilerParams(dimension_semantics=sem, vmem_limit_bytes=VMEM_LIMIT)


def _rms(x, w):
    return x * lax.rsqrt(jnp.mean(x * x, axis=-1, keepdims=True) + EPS) * w


def _rope128(v, c, s1, s2):
    return v * c + pltpu.roll(v, 96, 1) * s1 + pltpu.roll(v, 32, 1) * s2


def _inproj_kernel(x_ref, nw_ref, wa_ref, wr_ref, wb_ref, o_ref, kr_ref, a_sc):
    j = pl.program_id(1)
    nt = (((1,), (1,)), ((), ()))

    @pl.when(j == 0)
    def _():
        a = _rms(x_ref[...], nw_ref[...]).astype(BF16)
        a_sc[...] = a
        kr_ref[...] = lax.dot_general(a, wr_ref[...], nt, preferred_element_type=F32)
        o_ref[...] = lax.dot_general(a, wa_ref[...], nt, preferred_element_type=F32)

    @pl.when(j > 0)
    def _():
        o_ref[...] = lax.dot_general(a_sc[...], wb_ref[...], nt, preferred_element_type=F32)


def _inproj(x, norm_w, w_t, tm, tn, n_cols=None):
    m, k = x.shape
    n_a = Q_LORA + KV_LORA
    assert tn == n_a
    n = w_t.shape[0] - QK_ROPE if n_cols is None else n_cols
    return pl.pallas_call(
        _inproj_kernel,
        grid=(m // tm, n // tn),
        in_specs=[
            pl.BlockSpec((tm, k), lambda i, j: (i, 0)),
            pl.BlockSpec((1, k), lambda i, j: (0, 0)),
            pl.BlockSpec((tn, k), lambda i, j: (0, 0)),
            pl.BlockSpec((LANES, k), lambda i, j: (n_a // LANES, 0)),
            pl.BlockSpec((pl.Element(tn), pl.Element(k)),
                         lambda i, j: (pl.multiple_of(n_a + QK_ROPE + jnp.maximum(j - 1, 0) * tn, QK_ROPE), 0)),
        ],
        out_specs=[
            pl.BlockSpec((tm, tn), lambda i, j: (i, j)),
            pl.BlockSpec((tm, LANES), lambda i, j: (i, 0)),
        ],
        out_shape=[jax.ShapeDtypeStruct((m, n), F32), jax.ShapeDtypeStruct((m, LANES), F32)],
        scratch_shapes=[pltpu.VMEM((tm, k), BF16)],
        compiler_params=_params(("parallel", "arbitrary")),
        name="inproj",
    )(x, norm_w, w_t, w_t, w_t)


def _qproj_kernel(cq_ref, nw_ref, w_ref, c_ref, s1_ref, s2_ref, q_ref, *, scale):
    a = _rms(cq_ref[...], nw_ref[...]).astype(BF16)
    c, s1, s2 = c_ref[...], s1_ref[...], s2_ref[...]
    for h in range(N_HEADS):
        qh = jnp.dot(a, w_ref[:, h * QK_PAD:(h + 1) * QK_PAD], preferred_element_type=F32) * scale
        q_ref[:, h * QK_PAD:h * QK_PAD + LANES] = qh[:, :LANES].astype(BF16)
        q_ref[:, h * QK_PAD + LANES:(h + 1) * QK_PAD] = _rope128(qh[:, LANES:], c, s1, s2).astype(BF16)


def _qproj(proj, norm_w, w_q, c, s1, s2, tm):
    m = proj.shape[0]
    n = N_HEADS * QK_PAD
    tab = pl.BlockSpec((tm, LANES), lambda i: (i, 0))
    return pl.pallas_call(
        functools.partial(_qproj_kernel, scale=QK_HEAD ** -0.5 * math.log2(math.e)),
        grid=(m // tm,),
        in_specs=[
            pl.BlockSpec((tm, Q_LORA), lambda i: (i, COL_CQ // Q_LORA)),
            pl.BlockSpec((1, Q_LORA), lambda i: (0, 0)),
            pl.BlockSpec((Q_LORA, n), lambda i: (0, 0)),
            tab, tab, tab,
        ],
        out_specs=pl.BlockSpec((tm, n), lambda i: (i, 0)),
        out_shape=jax.ShapeDtypeStruct((m, n), BF16),
        compiler_params=_params(("parallel",)),
        name="qproj",
    )(proj, norm_w, w_q, c, s1, s2)


def _kvproj_kernel(ckv_ref, kr_ref, nw_ref, wk_ref, wv_ref, c_ref, s1_ref, s2_ref, k_ref, v_ref):
    a = _rms(ckv_ref[...], nw_ref[...]).astype(BF16)
    kpe = _rope128(kr_ref[...], c_ref[...], s1_ref[...], s2_ref[...]).astype(BF16)
    ones = jnp.ones((a.shape[0], V_PAD - V_HEAD), BF16)
    for h2 in range(N_HEADS // 2):
        kn = jnp.dot(a, wk_ref[:, h2 * 256:(h2 + 1) * 256], preferred_element_type=F32).astype(BF16)
        vv = jnp.dot(a, wv_ref[:, h2 * 256:(h2 + 1) * 256], preferred_element_type=F32).astype(BF16)
        for d in range(2):
            h = 2 * h2 + d
            k_ref[:, h * QK_PAD:h * QK_PAD + LANES] = kn[:, d * LANES:(d + 1) * LANES]
            k_ref[:, h * QK_PAD + LANES:(h + 1) * QK_PAD] = kpe
            v_ref[:, h * V_PAD:h * V_PAD + V_HEAD] = vv[:, d * LANES:(d + 1) * LANES]
            v_ref[:, h * V_PAD + V_HEAD:(h + 1) * V_PAD] = ones


def _kvproj(proj, krope, norm_w, w_k, w_v, c, s1, s2, tm):
    m = proj.shape[0]
    tab = pl.BlockSpec((tm, LANES), lambda i: (i, 0))
    return pl.pallas_call(
        _kvproj_kernel,
        grid=(m // tm,),
        in_specs=[
            pl.BlockSpec((tm, KV_LORA), lambda i: (i, COL_CKV // KV_LORA)),
            tab,
            pl.BlockSpec((1, KV_LORA), lambda i: (0, 0)),
            pl.BlockSpec((KV_LORA, N_HEADS * QK_NOPE), lambda i: (0, 0)),
            pl.BlockSpec((KV_LORA, N_HEADS * V_HEAD), lambda i: (0, 0)),
            tab, tab, tab,
        ],
        out_specs=[
            pl.BlockSpec((tm, N_HEADS * QK_PAD), lambda i: (i, 0)),
            pl.BlockSpec((tm, N_HEADS * V_PAD), lambda i: (i, 0)),
        ],
        out_shape=[jax.ShapeDtypeStruct((m, N_HEADS * QK_PAD), BF16),
                   jax.ShapeDtypeStruct((m, N_HEADS * V_PAD), BF16)],
        compiler_params=_params(("parallel",)),
        name="kvproj",
    )(proj, krope, norm_w, w_k, w_v, c, s1, s2)


def _attn_kernel(q_ref, qn_ref, k_ref, v_ref, km_ref, vm_ref, o_ref,
                 acc_sc, acc0_sc, m_sc, s_sc, p_sc, al_sc, *, t):
    i = pl.program_id(1)
    heads = range(ATTN_HEADS)
    nt = (((1,), (1,)), ((), ()))
    qs = [q_ref[:, g * QK_PAD:(g + 1) * QK_PAD] for g in heads]

    def chunk(ref, g, width, c):
        return ref[pl.ds(pl.multiple_of(c * t, t), t), g * width:(g + 1) * width]

    def scores(g, c, slot, q=None):
        q = qs[g] if q is None else q
        s_sc[g, slot] = lax.dot_general(q, chunk(k_ref, g, QK_PAD, c), nt, preferred_element_type=F32)

    def first_scores(i_tile, q_of):
        for g in heads:
            scores(g, i_tile, 1, q_of(g))
            scores(g, 0, 0, q_of(g))

    def softmax(g, src, dst, masked):
        for r in range(t // ATTN_STRIP):
            rows = slice(r * ATTN_STRIP, (r + 1) * ATTN_STRIP)
            s = s_sc[g, src, rows, :]
            if masked:
                row = r * ATTN_STRIP + lax.broadcasted_iota(jnp.int32, s.shape, 0)
                s = jnp.where(lax.broadcasted_iota(jnp.int32, s.shape, 1) <= row, s, NEG_BIG)
            m_old = m_sc[g, rows, :]
            m_new = jnp.maximum(m_old, jnp.broadcast_to(jnp.max(s, axis=-1, keepdims=True), m_old.shape))
            p_sc[g, dst, rows, :] = jnp.concatenate(
                [jnp.exp2(s[:, c * LANES:(c + 1) * LANES] - m_new) for c in range(t // LANES)],
                axis=1).astype(BF16)
            al_sc[g, dst, rows, :] = jnp.exp2(m_old - m_new)
            m_sc[g, rows, :] = m_new

    def accumulate(g, slot, c):
        pv = jnp.dot(p_sc[g, slot], chunk(v_ref, g, V_PAD, c), preferred_element_type=F32)
        alpha = al_sc[g, slot]
        acc_sc[g] = jnp.concatenate([alpha, alpha], axis=1) * acc_sc[g] + pv

    def stage(k, src, with_next=True):
        for g in heads:
            if with_next:
                scores(g, jnp.minimum(k + 1, i - 1), 1 - src)
            accumulate(g, src, jnp.where(k == 0, i, k - 1))
            softmax(g, src, 1 - src, masked=False)

    def tile_start(i_tile, q_of):
        first_scores(i_tile, q_of)
        for g in heads:
            s = lax.dot_general(q_of(g), km_ref[:, g * QK_PAD:(g + 1) * QK_PAD], nt, preferred_element_type=F32)
            m0 = jnp.max(s, axis=-1, keepdims=True)
            acc0_sc[g] = jnp.dot(jnp.exp2(s - m0).astype(BF16), vm_ref[:, g * V_PAD:(g + 1) * V_PAD],
                                 preferred_element_type=F32)
            m_sc[g] = jnp.broadcast_to(m0, (t, LANES))
            softmax(g, 1, 0, masked=True)

    def tile_end(last_slot, last_chunk):
        for g in heads:
            accumulate(g, last_slot, last_chunk)
            acc = acc_sc[g]
            o_ref[:, g * V_HEAD:(g + 1) * V_HEAD] = (acc[:, :V_HEAD] / acc[:, V_HEAD:]).astype(o_ref.dtype)
        tile_start(jnp.minimum(i + 1, pl.num_programs(1) - 1),
                   lambda g: qn_ref[:, g * QK_PAD:(g + 1) * QK_PAD])

    @pl.when(i == 0)
    def _():
        tile_start(i, lambda g: qs[g])

    for g in heads:
        acc_sc[g] = acc0_sc[g]

    def run_stages(first, count):
        for u in range(count):
            stage(first + u, u % 2)

    def trip(j, _):
        run_stages(ATTN_UNROLL * j, ATTN_UNROLL)
        return 0

    n_trips = lax.shift_right_logical(i, int(math.log2(ATTN_UNROLL)))
    lax.fori_loop(0, n_trips, trip, 0)
    done = n_trips * ATTN_UNROLL
    width = ATTN_UNROLL // 2
    while width >= 2:
        @pl.when((i & width) != 0)
        def _(done=done, width=width):
            run_stages(done, width)

        done = done + (i & width)
        width //= 2

    odd = (i & 1) == 1

    @pl.when(odd)
    def _():
        stage(i - 1, 0, with_next=False)
        tile_end(1, i - 1)

    @pl.when(jnp.logical_not(odd))
    def _():
        tile_end(0, jnp.maximum(i - 1, 0))


def _attention(q, k, v, k_meta, v_meta, t):
    n = q.shape[0]
    g = ATTN_HEADS
    return pl.pallas_call(
        functools.partial(_attn_kernel, t=t),
        grid=(N_HEADS // g, n // t),
        in_specs=[
            pl.BlockSpec((t, g * QK_PAD), lambda h, i: (i, h)),
            pl.BlockSpec((t, g * QK_PAD), lambda h, i: (jnp.minimum(i + 1, n // t - 1), h)),
            pl.BlockSpec((n, g * QK_PAD), lambda h, i: (0, h)),
            pl.BlockSpec((n, g * V_PAD), lambda h, i: (0, h)),
            pl.BlockSpec((N_META, g * QK_PAD), lambda h, i: (0, h)),
            pl.BlockSpec((N_META, g * V_PAD), lambda h, i: (0, h)),
        ],
        out_specs=pl.BlockSpec((t, g * V_HEAD), lambda h, i: (i, h)),
        out_shape=jax.ShapeDtypeStruct((n, N_HEADS * V_HEAD), BF16),
        scratch_shapes=[
            pltpu.VMEM((g, t, V_PAD), F32), pltpu.VMEM((g, t, V_PAD), F32), pltpu.VMEM((g, t, LANES), F32),
            pltpu.VMEM((g, 2, t, t), F32), pltpu.VMEM((g, 2, t, t), BF16),
            pltpu.VMEM((g, 2, t, LANES), F32),
        ],
        compiler_params=_params(("parallel", "arbitrary")),
        name="attention",
    )(q, q, k, v, k_meta, v_meta)


def _merge_kernel(o_ref, u_ref, halo_ref, um_ref, gm_ref, gp_ref, wo_ref, pw_ref, ps_ref, wpo_ref,
                  out_ref, ext_sc, *, tm):
    i = pl.program_id(0)
    ext_sc[0:N_META, :] = jnp.where(i == 0, um_ref[...], halo_ref[...])
    ext_sc[N_META:, :] = u_ref[...]
    y_pool = jnp.zeros((tm, D_MODEL), F32)
    for g, w in enumerate(POOL_WINDOWS):
        lo, hi = g * POOL_GROUP_DIM, (g + 1) * POOL_GROUP_DIM
        u = ext_sc[N_META:, lo:hi]
        tot = u
        for d in range(1, w):
            tot = tot + ext_sc[N_META - d:N_META - d + tm, lo:hi]
        pooled = tot * (1.0 / w) - u
        mixed = jnp.dot(pooled, pw_ref[g], preferred_element_type=F32) * ps_ref[:, lo:hi]
        y_pool = y_pool + jnp.dot(mixed, wpo_ref[lo:hi, :], preferred_element_type=F32)
    y_mla = jnp.dot(o_ref[...].astype(F32), wo_ref[...], preferred_element_type=F32)
    merged = jax.nn.sigmoid(gm_ref[...]) * y_mla + jax.nn.sigmoid(gp_ref[...]) * y_pool
    out_ref[...] = merged.astype(out_ref.dtype)


def _merge(o, proj, u_meta, w_o, pool_w, pool_scale, w_pool_out, tm):
    n = o.shape[0]
    hb = tm // N_META
    const = lambda i: (0, 0)
    return pl.pallas_call(
        functools.partial(_merge_kernel, tm=tm),
        grid=(n // tm,),
        in_specs=[
            pl.BlockSpec((tm, D_MODEL), lambda i: (i, 0)),
            pl.BlockSpec((tm, POOL_WIDTH), lambda i: (i, COL_POOL // POOL_WIDTH)),
            pl.BlockSpec((N_META, POOL_WIDTH),
                         lambda i: (jnp.maximum(i * hb - 1, 0), COL_POOL // POOL_WIDTH)),
            pl.BlockSpec((N_META, POOL_WIDTH), lambda i: (0, COL_POOL // POOL_WIDTH)),
            pl.BlockSpec((tm, D_MODEL), lambda i: (i, COL_GM // D_MODEL)),
            pl.BlockSpec((tm, D_MODEL), lambda i: (i, COL_GP // D_MODEL)),
            pl.BlockSpec((D_MODEL, D_MODEL), const, pipeline_mode=pl.Buffered(1)),
            pl.BlockSpec((len(POOL_WINDOWS), POOL_GROUP_DIM, POOL_GROUP_DIM), lambda i: (0, 0, 0),
                         pipeline_mode=pl.Buffered(1)),
            pl.BlockSpec((1, POOL_WIDTH), const),
            pl.BlockSpec((POOL_WIDTH, D_MODEL), const, pipeline_mode=pl.Buffered(1)),
        ],
        out_specs=pl.BlockSpec((tm, D_MODEL), lambda i: (i, 0)),
        out_shape=jax.ShapeDtypeStruct((n, D_MODEL), BF16),
        scratch_shapes=[pltpu.VMEM((tm + N_META, POOL_WIDTH), F32)],
        compiler_params=_params(("arbitrary",)),
        name="merge",
    )(o, proj, proj, u_meta, proj, proj, w_o, pool_w, pool_scale, w_pool_out)


def _route_kernel(mg_ref, x_ref, wout_ref, nw_ref, wr_ref, br_ref, h_ref, bn_ref, ids_ref, wts_ref):
    h = x_ref[...] + jnp.dot(mg_ref[...].astype(F32), wout_ref[...], preferred_element_type=F32)
    h_ref[...] = h
    bn = _rms(h, nw_ref[...])
    bn_ref[...] = bn
    logits = jnp.dot(bn.astype(BF16), wr_ref[...], preferred_element_type=F32) + br_ref[...]
    lt = logits.T
    tm = lt.shape[1]
    sub = lax.broadcasted_iota(jnp.int32, (SUBLANES, tm), 0)
    sub_f = sub.astype(F32)

    def first_max(vals):
        vmax = jnp.max(vals, axis=0, keepdims=True)
        idx = jnp.min(jnp.where(vals == vmax, sub_f, float(SUBLANES)), axis=0, keepdims=True)
        return vmax, idx.astype(jnp.int32)

    glog = lt[:N_GROUPS, :]
    gmax, gidx = first_max(glog)
    p_g = 1.0 / jnp.sum(jnp.exp(glog - gmax), axis=0, keepdims=True)
    e_in = jnp.zeros((EXPERTS_PER_GROUP, tm), F32)
    for g in range(N_GROUPS):
        lo = N_GROUPS + g * EXPERTS_PER_GROUP
        e_in = jnp.where(gidx == g, lt[lo:lo + EXPERTS_PER_GROUP, :], e_in)
    v1, i1 = first_max(e_in)
    v2, i2 = first_max(jnp.where(sub == i1, -jnp.inf, e_in))
    e2 = jnp.exp(v2 - v1)
    den = 1.0 + e2
    w1 = p_g * (1.0 / den)
    w2 = p_g * (e2 / den)
    base = gidx * EXPERTS_PER_GROUP
    ids_ref[...] = jnp.where(sub == 0, base + i1, jnp.where(sub == 1, base + i2, 0))
    wts_ref[...] = jnp.where(sub == 0, w1, jnp.where(sub == 1, w2, 0.0))


def _route(merged, x, w_out, norm_w, w_router, b_router, tm):
    n = x.shape[0]
    const = lambda i: (0, 0)
    row = lambda width: pl.BlockSpec((tm, width), lambda i: (i, 0))
    return pl.pallas_call(
        _route_kernel,
        grid=(n // tm,),
        in_specs=[
            row(D_MODEL), row(D_MODEL),
            pl.BlockSpec((D_MODEL, D_MODEL), const, pipeline_mode=pl.Buffered(1)),
            pl.BlockSpec((1, D_MODEL), const),
            pl.BlockSpec((D_MODEL, LANES), const),
            pl.BlockSpec((1, LANES), const),
        ],
        out_specs=[row(D_MODEL), row(D_MODEL),
                   pl.BlockSpec((SUBLANES, tm), lambda i: (0, i)), pl.BlockSpec((SUBLANES, tm), lambda i: (0, i))],
        out_shape=[jax.ShapeDtypeStruct((n, D_MODEL), F32), jax.ShapeDtypeStruct((n, D_MODEL), F32),
                   jax.ShapeDtypeStruct((SUBLANES, n), jnp.int32), jax.ShapeDtypeStruct((SUBLANES, n), F32)],
        compiler_params=_params(("parallel",)),
        name="route",
    )(merged, x, w_out, norm_w, w_router, b_router)


def _for_rows(n, fn):
    shift = int(math.log2(SUBLANES))
    n_groups = lax.shift_right_logical(n, shift)

    def group(g, _):
        for u in range(SUBLANES):
            fn(g, u)
        return 0

    def single(r, _):
        fn(n_groups, r)
        return 0

    lax.fori_loop(0, n_groups, group, 0)
    lax.fori_loop(0, n - n_groups * SUBLANES, single, 0)


def _experts_kernel(be_ref, nv_ref, sb_ref, tok_ref, dst_ref, bn_hbm, wg_ref, wu_ref, wd_ref, y_hbm,
                    xbuf, ybuf, gsem, ssem):
    b = pl.program_id(0)
    nb = pl.num_programs(0)
    slot = lax.rem(b, 2)

    def gather(blk, slt):
        base = sb_ref[blk]

        def copy(g, u):
            tok = tok_ref[base + g * SUBLANES + u]
            return pltpu.make_async_copy(bn_hbm.at[pl.ds(tok, 1)], xbuf.at[slt, g, pl.ds(u, 1)],
                                         gsem.at[slt])
        return copy

    def scatter(blk, slt):
        base = sb_ref[blk]

        def copy(g, u):
            dst = dst_ref[base + g * SUBLANES + u]
            return pltpu.make_async_copy(ybuf.at[slt, g, pl.ds(u, 1)], y_hbm.at[pl.ds(dst, 1)],
                                         ssem.at[slt])
        return copy

    def gather_start(blk, slt):
        copy = gather(blk, slt)
        _for_rows(nv_ref[blk], lambda g, u: copy(g, u).start())

    def scatter_wait(blk, slt):
        copy = scatter(blk, slt)
        _for_rows(nv_ref[blk], lambda g, u: copy(g, u).wait())

    @pl.when(b == 0)
    def _():
        xbuf[...] = jnp.zeros(xbuf.shape, xbuf.dtype)
        gather_start(0, 0)

    @pl.when(b + 1 < nb)
    def _():
        gather_start(b + 1, 1 - slot)

    @pl.when(nv_ref[b] > 0)
    def _():
        wait_copy = gather(b, slot)
        _for_rows(nv_ref[b], lambda g, u: wait_copy(g, u).wait())
        x = xbuf[slot].reshape(ROUTE_BLOCK, D_MODEL)
        gate = jnp.dot(x, wg_ref[0], preferred_element_type=F32)
        up = jnp.dot(x, wu_ref[0], preferred_element_type=F32)
        hdn = gate * jax.nn.sigmoid(gate) * up
        y = jnp.dot(hdn, wd_ref[0], preferred_element_type=F32)
        ybuf[slot] = y.reshape(ROUTE_BLOCK // SUBLANES, SUBLANES, D_MODEL)
        start_copy = scatter(b, slot)
        _for_rows(nv_ref[b], lambda g, u: start_copy(g, u).start())

    @pl.when(b >= 1)
    def _():
        scatter_wait(b - 1, 1 - slot)

    @pl.when(b == nb - 1)
    def _():
        scatter_wait(b, slot)


def _experts(block_expert, block_rows, block_base, tok, dst, bn, w_gate, w_up, w_down):
    n = bn.shape[0]
    wmap = lambda b, be, nv, sb, tk, ds: (be[b], 0, 0)
    groups = ROUTE_BLOCK // SUBLANES
    grid_spec = pltpu.PrefetchScalarGridSpec(
        num_scalar_prefetch=5,
        grid=(block_expert.shape[0],),
        in_specs=[
            pl.BlockSpec(memory_space=pl.ANY),
            pl.BlockSpec((1, D_MODEL, EXPERT_FF), wmap),
            pl.BlockSpec((1, D_MODEL, EXPERT_FF), wmap),
            pl.BlockSpec((1, EXPERT_FF, D_MODEL), wmap),
        ],
        out_specs=pl.BlockSpec(memory_space=pl.ANY),
        scratch_shapes=[
            pltpu.VMEM((2, groups, SUBLANES, D_MODEL), F32),
            pltpu.VMEM((2, groups, SUBLANES, D_MODEL), F32),
            pltpu.SemaphoreType.DMA((2,)),
            pltpu.SemaphoreType.DMA((2,)),
        ],
    )
    return pl.pallas_call(
        _experts_kernel,
        grid_spec=grid_spec,
        out_shape=jax.ShapeDtypeStruct((TOP_K * n, D_MODEL), F32),
        compiler_params=_params(("arbitrary",)),
        name="experts",
    )(block_expert, block_rows, block_base, tok, dst, bn, w_gate, w_up, w_down)


def _route_blocks(flat_e):
    a = flat_e.shape[0]
    i32 = jnp.int32
    order = jnp.argsort(flat_e).astype(i32)
    counts = jnp.sum(flat_e[:, None] == jnp.arange(N_EXPERTS, dtype=i32)[None, :], axis=0, dtype=i32)
    padded = (counts + ROUTE_BLOCK - 1) // ROUTE_BLOCK * ROUTE_BLOCK
    pad_end = jnp.cumsum(padded).astype(i32)
    pad_start = pad_end - padded
    seg_start = jnp.cumsum(counts).astype(i32) - counts
    n_blocks = -(-(a + N_EXPERTS * (ROUTE_BLOCK - 1)) // ROUTE_BLOCK)
    blk = jnp.arange(n_blocks, dtype=i32)
    be = jnp.sum(pad_end[None, :] <= (blk * ROUTE_BLOCK)[:, None], axis=1, dtype=i32)
    be = jnp.minimum(be, N_EXPERTS - 1)
    off = blk * ROUTE_BLOCK - pad_start[be]
    rows = jnp.clip(counts[be] - off, 0, ROUTE_BLOCK).astype(i32)
    base = jnp.where(rows > 0, seg_start[be] + off, 0).astype(i32)
    n_used = pad_end[-1] // ROUTE_BLOCK
    be = jnp.where(blk < n_used, be, be[jnp.maximum(n_used - 1, 0)])
    tok = order // TOP_K
    dst = (order % TOP_K) * (a // TOP_K) + tok
    return be, rows, base, tok, dst


def _final_kernel(h_ref, y0_ref, y1_ref, wts_ref, nw_ref, o_ref):
    w = wts_ref[...]
    moe = y0_ref[...] * w[:, 0:1] + y1_ref[...] * w[:, 1:2]
    o_ref[...] = _rms(h_ref[...] + moe, nw_ref[...])


def _final(h, y, wts, norm_w, tm):
    n = h.shape[0]
    return pl.pallas_call(
        _final_kernel,
        grid=(n // tm,),
        in_specs=[
            pl.BlockSpec((tm, D_MODEL), lambda i: (i, 0)),
            pl.BlockSpec((tm, D_MODEL), lambda i: (i, 0)),
            pl.BlockSpec((tm, D_MODEL), lambda i: (i + n // tm, 0)),
            pl.BlockSpec((tm, TOP_K), lambda i: (i, 0)),
            pl.BlockSpec((1, D_MODEL), lambda i: (0, 0)),
        ],
        out_specs=pl.BlockSpec((tm, D_MODEL), lambda i: (i, 0)),
        out_shape=jax.ShapeDtypeStruct((n, D_MODEL), F32),
        compiler_params=_params(("parallel",)),
        name="final",
    )(h, y, y, wts, norm_w)


def _rope_tables(length):
    f32 = np.float32
    inv = (f32(1.0) / np.power(f32(ROPE_THETA), np.arange(0, QK_ROPE, 2, dtype=f32) / f32(QK_ROPE))).astype(f32)
    ang = (np.arange(length, dtype=f32)[:, None] * inv[None, :]).astype(f32)
    cos, sin = np.cos(ang).astype(f32), np.sin(ang).astype(f32)
    z32 = np.zeros_like(cos)
    z64 = np.zeros((length, LANES - QK_ROPE), f32)
    c = np.concatenate([cos, cos, z64], axis=1)
    s1 = np.concatenate([-sin, z32, z64], axis=1)
    s2 = np.concatenate([z32, sin, z64], axis=1)
    return c, s1, s2


def kernel(x, meta_tokens, norm_mix_w, w_in, q_norm_w, w_uq, kv_norm_w, w_ukv, w_o_mla, pool_w,
           pool_scale, w_pool_out, w_out, norm_ffn_w, w_router_group, b_router_group,
           w_router_expert, b_router_expert, w_exp_gate, w_exp_up, w_exp_down, final_norm_w):
    assert x.shape == (1, 8192, D_MODEL) and norm_mix_w.shape[0] == 1
    n = x.shape[1]
    xr = x[0]

    w_in_t = jnp.swapaxes(w_in[0], 0, 1).astype(BF16)
    w_q = jnp.pad(w_uq[0].reshape(Q_LORA, N_HEADS, QK_HEAD),
                  ((0, 0), (0, 0), (0, QK_PAD - QK_HEAD))).reshape(Q_LORA, N_HEADS * QK_PAD).astype(BF16)
    w_kv = w_ukv[0].reshape(KV_LORA, N_HEADS, QK_NOPE + V_HEAD)
    w_k = w_kv[:, :, :QK_NOPE].reshape(KV_LORA, N_HEADS * QK_NOPE).astype(BF16)
    w_v = w_kv[:, :, QK_NOPE:].reshape(KV_LORA, N_HEADS * V_HEAD).astype(BF16)
    w_router = jnp.pad(jnp.concatenate([w_router_group[0], w_router_expert[0]], axis=1),
                       ((0, 0), (0, LANES - N_GROUPS - N_EXPERTS))).astype(BF16)
    b_router = jnp.pad(jnp.concatenate([b_router_group[0], b_router_expert[0]]),
                       (0, LANES - N_GROUPS - N_EXPERTS))[None]
    c, s1, s2 = _rope_tables(N_META + n)

    proj, krope = _inproj(xr, norm_mix_w, w_in_t, tm=1024, tn=1024)
    proj_m, krope_m = _inproj(meta_tokens, norm_mix_w, w_in_t, tm=N_META, tn=1024, n_cols=COL_GM)
    q = _qproj(proj, q_norm_w, w_q, c[N_META:], s1[N_META:], s2[N_META:], tm=512)
    k, v = _kvproj(proj, krope, kv_norm_w, w_k, w_v, c[N_META:], s1[N_META:], s2[N_META:], tm=512)
    k_m, v_m = _kvproj(proj_m, krope_m, kv_norm_w, w_k, w_v, c[:N_META], s1[:N_META], s2[:N_META],
                       tm=N_META)
    o = _attention(q, k, v, k_m, v_m, t=512)

    merged = _merge(o, proj, proj_m, w_o_mla[0], pool_w[0], pool_scale, w_pool_out[0], tm=256)
    h, bn, ids, wts = _route(merged, xr, w_out[0], norm_ffn_w, w_router, b_router, tm=512)

    be, rows, base, tok, dst = _route_blocks(ids[:TOP_K].T.reshape(n * TOP_K))
    wts = wts[:TOP_K].T
    y = _experts(be, rows, base, tok, dst, bn, w_exp_gate[0], w_exp_up[0], w_exp_down[0])

    out = _final(h, y, wts, final_norm_w[None], tm=512)
    return out[None]
```

```python
import functools
import math

import jax
import jax.numpy as jnp
import numpy as np
from jax import lax
from jax.experimental import pallas as pl
from jax.experimental.pallas import tpu as pltpu

F32 = jnp.float32
BF16 = jnp.bfloat16

D_MODEL = 2048
N_META = 16
EPS = 1e-6
N_HEADS = 16
QK_NOPE = 128
QK_ROPE = 64
QK_HEAD = QK_NOPE + QK_ROPE
V_HEAD = 128
Q_LORA = 512
KV_LORA = 512
ROPE_THETA = 10000.0
POOL_WINDOWS = (2, 4, 8, 16)
POOL_WIDTH = 1024
POOL_GROUP_DIM = 256
N_GROUPS = 8
EXPERTS_PER_GROUP = 8
N_EXPERTS = 64
TOP_K = 2
EXPERT_FF = 512

LANES = 128
SUBLANES = 8
QK_PAD = 256
V_PAD = 256
COL_CQ, COL_CKV, COL_POOL, COL_GM, COL_GP = 0, 512, 1024, 2048, 4096
ATTN_HEADS = 2
ATTN_UNROLL = 8
ATTN_STRIP = 64
ROUTE_BLOCK = 288
NEG_BIG = -1e30

VMEM_LIMIT = 56 * 1024 * 1024


def _params(sem):
    return pltpu.CompilerParams(dimension_semantics=sem, vmem_limit_bytes=VMEM_LIMIT)


def _rms(x, w):
    return x * lax.rsqrt(jnp.mean(x * x, axis=-1, keepdims=True) + EPS) * w


def _rope128(v, c, s1, s2):
    return v * c + pltpu.roll(v, 96, 1) * s1 + pltpu.roll(v, 32, 1) * s2


def _inproj_kernel(x_ref, nw_ref, wa_ref, wr_ref, wb_ref, o_ref, kr_ref, a_sc):
    j = pl.program_id(1)
    nt = (((1,), (1,)), ((), ()))

    @pl.when(j == 0)
    def _():
        a = _rms(x_ref[...], nw_ref[...]).astype(BF16)
        a_sc[...] = a
        kr_ref[...] = lax.dot_general(a, wr_ref[...], nt, preferred_element_type=F32)
        o_ref[...] = lax.dot_general(a, wa_ref[...], nt, preferred_element_type=F32)

    @pl.when(j > 0)
    def _():
        o_ref[...] = lax.dot_general(a_sc[...], wb_ref[...], nt, preferred_element_type=F32)


def _inproj(x, norm_w, w_t, tm, tn, n_cols=None):
    m, k = x.shape
    n_a = Q_LORA + KV_LORA
    assert tn == n_a
    n = w_t.shape[0] - QK_ROPE if n_cols is None else n_cols
    return pl.pallas_call(
        _inproj_kernel,
        grid=(m // tm, n // tn),
        in_specs=[
            pl.BlockSpec((tm, k), lambda i, j: (i, 0)),
            pl.BlockSpec((1, k), lambda i, j: (0, 0)),
            pl.BlockSpec((tn, k), lambda i, j: (0, 0)),
            pl.BlockSpec((LANES, k), lambda i, j: (n_a // LANES, 0)),
            pl.BlockSpec((pl.Element(tn), pl.Element(k)),
                         lambda i, j: (pl.multiple_of(n_a + QK_ROPE + jnp.maximum(j - 1, 0) * tn, QK_ROPE), 0)),
        ],
        out_specs=[
            pl.BlockSpec((tm, tn), lambda i, j: (i, j)),
            pl.BlockSpec((tm, LANES), lambda i, j: (i, 0)),
        ],
        out_shape=[jax.ShapeDtypeStruct((m, n), F32), jax.ShapeDtypeStruct((m, LANES), F32)],
        scratch_shapes=[pltpu.VMEM((tm, k), BF16)],
        compiler_params=_params(("parallel", "arbitrary")),
        name="inproj",
    )(x, norm_w, w_t, w_t, w_t)


def _qproj_kernel(cq_ref, nw_ref, w_ref, c_ref, s1_ref, s2_ref, q_ref, *, scale):
    a = _rms(cq_ref[...], nw_ref[...]).astype(BF16)
    c, s1, s2 = c_ref[...], s1_ref[...], s2_ref[...]
    for h in range(N_HEADS):
        qh = jnp.dot(a, w_ref[:, h * QK_PAD:(h + 1) * QK_PAD], preferred_element_type=F32) * scale
        q_ref[:, h * QK_PAD:h * QK_PAD + LANES] = qh[:, :LANES].astype(BF16)
        q_ref[:, h * QK_PAD + LANES:(h + 1) * QK_PAD] = _rope128(qh[:, LANES:], c, s1, s2).astype(BF16)


def _qproj(proj, norm_w, w_q, c, s1, s2, tm):
    m = proj.shape[0]
    n = N_HEADS * QK_PAD
    tab = pl.BlockSpec((tm, LANES), lambda i: (i, 0))
    return pl.pallas_call(
        functools.partial(_qproj_kernel, scale=QK_HEAD ** -0.5 * math.log2(math.e)),
        grid=(m // tm,),
        in_specs=[
            pl.BlockSpec((tm, Q_LORA), lambda i: (i, COL_CQ // Q_LORA)),
            pl.BlockSpec((1, Q_LORA), lambda i: (0, 0)),
            pl.BlockSpec((Q_LORA, n), lambda i: (0, 0)),
            tab, tab, tab,
        ],
        out_specs=pl.BlockSpec((tm, n), lambda i: (i, 0)),
        out_shape=jax.ShapeDtypeStruct((m, n), BF16),
        compiler_params=_params(("parallel",)),
        name="qproj",
    )(proj, norm_w, w_q, c, s1, s2)


def _kvproj_kernel(ckv_ref, kr_ref, nw_ref, wk_ref, wv_ref, c_ref, s1_ref, s2_ref, k_ref, v_ref):
    a = _rms(ckv_ref[...], nw_ref[...]).astype(BF16)
    kpe = _rope128(kr_ref[...], c_ref[...], s1_ref[...], s2_ref[...]).astype(BF16)
    ones = jnp.ones((a.shape[0], V_PAD - V_HEAD), BF16)
    for h2 in range(N_HEADS // 2):
        kn = jnp.dot(a, wk_ref[:, h2 * 256:(h2 + 1) * 256], preferred_element_type=F32).astype(BF16)
        vv = jnp.dot(a, wv_ref[:, h2 * 256:(h2 + 1) * 256], preferred_element_type=F32).astype(BF16)
        for d in range(2):
            h = 2 * h2 + d
            k_ref[:, h * QK_PAD:h * QK_PAD + LANES] = kn[:, d * LANES:(d + 1) * LANES]
            k_ref[:, h * QK_PAD + LANES:(h + 1) * QK_PAD] = kpe
            v_ref[:, h * V_PAD:h * V_PAD + V_HEAD] = vv[:, d * LANES:(d + 1) * LANES]
            v_ref[:, h * V_PAD + V_HEAD:(h + 1) * V_PAD] = ones


def _kvproj(proj, krope, norm_w, w_k, w_v, c, s1, s2, tm):
    m = proj.shape[0]
    tab = pl.BlockSpec((tm, LANES), lambda i: (i, 0))
    return pl.pallas_call(
        _kvproj_kernel,
        grid=(m // tm,),
        in_specs=[
            pl.BlockSpec((tm, KV_LORA), lambda i: (i, COL_CKV // KV_LORA)),
            tab,
            pl.BlockSpec((1, KV_LORA), lambda i: (0, 0)),
            pl.BlockSpec((KV_LORA, N_HEADS * QK_NOPE), lambda i: (0, 0)),
            pl.BlockSpec((KV_LORA, N_HEADS * V_HEAD), lambda i: (0, 0)),
            tab, tab, tab,
        ],
        out_specs=[
            pl.BlockSpec((tm, N_HEADS * QK_PAD), lambda i: (i, 0)),
            pl.BlockSpec((tm, N_HEADS * V_PAD), lambda i: (i, 0)),
        ],
        out_shape=[jax.ShapeDtypeStruct((m, N_HEADS * QK_PAD), BF16),
                   jax.ShapeDtypeStruct((m, N_HEADS * V_PAD), BF16)],
        compiler_params=_params(("parallel",)),
        name="kvproj",
    )(proj, krope, norm_w, w_k, w_v, c, s1, s2)


def _attn_kernel(q_ref, qn_ref, k_ref, v_ref, km_ref, vm_ref, o_ref,
                 acc_sc, acc0_sc, m_sc, s_sc, p_sc, al_sc, *, t):
    i = pl.program_id(1)
    heads = range(ATTN_HEADS)
    nt = (((1,), (1,)), ((), ()))
    qs = [q_ref[:, g * QK_PAD:(g + 1) * QK_PAD] for g in heads]

    def chunk(ref, g, width, c):
        return ref[pl.ds(pl.multiple_of(c * t, t), t), g * width:(g + 1) * width]

    def scores(g, c, slot, q=None):
        q = qs[g] if q is None else q
        s_sc[g, slot] = lax.dot_general(q, chunk(k_ref, g, QK_PAD, c), nt, preferred_element_type=F32)

    def first_scores(i_tile, q_of):
        for g in heads:
            scores(g, i_tile, 1, q_of(g))
            scores(g, 0, 0, q_of(g))

    def softmax(g, src, dst, masked):
        for r in range(t // ATTN_STRIP):
            rows = slice(r * ATTN_STRIP, (r + 1) * ATTN_STRIP)
            s = s_sc[g, src, rows, :]
            if masked:
                row = r * ATTN_STRIP + lax.broadcasted_iota(jnp.int32, s.shape, 0)
                s = jnp.where(lax.broadcasted_iota(jnp.int32, s.shape, 1) <= row, s, NEG_BIG)
            m_old = m_sc[g, rows, :]
            m_new = jnp.maximum(m_old, jnp.broadcast_to(jnp.max(s, axis=-1, keepdims=True), m_old.shape))
            p_sc[g, dst, rows, :] = jnp.concatenate(
                [jnp.exp2(s[:, c * LANES:(c + 1) * LANES] - m_new) for c in range(t // LANES)],
                axis=1).astype(BF16)
            al_sc[g, dst, rows, :] = jnp.exp2(m_old - m_new)
            m_sc[g, rows, :] = m_new

    def accumulate(g, slot, c):
        pv = jnp.dot(p_sc[g, slot], chunk(v_ref, g, V_PAD, c), preferred_element_type=F32)
        alpha = al_sc[g, slot]
        acc_sc[g] = jnp.concatenate([alpha, alpha], axis=1) * acc_sc[g] + pv

    def stage(k, src, with_next=True):
        for g in heads:
            if with_next:
                scores(g, jnp.minimum(k + 1, i - 1), 1 - src)
            accumulate(g, src, jnp.where(k == 0, i, k - 1))
            softmax(g, src, 1 - src, masked=False)

    def tile_start(i_tile, q_of):
        first_scores(i_tile, q_of)
        for g in heads:
            s = lax.dot_general(q_of(g), km_ref[:, g * QK_PAD:(g + 1) * QK_PAD], nt, preferred_element_type=F32)
            m0 = jnp.max(s, axis=-1, keepdims=True)
            acc0_sc[g] = jnp.dot(jnp.exp2(s - m0).astype(BF16), vm_ref[:, g * V_PAD:(g + 1) * V_PAD],
                                 preferred_element_type=F32)
            m_sc[g] = jnp.broadcast_to(m0, (t, LANES))
            softmax(g, 1, 0, masked=True)

    def tile_end(last_slot, last_chunk):
        for g in heads:
            accumulate(g, last_slot, last_chunk)
            acc = acc_sc[g]
            o_ref[:, g * V_HEAD:(g + 1) * V_HEAD] = (acc[:, :V_HEAD] / acc[:, V_HEAD:]).astype(o_ref.dtype)
        tile_start(jnp.minimum(i + 1, pl.num_programs(1) - 1),
                   lambda g: qn_ref[:, g * QK_PAD:(g + 1) * QK_PAD])

    @pl.when(i == 0)
    def _():
        tile_start(i, lambda g: qs[g])

    for g in heads:
        acc_sc[g] = acc0_sc[g]

    def run_stages(first, count):
        for u in range(count):
            stage(first + u, u % 2)

    def trip(j, _):
        run_stages(ATTN_UNROLL * j, ATTN_UNROLL)
        return 0

    n_trips = lax.shift_right_logical(i, int(math.log2(ATTN_UNROLL)))
    lax.fori_loop(0, n_trips, trip, 0)
    done = n_trips * ATTN_UNROLL
    width = ATTN_UNROLL // 2
    while width >= 2:
        @pl.when((i & width) != 0)
        def _(done=done, width=width):
            run_stages(done, width)

        done = done + (i & width)
        width //= 2

    odd = (i & 1) == 1

    @pl.when(odd)
    def _():
        stage(i - 1, 0, with_next=False)
        tile_end(1, i - 1)

    @pl.when(jnp.logical_not(odd))
    def _():
        tile_end(0, jnp.maximum(i - 1, 0))


def _attention(q, k, v, k_meta, v_meta, t):
    n = q.shape[0]
    g = ATTN_HEADS
    return pl.pallas_call(
        functools.partial(_attn_kernel, t=t),
        grid=(N_HEADS // g, n // t),
        in_specs=[
            pl.BlockSpec((t, g * QK_PAD), lambda h, i: (i, h)),
            pl.BlockSpec((t, g * QK_PAD), lambda h, i: (jnp.minimum(i + 1, n // t - 1), h)),
            pl.BlockSpec((n, g * QK_PAD), lambda h, i: (0, h)),
            pl.BlockSpec((n, g * V_PAD), lambda h, i: (0, h)),
            pl.BlockSpec((N_META, g * QK_PAD), lambda h, i: (0, h)),
            pl.BlockSpec((N_META, g * V_PAD), lambda h, i: (0, h)),
        ],
        out_specs=pl.BlockSpec((t, g * V_HEAD), lambda h, i: (i, h)),
        out_shape=jax.ShapeDtypeStruct((n, N_HEADS * V_HEAD), BF16),
        scratch_shapes=[
            pltpu.VMEM((g, t, V_PAD), F32), pltpu.VMEM((g, t, V_PAD), F32), pltpu.VMEM((g, t, LANES), F32),
            pltpu.VMEM((g, 2, t, t), F32), pltpu.VMEM((g, 2, t, t), BF16),
            pltpu.VMEM((g, 2, t, LANES), F32),
        ],
        compiler_params=_params(("parallel", "arbitrary")),
        name="attention",
    )(q, q, k, v, k_meta, v_meta)


def _merge_kernel(o_ref, u_ref, halo_ref, um_ref, gm_ref, gp_ref, wo_ref, pw_ref, ps_ref, wpo_ref,
                  out_ref, ext_sc, *, tm):
    i = pl.program_id(0)
    ext_sc[0:N_META, :] = jnp.where(i == 0, um_ref[...], halo_ref[...])
    ext_sc[N_META:, :] = u_ref[...]
    y_pool = jnp.zeros((tm, D_MODEL), F32)
    for g, w in enumerate(POOL_WINDOWS):
        lo, hi = g * POOL_GROUP_DIM, (g + 1) * POOL_GROUP_DIM
        u = ext_sc[N_META:, lo:hi]
        tot = u
        for d in range(1, w):
            tot = tot + ext_sc[N_META - d:N_META - d + tm, lo:hi]
        pooled = tot * (1.0 / w) - u
        mixed = jnp.dot(pooled, pw_ref[g], preferred_element_type=F32) * ps_ref[:, lo:hi]
        y_pool = y_pool + jnp.dot(mixed, wpo_ref[lo:hi, :], preferred_element_type=F32)
    y_mla = jnp.dot(o_ref[...].astype(F32), wo_ref[...], preferred_element_type=F32)
    merged = jax.nn.sigmoid(gm_ref[...]) * y_mla + jax.nn.sigmoid(gp_ref[...]) * y_pool
    out_ref[...] = merged.astype(out_ref.dtype)


def _merge(o, proj, u_meta, w_o, pool_w, pool_scale, w_pool_out, tm):
    n = o.shape[0]
    hb = tm // N_META
    const = lambda i: (0, 0)
    return pl.pallas_call(
        functools.partial(_merge_kernel, tm=tm),
        grid=(n // tm,),
        in_specs=[
            pl.BlockSpec((tm, D_MODEL), lambda i: (i, 0)),
            pl.BlockSpec((tm, POOL_WIDTH), lambda i: (i, COL_POOL // POOL_WIDTH)),
            pl.BlockSpec((N_META, POOL_WIDTH),
                         lambda i: (jnp.maximum(i * hb - 1, 0), COL_POOL // POOL_WIDTH)),
            pl.BlockSpec((N_META, POOL_WIDTH), lambda i: (0, COL_POOL // POOL_WIDTH)),
            pl.BlockSpec((tm, D_MODEL), lambda i: (i, COL_GM // D_MODEL)),
            pl.BlockSpec((tm, D_MODEL), lambda i: (i, COL_GP // D_MODEL)),
            pl.BlockSpec((D_MODEL, D_MODEL), const, pipeline_mode=pl.Buffered(1)),
            pl.BlockSpec((len(POOL_WINDOWS), POOL_GROUP_DIM, POOL_GROUP_DIM), lambda i: (0, 0, 0),
                         pipeline_mode=pl.Buffered(1)),
            pl.BlockSpec((1, POOL_WIDTH), const),
            pl.BlockSpec((POOL_WIDTH, D_MODEL), const, pipeline_mode=pl.Buffered(1)),
        ],
        out_specs=pl.BlockSpec((tm, D_MODEL), lambda i: (i, 0)),
        out_shape=jax.ShapeDtypeStruct((n, D_MODEL), BF16),
        scratch_shapes=[pltpu.VMEM((tm + N_META, POOL_WIDTH), F32)],
        compiler_params=_params(("arbitrary",)),
        name="merge",
    )(o, proj, proj, u_meta, proj, proj, w_o, pool_w, pool_scale, w_pool_out)


def _route_kernel(mg_ref, x_ref, wout_ref, nw_ref, wr_ref, br_ref, h_ref, bn_ref, ids_ref, wts_ref):
    h = x_ref[...] + jnp.dot(mg_ref[...].astype(F32), wout_ref[...], preferred_element_type=F32)
    h_ref[...] = h
    bn = _rms(h, nw_ref[...])
    bn_ref[...] = bn
    logits = jnp.dot(bn.astype(BF16), wr_ref[...], preferred_element_type=F32) + br_ref[...]
    lt = logits.T
    tm = lt.shape[1]
    sub = lax.broadcasted_iota(jnp.int32, (SUBLANES, tm), 0)
    sub_f = sub.astype(F32)

    def first_max(vals):
        vmax = jnp.max(vals, axis=0, keepdims=True)
        idx = jnp.min(jnp.where(vals == vmax, sub_f, float(SUBLANES)), axis=0, keepdims=True)
        return vmax, idx.astype(jnp.int32)

    glog = lt[:N_GROUPS, :]
    gmax, gidx = first_max(glog)
    p_g = 1.0 / jnp.sum(jnp.exp(glog - gmax), axis=0, keepdims=True)
    e_in = jnp.zeros((EXPERTS_PER_GROUP, tm), F32)
    for g in range(N_GROUPS):
        lo = N_GROUPS + g * EXPERTS_PER_GROUP
        e_in = jnp.where(gidx == g, lt[lo:lo + EXPERTS_PER_GROUP, :], e_in)
    v1, i1 = first_max(e_in)
    v2, i2 = first_max(jnp.where(sub == i1, -jnp.inf, e_in))
    e2 = jnp.exp(v2 - v1)
    den = 1.0 + e2
    w1 = p_g * (1.0 / den)
    w2 = p_g * (e2 / den)
    base = gidx * EXPERTS_PER_GROUP
    ids_ref[...] = jnp.where(sub == 0, base + i1, jnp.where(sub == 1, base + i2, 0))
    wts_ref[...] = jnp.where(sub == 0, w1, jnp.where(sub == 1, w2, 0.0))


def _route(merged, x, w_out, norm_w, w_router, b_router, tm):
    n = x.shape[0]
    const = lambda i: (0, 0)
    row = lambda width: pl.BlockSpec((tm, width), lambda i: (i, 0))
    return pl.pallas_call(
        _route_kernel,
        grid=(n // tm,),
        in_specs=[
            row(D_MODEL), row(D_MODEL),
            pl.BlockSpec((D_MODEL, D_MODEL), const, pipeline_mode=pl.Buffered(1)),
            pl.BlockSpec((1, D_MODEL), const),
            pl.BlockSpec((D_MODEL, LANES), const),
            pl.BlockSpec((1, LANES), const),
        ],
        out_specs=[row(D_MODEL), row(D_MODEL),
                   pl.BlockSpec((SUBLANES, tm), lambda i: (0, i)), pl.BlockSpec((SUBLANES, tm), lambda i: (0, i))],
        out_shape=[jax.ShapeDtypeStruct((n, D_MODEL), F32), jax.ShapeDtypeStruct((n, D_MODEL), F32),
                   jax.ShapeDtypeStruct((SUBLANES, n), jnp.int32), jax.ShapeDtypeStruct((SUBLANES, n), F32)],
        compiler_params=_params(("parallel",)),
        name="route",
    )(merged, x, w_out, norm_w, w_router, b_router)


def _for_rows(n, fn):
    shift = int(math.log2(SUBLANES))
    n_groups = lax.shift_right_logical(n, shift)

    def group(g, _):
        for u in range(SUBLANES):
            fn(g, u)
        return 0

    def single(r, _):
        fn(n_groups, r)
        return 0

    lax.fori_loop(0, n_groups, group, 0)
    lax.fori_loop(0, n - n_groups * SUBLANES, single, 0)


def _experts_kernel(fb_ref, nv_ref, sb_ref, tok_ref, dst_ref, bn_hbm, wg_ref, wu_ref, wd_ref, y_hbm,
                    xbuf, ybuf, gsem, ssem):
    e = pl.program_id(0)
    n_experts = pl.num_programs(0)
    n_used = fb_ref[n_experts]

    def gather(blk, slt):
        base = sb_ref[blk]

        def copy(g, u):
            tok = tok_ref[base + g * SUBLANES + u]
            return pltpu.make_async_copy(bn_hbm.at[pl.ds(tok, 1)], xbuf.at[slt, g, pl.ds(u, 1)],
                                         gsem.at[slt])
        return copy

    def scatter(blk, slt):
        base = sb_ref[blk]

        def copy(g, u):
            dst = dst_ref[base + g * SUBLANES + u]
            return pltpu.make_async_copy(ybuf.at[slt, g, pl.ds(u, 1)], y_hbm.at[pl.ds(dst, 1)],
                                         ssem.at[slt])
        return copy

    def gather_start(blk, slt):
        copy = gather(blk, slt)
        _for_rows(nv_ref[blk], lambda g, u: copy(g, u).start())

    def scatter_wait(blk, slt):
        copy = scatter(blk, slt)
        _for_rows(nv_ref[blk], lambda g, u: copy(g, u).wait())

    @pl.when(e == 0)
    def _():
        xbuf[...] = jnp.zeros(xbuf.shape, xbuf.dtype)
        gather_start(0, 0)

    def block(b, _):
        slot = lax.rem(b, 2)

        @pl.when(b + 1 < n_used)
        def _():
            gather_start(b + 1, 1 - slot)

        wait_copy = gather(b, slot)
        _for_rows(nv_ref[b], lambda g, u: wait_copy(g, u).wait())
        x = xbuf[slot].reshape(ROUTE_BLOCK, D_MODEL)
        gate = jnp.dot(x, wg_ref[0], preferred_element_type=F32)
        up = jnp.dot(x, wu_ref[0], preferred_element_type=F32)
        hdn = gate * jax.nn.sigmoid(gate) * up
        y = jnp.dot(hdn, wd_ref[0], preferred_element_type=F32)
        ybuf[slot] = y.reshape(ROUTE_BLOCK // SUBLANES, SUBLANES, D_MODEL)
        start_copy = scatter(b, slot)
        _for_rows(nv_ref[b], lambda g, u: start_copy(g, u).start())

        @pl.when(b >= 1)
        def _():
            scatter_wait(b - 1, 1 - slot)

        return 0

    lax.fori_loop(fb_ref[e], fb_ref[e + 1], block, 0)

    @pl.when(e == n_experts - 1)
    def _():
        scatter_wait(n_used - 1, lax.rem(n_used - 1, 2))


def _experts(first_block, block_rows, block_base, tok, dst, bn, w_gate, w_up, w_down):
    n = bn.shape[0]
    wmap = lambda e, fb, nv, sb, tk, ds: (e, 0, 0)
    groups = ROUTE_BLOCK // SUBLANES
    grid_spec = pltpu.PrefetchScalarGridSpec(
        num_scalar_prefetch=5,
        grid=(N_EXPERTS,),
        in_specs=[
            pl.BlockSpec(memory_space=pl.ANY),
            pl.BlockSpec((1, D_MODEL, EXPERT_FF), wmap),
            pl.BlockSpec((1, D_MODEL, EXPERT_FF), wmap),
            pl.BlockSpec((1, EXPERT_FF, D_MODEL), wmap),
        ],
        out_specs=pl.BlockSpec(memory_space=pl.ANY),
        scratch_shapes=[
            pltpu.VMEM((2, groups, SUBLANES, D_MODEL), F32),
            pltpu.VMEM((2, groups, SUBLANES, D_MODEL), F32),
            pltpu.SemaphoreType.DMA((2,)),
            pltpu.SemaphoreType.DMA((2,)),
        ],
    )
    return pl.pallas_call(
        _experts_kernel,
        grid_spec=grid_spec,
        out_shape=jax.ShapeDtypeStruct((TOP_K * n, D_MODEL), F32),
        compiler_params=_params(("arbitrary",)),
        name="experts",
    )(first_block, block_rows, block_base, tok, dst, bn, w_gate, w_up, w_down)


def _route_blocks(flat_e):
    a = flat_e.shape[0]
    i32 = jnp.int32
    order = jnp.argsort(flat_e).astype(i32)
    counts = jnp.sum(flat_e[:, None] == jnp.arange(N_EXPERTS, dtype=i32)[None, :], axis=0, dtype=i32)
    padded = (counts + ROUTE_BLOCK - 1) // ROUTE_BLOCK * ROUTE_BLOCK
    pad_end = jnp.cumsum(padded).astype(i32)
    pad_start = pad_end - padded
    seg_start = jnp.cumsum(counts).astype(i32) - counts
    n_blocks = -(-(a + N_EXPERTS * (ROUTE_BLOCK - 1)) // ROUTE_BLOCK)
    blk = jnp.arange(n_blocks, dtype=i32)
    be = jnp.sum(pad_end[None, :] <= (blk * ROUTE_BLOCK)[:, None], axis=1, dtype=i32)
    be = jnp.minimum(be, N_EXPERTS - 1)
    off = blk * ROUTE_BLOCK - pad_start[be]
    rows = jnp.clip(counts[be] - off, 0, ROUTE_BLOCK).astype(i32)
    base = jnp.where(rows > 0, seg_start[be] + off, 0).astype(i32)
    first = jnp.concatenate([pad_start, pad_end[-1:]]) // ROUTE_BLOCK
    tok = order // TOP_K
    dst = (order % TOP_K) * (a // TOP_K) + tok
    return first.astype(i32), rows, base, tok, dst


def _final_kernel(h_ref, y0_ref, y1_ref, wts_ref, nw_ref, o_ref):
    w = wts_ref[...]
    moe = y0_ref[...] * w[:, 0:1] + y1_ref[...] * w[:, 1:2]
    o_ref[...] = _rms(h_ref[...] + moe, nw_ref[...])


def _final(h, y, wts, norm_w, tm):
    n = h.shape[0]
    return pl.pallas_call(
        _final_kernel,
        grid=(n // tm,),
        in_specs=[
            pl.BlockSpec((tm, D_MODEL), lambda i: (i, 0)),
            pl.BlockSpec((tm, D_MODEL), lambda i: (i, 0)),
            pl.BlockSpec((tm, D_MODEL), lambda i: (i + n // tm, 0)),
            pl.BlockSpec((tm, TOP_K), lambda i: (i, 0)),
            pl.BlockSpec((1, D_MODEL), lambda i: (0, 0)),
        ],
        out_specs=pl.BlockSpec((tm, D_MODEL), lambda i: (i, 0)),
        out_shape=jax.ShapeDtypeStruct((n, D_MODEL), F32),
        compiler_params=_params(("parallel",)),
        name="final",
    )(h, y, y, wts, norm_w)


def _rope_tables(length):
    f32 = np.float32
    inv = (f32(1.0) / np.power(f32(ROPE_THETA), np.arange(0, QK_ROPE, 2, dtype=f32) / f32(QK_ROPE))).astype(f32)
    ang = (np.arange(length, dtype=f32)[:, None] * inv[None, :]).astype(f32)
    cos, sin = np.cos(ang).astype(f32), np.sin(ang).astype(f32)
    z32 = np.zeros_like(cos)
    z64 = np.zeros((length, LANES - QK_ROPE), f32)
    c = np.concatenate([cos, cos, z64], axis=1)
    s1 = np.concatenate([-sin, z32, z64], axis=1)
    s2 = np.concatenate([z32, sin, z64], axis=1)
    return c, s1, s2


def kernel(x, meta_tokens, norm_mix_w, w_in, q_norm_w, w_uq, kv_norm_w, w_ukv, w_o_mla, pool_w,
           pool_scale, w_pool_out, w_out, norm_ffn_w, w_router_group, b_router_group,
           w_router_expert, b_router_expert, w_exp_gate, w_exp_up, w_exp_down, final_norm_w):
    assert x.shape == (1, 8192, D_MODEL) and norm_mix_w.shape[0] == 1
    n = x.shape[1]
    xr = x[0]

    w_in_t = jnp.swapaxes(w_in[0], 0, 1).astype(BF16)
    w_q = jnp.pad(w_uq[0].reshape(Q_LORA, N_HEADS, QK_HEAD),
                  ((0, 0), (0, 0), (0, QK_PAD - QK_HEAD))).reshape(Q_LORA, N_HEADS * QK_PAD).astype(BF16)
    w_kv = w_ukv[0].reshape(KV_LORA, N_HEADS, QK_NOPE + V_HEAD)
    w_k = w_kv[:, :, :QK_NOPE].reshape(KV_LORA, N_HEADS * QK_NOPE).astype(BF16)
    w_v = w_kv[:, :, QK_NOPE:].reshape(KV_LORA, N_HEADS * V_HEAD).astype(BF16)
    w_router = jnp.pad(jnp.concatenate([w_router_group[0], w_router_expert[0]], axis=1),
                       ((0, 0), (0, LANES - N_GROUPS - N_EXPERTS))).astype(BF16)
    b_router = jnp.pad(jnp.concatenate([b_router_group[0], b_router_expert[0]]),
                       (0, LANES - N_GROUPS - N_EXPERTS))[None]
    c, s1, s2 = _rope_tables(N_META + n)

    proj, krope = _inproj(xr, norm_mix_w, w_in_t, tm=1024, tn=1024)
    proj_m, krope_m = _inproj(meta_tokens, norm_mix_w, w_in_t, tm=N_META, tn=1024, n_cols=COL_GM)
    q = _qproj(proj, q_norm_w, w_q, c[N_META:], s1[N_META:], s2[N_META:], tm=512)
    k, v = _kvproj(proj, krope, kv_norm_w, w_k, w_v, c[N_META:], s1[N_META:], s2[N_META:], tm=512)
    k_m, v_m = _kvproj(proj_m, krope_m, kv_norm_w, w_k, w_v, c[:N_META], s1[:N_META], s2[:N_META],
                       tm=N_META)
    o = _attention(q, k, v, k_m, v_m, t=512)

    merged = _merge(o, proj, proj_m, w_o_mla[0], pool_w[0], pool_scale, w_pool_out[0], tm=256)
    h, bn, ids, wts = _route(merged, xr, w_out[0], norm_ffn_w, w_router, b_router, tm=512)

    first, rows, base, tok, dst = _route_blocks(ids[:TOP_K].T.reshape(n * TOP_K))
    wts = wts[:TOP_K].T
    y = _experts(first, rows, base, tok, dst, bn, w_exp_gate[0], w_exp_up[0], w_exp_down[0])

    out = _final(h, y, wts, final_norm_w[None], tm=512)
    return out[None]
```

```python
import functools
import math

import jax
import jax.numpy as jnp
import numpy as np
from jax import lax
from jax.experimental import pallas as pl
from jax.experimental.pallas import tpu as pltpu

F32 = jnp.float32
BF16 = jnp.bfloat16

D_MODEL = 2048
N_META = 16
EPS = 1e-6
N_HEADS = 16
QK_NOPE = 128
QK_ROPE = 64
QK_HEAD = QK_NOPE + QK_ROPE
V_HEAD = 128
Q_LORA = 512
KV_LORA = 512
ROPE_THETA = 10000.0
POOL_WINDOWS = (2, 4, 8, 16)
POOL_WIDTH = 1024
POOL_GROUP_DIM = 256
N_GROUPS = 8
EXPERTS_PER_GROUP = 8
N_EXPERTS = 64
TOP_K = 2
EXPERT_FF = 512

LANES = 128
SUBLANES = 8
QK_PAD = 256
V_PAD = 256
COL_CQ, COL_CKV, COL_POOL, COL_GM, COL_GP = 0, 512, 1024, 2048, 4096
ATTN_HEADS = 2
ATTN_UNROLL = 8
ATTN_STRIP = 64
ROUTE_BLOCK = 288
NEG_BIG = -1e30

VMEM_LIMIT = 56 * 1024 * 1024


def _params(sem):
    return pltpu.CompilerParams(dimension_semantics=sem, vmem_limit_bytes=VMEM_LIMIT)


def _rms(x, w):
    return x * lax.rsqrt(jnp.mean(x * x, axis=-1, keepdims=True) + EPS) * w


def _rope128(v, c, s1, s2):
    return v * c + pltpu.roll(v, 96, 1) * s1 + pltpu.roll(v, 32, 1) * s2


def _inproj_kernel(x_ref, nw_ref, wa_ref, wr_ref, wb_ref, o_ref, kr_ref, a_sc):
    j = pl.program_id(1)
    nt = (((1,), (1,)), ((), ()))

    @pl.when(j == 0)
    def _():
        a = _rms(x_ref[...], nw_ref[...]).astype(BF16)
        a_sc[...] = a
        kr_ref[...] = lax.dot_general(a, wr_ref[...], nt, preferred_element_type=F32)
        o_ref[...] = lax.dot_general(a, wa_ref[...], nt, preferred_element_type=F32)

    @pl.when(j > 0)
    def _():
        o_ref[...] = lax.dot_general(a_sc[...], wb_ref[...], nt, preferred_element_type=F32)


def _inproj(x, norm_w, w_t, tm, tn, n_cols=None):
    m, k = x.shape
    n_a = Q_LORA + KV_LORA
    assert tn == n_a
    n = w_t.shape[0] - QK_ROPE if n_cols is None else n_cols
    return pl.pallas_call(
        _inproj_kernel,
        grid=(m // tm, n // tn),
        in_specs=[
            pl.BlockSpec((tm, k), lambda i, j: (i, 0)),
            pl.BlockSpec((1, k), lambda i, j: (0, 0)),
            pl.BlockSpec((tn, k), lambda i, j: (0, 0)),
            pl.BlockSpec((LANES, k), lambda i, j: (n_a // LANES, 0)),
            pl.BlockSpec((pl.Element(tn), pl.Element(k)),
                         lambda i, j: (pl.multiple_of(n_a + QK_ROPE + jnp.maximum(j - 1, 0) * tn, QK_ROPE), 0)),
        ],
        out_specs=[
            pl.BlockSpec((tm, tn), lambda i, j: (i, j)),
            pl.BlockSpec((tm, LANES), lambda i, j: (i, 0)),
        ],
        out_shape=[jax.ShapeDtypeStruct((m, n), F32), jax.ShapeDtypeStruct((m, LANES), F32)],
        scratch_shapes=[pltpu.VMEM((tm, k), BF16)],
        compiler_params=_params(("parallel", "arbitrary")),
        name="inproj",
    )(x, norm_w, w_t, w_t, w_t)


def _qproj_kernel(cq_ref, nw_ref, w_ref, c_ref, s1_ref, s2_ref, q_ref, *, scale):
    a = _rms(cq_ref[...], nw_ref[...]).astype(BF16)
    c, s1, s2 = c_ref[...], s1_ref[...], s2_ref[...]
    for h in range(N_HEADS):
        qh = jnp.dot(a, w_ref[:, h * QK_PAD:(h + 1) * QK_PAD], preferred_element_type=F32) * scale
        q_ref[:, h * QK_PAD:h * QK_PAD + LANES] = qh[:, :LANES].astype(BF16)
        q_ref[:, h * QK_PAD + LANES:(h + 1) * QK_PAD] = _rope128(qh[:, LANES:], c, s1, s2).astype(BF16)


def _qproj(proj, norm_w, w_q, c, s1, s2, tm):
    m = proj.shape[0]
    n = N_HEADS * QK_PAD
    tab = pl.BlockSpec((tm, LANES), lambda i: (i, 0))
    return pl.pallas_call(
        functools.partial(_qproj_kernel, scale=QK_HEAD ** -0.5 * math.log2(math.e)),
        grid=(m // tm,),
        in_specs=[
            pl.BlockSpec((tm, Q_LORA), lambda i: (i, COL_CQ // Q_LORA)),
            pl.BlockSpec((1, Q_LORA), lambda i: (0, 0)),
            pl.BlockSpec((Q_LORA, n), lambda i: (0, 0)),
            tab, tab, tab,
        ],
        out_specs=pl.BlockSpec((tm, n), lambda i: (i, 0)),
        out_shape=jax.ShapeDtypeStruct((m, n), BF16),
        compiler_params=_params(("parallel",)),
        name="qproj",
    )(proj, norm_w, w_q, c, s1, s2)


def _kvproj_kernel(ckv_ref, kr_ref, nw_ref, wk_ref, wv_ref, c_ref, s1_ref, s2_ref, k_ref, v_ref):
    a = _rms(ckv_ref[...], nw_ref[...]).astype(BF16)
    kpe = _rope128(kr_ref[...], c_ref[...], s1_ref[...], s2_ref[...]).astype(BF16)
    ones = jnp.ones((a.shape[0], V_PAD - V_HEAD), BF16)
    for h2 in range(N_HEADS // 2):
        kn = jnp.dot(a, wk_ref[:, h2 * 256:(h2 + 1) * 256], preferred_element_type=F32).astype(BF16)
        vv = jnp.dot(a, wv_ref[:, h2 * 256:(h2 + 1) * 256], preferred_element_type=F32).astype(BF16)
        for d in range(2):
            h = 2 * h2 + d
            k_ref[:, h * QK_PAD:h * QK_PAD + LANES] = kn[:, d * LANES:(d + 1) * LANES]
            k_ref[:, h * QK_PAD + LANES:(h + 1) * QK_PAD] = kpe
            v_ref[:, h * V_PAD:h * V_PAD + V_HEAD] = vv[:, d * LANES:(d + 1) * LANES]
            v_ref[:, h * V_PAD + V_HEAD:(h + 1) * V_PAD] = ones


def _kvproj(proj, krope, norm_w, w_k, w_v, c, s1, s2, tm):
    m = proj.shape[0]
    tab = pl.BlockSpec((tm, LANES), lambda i: (i, 0))
    return pl.pallas_call(
        _kvproj_kernel,
        grid=(m // tm,),
        in_specs=[
            pl.BlockSpec((tm, KV_LORA), lambda i: (i, COL_CKV // KV_LORA)),
            tab,
            pl.BlockSpec((1, KV_LORA), lambda i: (0, 0)),
            pl.BlockSpec((KV_LORA, N_HEADS * QK_NOPE), lambda i: (0, 0)),
            pl.BlockSpec((KV_LORA, N_HEADS * V_HEAD), lambda i: (0, 0)),
            tab, tab, tab,
        ],
        out_specs=[
            pl.BlockSpec((tm, N_HEADS * QK_PAD), lambda i: (i, 0)),
            pl.BlockSpec((tm, N_HEADS * V_PAD), lambda i: (i, 0)),
        ],
        out_shape=[jax.ShapeDtypeStruct((m, N_HEADS * QK_PAD), BF16),
                   jax.ShapeDtypeStruct((m, N_HEADS * V_PAD), BF16)],
        compiler_params=_params(("parallel",)),
        name="kvproj",
    )(proj, krope, norm_w, w_k, w_v, c, s1, s2)


def _attn_kernel(q_ref, qn_ref, k_ref, v_ref, km_ref, vm_ref, o_ref,
                 acc_sc, acc0_sc, m_sc, s_sc, p_sc, al_sc, *, t):
    i = pl.program_id(1)
    heads = range(ATTN_HEADS)
    nt = (((1,), (1,)), ((), ()))
    qs = [q_ref[:, g * QK_PAD:(g + 1) * QK_PAD] for g in heads]

    def chunk(ref, g, width, c):
        return ref[pl.ds(pl.multiple_of(c * t, t), t), g * width:(g + 1) * width]

    def scores(g, c, slot, q=None):
        q = qs[g] if q is None else q
        s_sc[g, slot] = lax.dot_general(q, chunk(k_ref, g, QK_PAD, c), nt, preferred_element_type=F32)

    def first_scores(i_tile, q_of):
        for g in heads:
            scores(g, i_tile, 1, q_of(g))
            scores(g, 0, 0, q_of(g))

    def softmax(g, src, dst, masked):
        for r in range(t // ATTN_STRIP):
            rows = slice(r * ATTN_STRIP, (r + 1) * ATTN_STRIP)
            s = s_sc[g, src, rows, :]
            if masked:
                row = r * ATTN_STRIP + lax.broadcasted_iota(jnp.int32, s.shape, 0)
                s = jnp.where(lax.broadcasted_iota(jnp.int32, s.shape, 1) <= row, s, NEG_BIG)
            m_old = m_sc[g, rows, :]
            m_new = jnp.maximum(m_old, jnp.broadcast_to(jnp.max(s, axis=-1, keepdims=True), m_old.shape))
            p_sc[g, dst, rows, :] = jnp.concatenate(
                [jnp.exp2(s[:, c * LANES:(c + 1) * LANES] - m_new) for c in range(t // LANES)],
                axis=1).astype(BF16)
            al_sc[g, dst, rows, :] = jnp.exp2(m_old - m_new)
            m_sc[g, rows, :] = m_new

    def accumulate(g, slot, c):
        pv = jnp.dot(p_sc[g, slot], chunk(v_ref, g, V_PAD, c), preferred_element_type=F32)
        alpha = al_sc[g, slot]
        acc_sc[g] = jnp.concatenate([alpha, alpha], axis=1) * acc_sc[g] + pv

    def stage(k, src, with_next=True):
        for g in heads:
            if with_next:
                scores(g, jnp.minimum(k + 1, i - 1), 1 - src)
            accumulate(g, src, jnp.where(k == 0, i, k - 1))
            softmax(g, src, 1 - src, masked=False)

    def tile_start(i_tile, q_of):
        first_scores(i_tile, q_of)
        for g in heads:
            s = lax.dot_general(q_of(g), km_ref[:, g * QK_PAD:(g + 1) * QK_PAD], nt, preferred_element_type=F32)
            m0 = jnp.max(s, axis=-1, keepdims=True)
            acc0_sc[g] = jnp.dot(jnp.exp2(s - m0).astype(BF16), vm_ref[:, g * V_PAD:(g + 1) * V_PAD],
                                 preferred_element_type=F32)
            m_sc[g] = jnp.broadcast_to(m0, (t, LANES))
            softmax(g, 1, 0, masked=True)

    def tile_end(last_slot, last_chunk):
        for g in heads:
            accumulate(g, last_slot, last_chunk)
            acc = acc_sc[g]
            o_ref[:, g * V_HEAD:(g + 1) * V_HEAD] = (acc[:, :V_HEAD] / acc[:, V_HEAD:]).astype(o_ref.dtype)
        tile_start(jnp.minimum(i + 1, pl.num_programs(1) - 1),
                   lambda g: qn_ref[:, g * QK_PAD:(g + 1) * QK_PAD])

    @pl.when(i == 0)
    def _():
        tile_start(i, lambda g: qs[g])

    for g in heads:
        acc_sc[g] = acc0_sc[g]

    def run_stages(first, count):
        for u in range(count):
            stage(first + u, u % 2)

    def trip(j, _):
        run_stages(ATTN_UNROLL * j, ATTN_UNROLL)
        return 0

    n_trips = lax.shift_right_logical(i, int(math.log2(ATTN_UNROLL)))
    lax.fori_loop(0, n_trips, trip, 0)
    done = n_trips * ATTN_UNROLL
    width = ATTN_UNROLL // 2
    while width >= 2:
        @pl.when((i & width) != 0)
        def _(done=done, width=width):
            run_stages(done, width)

        done = done + (i & width)
        width //= 2

    odd = (i & 1) == 1

    @pl.when(odd)
    def _():
        stage(i - 1, 0, with_next=False)
        tile_end(1, i - 1)

    @pl.when(jnp.logical_not(odd))
    def _():
        tile_end(0, jnp.maximum(i - 1, 0))


def _attention(q, k, v, k_meta, v_meta, t):
    n = q.shape[0]
    g = ATTN_HEADS
    return pl.pallas_call(
        functools.partial(_attn_kernel, t=t),
        grid=(N_HEADS // g, n // t),
        in_specs=[
            pl.BlockSpec((t, g * QK_PAD), lambda h, i: (i, h)),
            pl.BlockSpec((t, g * QK_PAD), lambda h, i: (jnp.minimum(i + 1, n // t - 1), h)),
            pl.BlockSpec((n, g * QK_PAD), lambda h, i: (0, h)),
            pl.BlockSpec((n, g * V_PAD), lambda h, i: (0, h)),
            pl.BlockSpec((N_META, g * QK_PAD), lambda h, i: (0, h)),
            pl.BlockSpec((N_META, g * V_PAD), lambda h, i: (0, h)),
        ],
        out_specs=pl.BlockSpec((t, g * V_HEAD), lambda h, i: (i, h)),
        out_shape=jax.ShapeDtypeStruct((n, N_HEADS * V_HEAD), BF16),
        scratch_shapes=[
            pltpu.VMEM((g, t, V_PAD), F32), pltpu.VMEM((g, t, V_PAD), F32), pltpu.VMEM((g, t, LANES), F32),
            pltpu.VMEM((g, 2, t, t), F32), pltpu.VMEM((g, 2, t, t), BF16),
            pltpu.VMEM((g, 2, t, LANES), F32),
        ],
        compiler_params=_params(("parallel", "arbitrary")),
        name="attention",
    )(q, q, k, v, k_meta, v_meta)


def _merge_kernel(o_ref, u_ref, halo_ref, um_ref, gm_ref, gp_ref, wo_ref, pw_ref, ps_ref, wpo_ref,
                  out_ref, ext_sc, *, tm):
    i = pl.program_id(0)
    ext_sc[0:N_META, :] = jnp.where(i == 0, um_ref[...], halo_ref[...])
    ext_sc[N_META:, :] = u_ref[...]
    y_pool = jnp.zeros((tm, D_MODEL), F32)
    for g, w in enumerate(POOL_WINDOWS):
        lo, hi = g * POOL_GROUP_DIM, (g + 1) * POOL_GROUP_DIM
        u = ext_sc[N_META:, lo:hi]
        tot = u
        for d in range(1, w):
            tot = tot + ext_sc[N_META - d:N_META - d + tm, lo:hi]
        pooled = tot * (1.0 / w) - u
        mixed = jnp.dot(pooled, pw_ref[g], preferred_element_type=F32) * ps_ref[:, lo:hi]
        y_pool = y_pool + jnp.dot(mixed, wpo_ref[lo:hi, :], preferred_element_type=F32)
    y_mla = jnp.dot(o_ref[...].astype(F32), wo_ref[...], preferred_element_type=F32)
    merged = jax.nn.sigmoid(gm_ref[...]) * y_mla + jax.nn.sigmoid(gp_ref[...]) * y_pool
    out_ref[...] = merged.astype(out_ref.dtype)


def _merge(o, proj, u_meta, w_o, pool_w, pool_scale, w_pool_out, tm):
    n = o.shape[0]
    hb = tm // N_META
    const = lambda i: (0, 0)
    return pl.pallas_call(
        functools.partial(_merge_kernel, tm=tm),
        grid=(n // tm,),
        in_specs=[
            pl.BlockSpec((tm, D_MODEL), lambda i: (i, 0)),
            pl.BlockSpec((tm, POOL_WIDTH), lambda i: (i, COL_POOL // POOL_WIDTH)),
            pl.BlockSpec((N_META, POOL_WIDTH),
                         lambda i: (jnp.maximum(i * hb - 1, 0), COL_POOL // POOL_WIDTH)),
            pl.BlockSpec((N_META, POOL_WIDTH), lambda i: (0, COL_POOL // POOL_WIDTH)),
            pl.BlockSpec((tm, D_MODEL), lambda i: (i, COL_GM // D_MODEL)),
            pl.BlockSpec((tm, D_MODEL), lambda i: (i, COL_GP // D_MODEL)),
            pl.BlockSpec((D_MODEL, D_MODEL), const, pipeline_mode=pl.Buffered(1)),
            pl.BlockSpec((len(POOL_WINDOWS), POOL_GROUP_DIM, POOL_GROUP_DIM), lambda i: (0, 0, 0),
                         pipeline_mode=pl.Buffered(1)),
            pl.BlockSpec((1, POOL_WIDTH), const),
            pl.BlockSpec((POOL_WIDTH, D_MODEL), const, pipeline_mode=pl.Buffered(1)),
        ],
        out_specs=pl.BlockSpec((tm, D_MODEL), lambda i: (i, 0)),
        out_shape=jax.ShapeDtypeStruct((n, D_MODEL), BF16),
        scratch_shapes=[pltpu.VMEM((tm + N_META, POOL_WIDTH), F32)],
        compiler_params=_params(("arbitrary",)),
        name="merge",
    )(o, proj, proj, u_meta, proj, proj, w_o, pool_w, pool_scale, w_pool_out)


def _route_kernel(mg_ref, x_ref, wout_ref, nw_ref, wr_ref, br_ref, h_ref, bn_ref, ids_ref, wts_ref):
    h = x_ref[...] + jnp.dot(mg_ref[...].astype(F32), wout_ref[...], preferred_element_type=F32)
    h_ref[...] = h
    bn = _rms(h, nw_ref[...])
    half = D_MODEL // 2
    lo = lax.bitcast_convert_type(bn[:, :half].astype(BF16).astype(F32), jnp.uint32)
    hi = lax.bitcast_convert_type(bn[:, half:].astype(BF16).astype(F32), jnp.uint32)
    bn_ref[...] = (hi & jnp.uint32(0xFFFF0000)) | lax.shift_right_logical(lo, jnp.uint32(16))
    logits = jnp.dot(bn.astype(BF16), wr_ref[...], preferred_element_type=F32) + br_ref[...]
    lt = logits.T
    tm = lt.shape[1]
    sub = lax.broadcasted_iota(jnp.int32, (SUBLANES, tm), 0)
    sub_f = sub.astype(F32)

    def first_max(vals):
        vmax = jnp.max(vals, axis=0, keepdims=True)
        idx = jnp.min(jnp.where(vals == vmax, sub_f, float(SUBLANES)), axis=0, keepdims=True)
        return vmax, idx.astype(jnp.int32)

    glog = lt[:N_GROUPS, :]
    gmax, gidx = first_max(glog)
    p_g = 1.0 / jnp.sum(jnp.exp(glog - gmax), axis=0, keepdims=True)
    e_in = jnp.zeros((EXPERTS_PER_GROUP, tm), F32)
    for g in range(N_GROUPS):
        lo = N_GROUPS + g * EXPERTS_PER_GROUP
        e_in = jnp.where(gidx == g, lt[lo:lo + EXPERTS_PER_GROUP, :], e_in)
    v1, i1 = first_max(e_in)
    v2, i2 = first_max(jnp.where(sub == i1, -jnp.inf, e_in))
    e2 = jnp.exp(v2 - v1)
    den = 1.0 + e2
    w1 = p_g * (1.0 / den)
    w2 = p_g * (e2 / den)
    base = gidx * EXPERTS_PER_GROUP
    ids_ref[...] = jnp.where(sub == 0, base + i1, jnp.where(sub == 1, base + i2, 0))
    wts_ref[...] = jnp.where(sub == 0, w1, jnp.where(sub == 1, w2, 0.0))


def _route(merged, x, w_out, norm_w, w_router, b_router, tm):
    n = x.shape[0]
    const = lambda i: (0, 0)
    row = lambda width: pl.BlockSpec((tm, width), lambda i: (i, 0))
    return pl.pallas_call(
        _route_kernel,
        grid=(n // tm,),
        in_specs=[
            row(D_MODEL), row(D_MODEL),
            pl.BlockSpec((D_MODEL, D_MODEL), const, pipeline_mode=pl.Buffered(1)),
            pl.BlockSpec((1, D_MODEL), const),
            pl.BlockSpec((D_MODEL, LANES), const),
            pl.BlockSpec((1, LANES), const),
        ],
        out_specs=[row(D_MODEL), row(D_MODEL // 2),
                   pl.BlockSpec((SUBLANES, tm), lambda i: (0, i)), pl.BlockSpec((SUBLANES, tm), lambda i: (0, i))],
        out_shape=[jax.ShapeDtypeStruct((n, D_MODEL), F32), jax.ShapeDtypeStruct((n, D_MODEL // 2), jnp.uint32),
                   jax.ShapeDtypeStruct((SUBLANES, n), jnp.int32), jax.ShapeDtypeStruct((SUBLANES, n), F32)],
        compiler_params=_params(("parallel",)),
        name="route",
    )(merged, x, w_out, norm_w, w_router, b_router)


def _for_rows(n, fn):
    shift = int(math.log2(SUBLANES))
    n_groups = lax.shift_right_logical(n, shift)

    def group(g, _):
        for u in range(SUBLANES):
            fn(g, u)
        return 0

    def single(r, _):
        fn(n_groups, r)
        return 0

    lax.fori_loop(0, n_groups, group, 0)
    lax.fori_loop(0, n - n_groups * SUBLANES, single, 0)


def _experts_kernel(fb_ref, nv_ref, sb_ref, tok_ref, dst_ref, bn_hbm, wg_ref, wu_ref, wd_ref, y_hbm,
                    xbuf, ybuf, gsem, ssem):
    e = pl.program_id(0)
    n_experts = pl.num_programs(0)
    n_used = fb_ref[n_experts]

    def gather(blk, slt):
        base = sb_ref[blk]

        def copy(g, u):
            tok = tok_ref[base + g * SUBLANES + u]
            return pltpu.make_async_copy(bn_hbm.at[pl.ds(tok, 1)], xbuf.at[slt, g, pl.ds(u, 1)],
                                         gsem.at[slt])
        return copy

    def scatter(blk, slt):
        base = sb_ref[blk]

        def copy(g, u):
            dst = dst_ref[base + g * SUBLANES + u]
            return pltpu.make_async_copy(ybuf.at[slt, g, pl.ds(u, 1)], y_hbm.at[pl.ds(dst, 1)],
                                         ssem.at[slt])
        return copy

    def gather_start(blk, slt):
        copy = gather(blk, slt)
        _for_rows(nv_ref[blk], lambda g, u: copy(g, u).start(priority=1))

    def scatter_wait(blk, slt):
        copy = scatter(blk, slt)
        _for_rows(nv_ref[blk], lambda g, u: copy(g, u).wait())

    @pl.when(e == 0)
    def _():
        xbuf[...] = jnp.zeros(xbuf.shape, xbuf.dtype)
        gather_start(0, 0)

    def block(b, _):
        slot = lax.rem(b, 2)

        @pl.when(b + 1 < n_used)
        def _():
            gather_start(b + 1, 1 - slot)

        wait_copy = gather(b, slot)
        _for_rows(nv_ref[b], lambda g, u: wait_copy(g, u).wait())
        xw = xbuf[slot].reshape(ROUTE_BLOCK, D_MODEL // 2)
        x = jnp.concatenate(
            [lax.bitcast_convert_type(lax.shift_left(xw, jnp.uint32(16)), F32),
             lax.bitcast_convert_type(xw & jnp.uint32(0xFFFF0000), F32)], axis=1)
        gate = jnp.dot(x, wg_ref[0], preferred_element_type=F32)
        up = jnp.dot(x, wu_ref[0], preferred_element_type=F32)
        hdn = gate * jax.nn.sigmoid(gate) * up
        y = jnp.dot(hdn, wd_ref[0], preferred_element_type=F32)
        ybuf[slot] = y.reshape(ROUTE_BLOCK // SUBLANES, SUBLANES, D_MODEL)
        start_copy = scatter(b, slot)
        _for_rows(nv_ref[b], lambda g, u: start_copy(g, u).start())

        @pl.when(b >= 1)
        def _():
            scatter_wait(b - 1, 1 - slot)

        return 0

    lax.fori_loop(fb_ref[e], fb_ref[e + 1], block, 0)

    @pl.when(e == n_experts - 1)
    def _():
        scatter_wait(n_used - 1, lax.rem(n_used - 1, 2))


def _experts(first_block, block_rows, block_base, tok, dst, bn, w_gate, w_up, w_down):
    n = bn.shape[0]
    wmap = lambda e, fb, nv, sb, tk, ds: (e, 0, 0)
    groups = ROUTE_BLOCK // SUBLANES
    grid_spec = pltpu.PrefetchScalarGridSpec(
        num_scalar_prefetch=5,
        grid=(N_EXPERTS,),
        in_specs=[
            pl.BlockSpec(memory_space=pl.ANY),
            pl.BlockSpec((1, D_MODEL, EXPERT_FF), wmap),
            pl.BlockSpec((1, D_MODEL, EXPERT_FF), wmap),
            pl.BlockSpec((1, EXPERT_FF, D_MODEL), wmap),
        ],
        out_specs=pl.BlockSpec(memory_space=pl.ANY),
        scratch_shapes=[
            pltpu.VMEM((2, groups, SUBLANES, D_MODEL // 2), jnp.uint32),
            pltpu.VMEM((2, groups, SUBLANES, D_MODEL), F32),
            pltpu.SemaphoreType.DMA((2,)),
            pltpu.SemaphoreType.DMA((2,)),
        ],
    )
    return pl.pallas_call(
        _experts_kernel,
        grid_spec=grid_spec,
        out_shape=jax.ShapeDtypeStruct((TOP_K * n, D_MODEL), F32),
        compiler_params=_params(("arbitrary",)),
        name="experts",
    )(first_block, block_rows, block_base, tok, dst, bn, w_gate, w_up, w_down)


def _route_blocks(flat_e):
    a = flat_e.shape[0]
    i32 = jnp.int32
    order = jnp.argsort(flat_e).astype(i32)
    counts = jnp.sum(flat_e[:, None] == jnp.arange(N_EXPERTS, dtype=i32)[None, :], axis=0, dtype=i32)
    padded = (counts + ROUTE_BLOCK - 1) // ROUTE_BLOCK * ROUTE_BLOCK
    pad_end = jnp.cumsum(padded).astype(i32)
    pad_start = pad_end - padded
    seg_start = jnp.cumsum(counts).astype(i32) - counts
    n_blocks = -(-(a + N_EXPERTS * (ROUTE_BLOCK - 1)) // ROUTE_BLOCK)
    blk = jnp.arange(n_blocks, dtype=i32)
    be = jnp.sum(pad_end[None, :] <= (blk * ROUTE_BLOCK)[:, None], axis=1, dtype=i32)
    be = jnp.minimum(be, N_EXPERTS - 1)
    off = blk * ROUTE_BLOCK - pad_start[be]
    rows = jnp.clip(counts[be] - off, 0, ROUTE_BLOCK).astype(i32)
    base = jnp.where(rows > 0, seg_start[be] + off, 0).astype(i32)
    first = jnp.concatenate([pad_start, pad_end[-1:]]) // ROUTE_BLOCK
    tok = order // TOP_K
    dst = (order % TOP_K) * (a // TOP_K) + tok
    return first.astype(i32), rows, base, tok, dst


def _final_kernel(h_ref, y0_ref, y1_ref, wts_ref, nw_ref, o_ref):
    w = wts_ref[...]
    moe = y0_ref[...] * w[:, 0:1] + y1_ref[...] * w[:, 1:2]
    o_ref[...] = _rms(h_ref[...] + moe, nw_ref[...])


def _final(h, y, wts, norm_w, tm):
    n = h.shape[0]
    return pl.pallas_call(
        _final_kernel,
        grid=(n // tm,),
        in_specs=[
            pl.BlockSpec((tm, D_MODEL), lambda i: (i, 0)),
            pl.BlockSpec((tm, D_MODEL), lambda i: (i, 0)),
            pl.BlockSpec((tm, D_MODEL), lambda i: (i + n // tm, 0)),
            pl.BlockSpec((tm, TOP_K), lambda i: (i, 0)),
            pl.BlockSpec((1, D_MODEL), lambda i: (0, 0)),
        ],
        out_specs=pl.BlockSpec((tm, D_MODEL), lambda i: (i, 0)),
        out_shape=jax.ShapeDtypeStruct((n, D_MODEL), F32),
        compiler_params=_params(("parallel",)),
        name="final",
    )(h, y, y, wts, norm_w)


def _rope_tables(length):
    f32 = np.float32
    inv = (f32(1.0) / np.power(f32(ROPE_THETA), np.arange(0, QK_ROPE, 2, dtype=f32) / f32(QK_ROPE))).astype(f32)
    ang = (np.arange(length, dtype=f32)[:, None] * inv[None, :]).astype(f32)
    cos, sin = np.cos(ang).astype(f32), np.sin(ang).astype(f32)
    z32 = np.zeros_like(cos)
    z64 = np.zeros((length, LANES - QK_ROPE), f32)
    c = np.concatenate([cos, cos, z64], axis=1)
    s1 = np.concatenate([-sin, z32, z64], axis=1)
    s2 = np.concatenate([z32, sin, z64], axis=1)
    return c, s1, s2


def kernel(x, meta_tokens, norm_mix_w, w_in, q_norm_w, w_uq, kv_norm_w, w_ukv, w_o_mla, pool_w,
           pool_scale, w_pool_out, w_out, norm_ffn_w, w_router_group, b_router_group,
           w_router_expert, b_router_expert, w_exp_gate, w_exp_up, w_exp_down, final_norm_w):
    assert x.shape == (1, 8192, D_MODEL) and norm_mix_w.shape[0] == 1
    n = x.shape[1]
    xr = x[0]

    w_in_t = jnp.swapaxes(w_in[0], 0, 1).astype(BF16)
    w_q = jnp.pad(w_uq[0].reshape(Q_LORA, N_HEADS, QK_HEAD),
                  ((0, 0), (0, 0), (0, QK_PAD - QK_HEAD))).reshape(Q_LORA, N_HEADS * QK_PAD).astype(BF16)
    w_kv = w_ukv[0].reshape(KV_LORA, N_HEADS, QK_NOPE + V_HEAD)
    w_k = w_kv[:, :, :QK_NOPE].reshape(KV_LORA, N_HEADS * QK_NOPE).astype(BF16)
    w_v = w_kv[:, :, QK_NOPE:].reshape(KV_LORA, N_HEADS * V_HEAD).astype(BF16)
    w_router = jnp.pad(jnp.concatenate([w_router_group[0], w_router_expert[0]], axis=1),
                       ((0, 0), (0, LANES - N_GROUPS - N_EXPERTS))).astype(BF16)
    b_router = jnp.pad(jnp.concatenate([b_router_group[0], b_router_expert[0]]),
                       (0, LANES - N_GROUPS - N_EXPERTS))[None]
    c, s1, s2 = _rope_tables(N_META + n)

    proj, krope = _inproj(xr, norm_mix_w, w_in_t, tm=1024, tn=1024)
    proj_m, krope_m = _inproj(meta_tokens, norm_mix_w, w_in_t, tm=N_META, tn=1024, n_cols=COL_GM)
    q = _qproj(proj, q_norm_w, w_q, c[N_META:], s1[N_META:], s2[N_META:], tm=512)
    k, v = _kvproj(proj, krope, kv_norm_w, w_k, w_v, c[N_META:], s1[N_META:], s2[N_META:], tm=512)
    k_m, v_m = _kvproj(proj_m, krope_m, kv_norm_w, w_k, w_v, c[:N_META], s1[:N_META], s2[:N_META],
                       tm=N_META)
    o = _attention(q, k, v, k_m, v_m, t=512)

    merged = _merge(o, proj, proj_m, w_o_mla[0], pool_w[0], pool_scale, w_pool_out[0], tm=256)
    h, bn, ids, wts = _route(merged, xr, w_out[0], norm_ffn_w, w_router, b_router, tm=512)

    first, rows, base, tok, dst = _route_blocks(ids[:TOP_K].T.reshape(n * TOP_K))
    wts = wts[:TOP_K].T
    y = _experts(first, rows, base, tok, dst, bn, w_exp_gate[0], w_exp_up[0], w_exp_down[0])

    out = _final(h, y, wts, final_norm_w[None], tm=512)
    return out[None]
```
